```python
import jax, jax.numpy as jnp
from jax import lax
import numpy as np

D_MODEL = 1024
BATCH = 16
SEQ = 2048
DEPTH = 2
DEC_BATCH = 128
DEC_SEQ = 8
PAST_LEN = 8192
PAGE_SIZE = 128

D_MIX = D_MODEL
NORM_EPS = 1e-6
MASK_VALUE = -1e30
MIN_FORGET = 1e-20
MLA_HEADS = D_MODEL // 128
MLA_NOPE = 64
MLA_ROPE = 32
MLA_V = 64
MLA_Q_RANK = 3 * D_MODEL // 8
MLA_KV_RANK = D_MODEL // 4
MLA_WIDTH = MLA_HEADS * MLA_V
MLA_SCALE = (MLA_NOPE + MLA_ROPE) ** -0.5
ROPE_BASE = 10000.0
Q_BLOCK = 128
RWKV_WIDTH = D_MIX // 4
RWKV_HEAD = 64
RWKV_HEADS = RWKV_WIDTH // RWKV_HEAD
DECAY_RANK = 64
ICLR_RANK = 64
VRES_RANK = 32
RWKV_GN_EPS = 64e-5
HGRN_WIDTH = D_MIX - MLA_WIDTH - RWKV_WIDTH
HGRN_HEADS = 4
HGRN_HEAD = HGRN_WIDTH // HGRN_HEADS
HGRN_CHUNK = 64
MLA_IN = MLA_Q_RANK + MLA_KV_RANK + MLA_ROPE
RWKV_IN = 3 * RWKV_WIDTH + DECAY_RANK + ICLR_RANK
HGRN_IN = 3 * HGRN_WIDTH
IN_COLS = MLA_IN + RWKV_IN + HGRN_IN + D_MIX

kernel_name = 'hymba_mla_rwkv7_hgrn2_step'


def rms_norm(x, g, eps=NORM_EPS):
    xf = x.astype(jnp.float32)
    y = xf * lax.rsqrt(jnp.mean(xf * xf, axis=-1, keepdims=True) + eps)
    return (y * g.astype(jnp.float32)).astype(x.dtype)


def rope(x, pos):
    half = MLA_ROPE // 2
    inv_freq = ROPE_BASE ** (-jnp.arange(half, dtype=jnp.float32) / half)
    ang = pos.astype(jnp.float32)[:, None] * inv_freq[None, :]
    cos = jnp.cos(ang)[None, :, None, :]
    sin = jnp.sin(ang)[None, :, None, :]
    xf = x.astype(jnp.float32)
    x1, x2 = xf[..., :half], xf[..., half:]
    return jnp.concatenate([x1 * cos - x2 * sin, x2 * cos + x1 * sin], axis=-1).astype(x.dtype)


def mla_attend_prompt(q_nope, q_pe, ckv, kpe, w_kv_b):
    B, T, H, _ = q_nope.shape
    kv = jnp.einsum('bsr,rhd->bshd', ckv, w_kv_b)
    k_nope, v = kv[..., :MLA_NOPE], kv[..., MLA_NOPE:]
    nb = T // Q_BLOCK
    qn = q_nope.reshape(B, nb, Q_BLOCK, H, MLA_NOPE).transpose(1, 0, 2, 3, 4)
    qp = q_pe.reshape(B, nb, Q_BLOCK, H, MLA_ROPE).transpose(1, 0, 2, 3, 4)
    key_pos = jnp.arange(T)

    def block(args):
        qn_b, qp_b, start = args
        s = jnp.einsum('bqhd,bkhd->bhqk', qn_b, k_nope) + jnp.einsum('bqhr,bkr->bhqk', qp_b, kpe)
        q_pos = start + jnp.arange(Q_BLOCK)
        s = jnp.where(key_pos[None, :] <= q_pos[:, None], s.astype(jnp.float32) * MLA_SCALE, MASK_VALUE)
        pr = jax.nn.softmax(s, axis=-1).astype(v.dtype)
        return jnp.einsum('bhqk,bkhd->bqhd', pr, v)

    o = lax.map(block, (qn, qp, jnp.arange(nb) * Q_BLOCK))
    return o.transpose(1, 0, 2, 3, 4).reshape(B, T, H, MLA_V)


def mla_attend_sample(q_nope, q_pe, ckv, kpe, w_kv_b, cache_ckv, cache_kpe, page_table, l):
    B, T = q_nope.shape[:2]
    w_uk, w_uv = w_kv_b[..., :MLA_NOPE], w_kv_b[..., MLA_NOPE:]
    q_lat = jnp.einsum('bthn,rhn->bthr', q_nope, w_uk)
    past_ckv = cache_ckv[l, page_table].reshape(B, -1, MLA_KV_RANK)
    past_kpe = cache_kpe[l, page_table].reshape(B, -1, MLA_ROPE)
    n_past = past_ckv.shape[1]
    s_past = jnp.einsum('bthr,bkr->bhtk', q_lat, past_ckv) + jnp.einsum('bthr,bkr->bhtk', q_pe, past_kpe)
    s_new = jnp.einsum('bthr,bkr->bhtk', q_lat, ckv) + jnp.einsum('bthr,bkr->bhtk', q_pe, kpe)
    causal = jnp.arange(T)[:, None] >= jnp.arange(T)[None, :]
    s = jnp.concatenate([s_past.astype(jnp.float32), s_new.astype(jnp.float32)], axis=-1) * MLA_SCALE
    mask = jnp.concatenate([jnp.ones((T, n_past), dtype=bool), causal], axis=-1)
    pr = jax.nn.softmax(jnp.where(mask, s, MASK_VALUE), axis=-1)
    lat = (jnp.einsum('bhtk,bkr->bthr', pr[..., :n_past].astype(past_ckv.dtype), past_ckv)
           + jnp.einsum('bhtk,bkr->bthr', pr[..., n_past:].astype(ckv.dtype), ckv))
    return jnp.einsum('bthr,rhv->bthv', lat, w_uv)


def mla_branch(p, pos, l, cache, P):
    B, T, _ = p.shape
    cq, ckv, kpe = jnp.split(p, [MLA_Q_RANK, MLA_Q_RANK + MLA_KV_RANK], axis=-1)
    cq = rms_norm(cq, P['mla_q_norm_g'][l])
    ckv = rms_norm(ckv, P['mla_kv_norm_g'][l])
    q = (cq @ P['mla_w_q_b'][l]).reshape(B, T, MLA_HEADS, MLA_NOPE + MLA_ROPE)
    q_nope = q[..., :MLA_NOPE]
    q_pe = rope(q[..., MLA_NOPE:], pos)
    kpe = rope(kpe[:, :, None, :], pos)[:, :, 0, :]
    w_kv_b = P['mla_w_kv_b'][l].reshape(MLA_KV_RANK, MLA_HEADS, MLA_NOPE + MLA_V)
    if cache is None:
        o = mla_attend_prompt(q_nope, q_pe, ckv, kpe, w_kv_b)
    else:
        cache_ckv, cache_kpe, page_table = cache
        o = mla_attend_sample(q_nope, q_pe, ckv, kpe, w_kv_b, cache_ckv, cache_kpe, page_table, l)
    return o.reshape(B, T, MLA_WIDTH).astype(p.dtype), ckv, kpe


def rwkv_branch(p, shift_prev, s0, v_first, l, P):
    B, T, _ = p.shape
    f32 = jnp.float32
    prev = jnp.concatenate([shift_prev[:, None, :].astype(p.dtype), p[:, :-1]], axis=1)
    pm = p + P['rwkv_mu'][l] * (prev - p)
    r, k, v, xw, xa = jnp.split(pm, [RWKV_WIDTH, 2 * RWKV_WIDTH, 3 * RWKV_WIDTH, 3 * RWKV_WIDTH + DECAY_RANK], axis=-1)
    w_log = -jax.nn.softplus(-(P['rwkv_w0'][l] + jnp.tanh(xw) @ P['rwkv_w_decay_b'][l])) - 0.5
    decay = jnp.exp(-jnp.exp(w_log.astype(f32)))
    a = jax.nn.sigmoid(P['rwkv_a0'][l] + xa @ P['rwkv_w_iclr_b'][l])
    if l == 0:
        v_first = v
    else:
        mix = jax.nn.sigmoid(P['rwkv_v0'][l - 1] + (v @ P['rwkv_w_vres_a'][l - 1]) @ P['rwkv_w_vres_b'][l - 1])
        v = v + (v_first - v) * mix
    kk = k * P['rwkv_k_k'][l]
    k = k * (1 + (a - 1) * P['rwkv_k_a'][l])

    def heads(t):
        return t.astype(f32).reshape(B, T, RWKV_HEADS, RWKV_HEAD)

    r_h, w_h, k_h, v_h, kk_h, a_h = map(heads, (r, decay, k, v, kk, a))
    kk_h = kk_h / jnp.maximum(jnp.sqrt(jnp.sum(kk_h * kk_h, axis=-1, keepdims=True)), 1e-12)

    def step(S, inp):
        r_t, w_t, k_t, v_t, kk_t, a_t = inp
        sa = jnp.einsum('bhvk,bhk->bhv', S, -kk_t)
        S = (S * w_t[:, :, None, :] + sa[..., None] * (kk_t * a_t)[:, :, None, :]
             + v_t[..., None] * k_t[:, :, None, :])
        return S, jnp.einsum('bhvk,bhk->bhv', S, r_t)

    xs = tuple(t.transpose(1, 0, 2, 3) for t in (r_h, w_h, k_h, v_h, kk_h, a_h))
    S, o = lax.scan(step, s0.astype(f32), xs)
    o = o.transpose(1, 0, 2, 3)
    mean = jnp.mean(o, axis=-1, keepdims=True)
    var = jnp.mean(jnp.square(o - mean), axis=-1, keepdims=True)
    o = ((o - mean) * lax.rsqrt(var + RWKV_GN_EPS)).reshape(B, T, RWKV_WIDTH)
    o = o * P['rwkv_gn_g'][l] + P['rwkv_gn_b'][l]
    bonus = jnp.sum(r_h * k_h * P['rwkv_r_k'][l], axis=-1, keepdims=True) * v_h
    o = o + bonus.reshape(B, T, RWKV_WIDTH)
    return o.astype(p.dtype), S, p[:, -1], v_first


def gla_chunked(q, k, v, g, s0):
    B, T, H, _ = q.shape
    C = min(HGRN_CHUNK, T)
    n = -(-T // C)
    pad = n * C - T

    def prep(a):
        a = jnp.pad(a, ((0, 0), (0, pad), (0, 0), (0, 0)))
        return a.reshape(B, n, C, H, a.shape[-1]).transpose(1, 0, 3, 2, 4)

    qc, kc, vc, gc = map(prep, (q, k, v, g))
    tri = jnp.tril(jnp.ones((C, C), dtype=bool))[:, :, None]

    def step(S, inp):
        qb, kb, vb, gb = inp
        b = jnp.cumsum(gb, axis=2)
        diff = b[:, :, :, None, :] - b[:, :, None, :, :]
        dec = jnp.where(tri, jnp.exp(jnp.where(tri, diff, 0.0)), 0.0)
        att = jnp.sum(qb[:, :, :, None, :] * kb[:, :, None, :, :] * dec, axis=-1)
        o = (jnp.einsum('bhts,bhsv->bhtv', att, vb)
             + jnp.einsum('bhtd,bhdv->bhtv', qb * jnp.exp(b), S))
        b_last = b[:, :, -1:, :]
        S = (S * jnp.exp(b_last[:, :, 0, :])[..., None]
             + jnp.einsum('bhsd,bhsv->bhdv', kb * jnp.exp(b_last - b), vb))
        return S, o

    S, o = lax.scan(step, s0, (qc, kc, vc, gc))
    o = o.transpose(1, 0, 3, 2, 4).reshape(B, n * C, H, v.shape[-1])[:, :T]
    return o, S


def hgrn_branch(p, s0, lb, norm_g):
    B, T, _ = p.shape
    q_raw, f_raw, i = jnp.split(p.astype(jnp.float32), 3, axis=-1)
    q = jax.nn.silu(q_raw) * HGRN_HEAD ** -0.5
    sig = jax.nn.sigmoid(f_raw)
    f = lb + (1 - lb) * sig
    log_f = jnp.log(jnp.maximum(f, MIN_FORGET))
    k = (1 - lb) * (1 - sig)

    def heads(t):
        return t.reshape(B, T, HGRN_HEADS, HGRN_HEAD)

    o, S = gla_chunked(heads(q), heads(k), heads(i), heads(log_f), s0.astype(jnp.float32))
    o = rms_norm(o, norm_g).reshape(B, T, HGRN_WIDTH)
    return o.astype(p.dtype), S


def run_trunk(x, c, pos, shift0, rwkv_s0, hgrn_s0, cache, P):
    lb_p = jax.nn.softmax(P['hgrn_lb_raw'].astype(jnp.float32), axis=0)
    lb_all = jnp.cumsum(lb_p, axis=0) - lb_p[0]
    cs = jax.nn.silu(c)
    v_first = None
    ckvs, kpes, rs, shs, hs = [], [], [], [], []
    for l in range(DEPTH):
        ada = cs @ P['w_ada'][l] + P['b_ada'][l]
        a_shift, a_scale, a_gate = jnp.split(ada, 3, axis=-1)
        h = rms_norm(x, P['pre_norm_g'][l]) * (1 + a_scale[:, None, :]) + a_shift[:, None, :]
        proj = h @ P['w_in'][l]
        p_mla, p_rwkv, p_hgrn, gate = jnp.split(
            proj, [MLA_IN, MLA_IN + RWKV_IN, MLA_IN + RWKV_IN + HGRN_IN], axis=-1)
        o_mla, ckv, kpe = mla_branch(p_mla, pos, l, cache, P)
        o_rwkv, s_r, sh, v_first = rwkv_branch(p_rwkv, shift0[l], rwkv_s0[l], v_first, l, P)
        o_hgrn, s_h = hgrn_branch(p_hgrn, hgrn_s0[l], lb_all[l], P['hgrn_norm_g'][l])
        o = jnp.concatenate([o_mla, o_rwkv, o_hgrn], axis=-1) * jax.nn.silu(gate)
        y = rms_norm(o @ P['w_out'][l], P['post_norm_g'][l])
        x = x + a_gate[:, None, :] * y
        ckvs.append(ckv)
        kpes.append(kpe)
        rs.append(s_r)
        shs.append(sh)
        hs.append(s_h)
    return x, jnp.stack(ckvs), jnp.stack(kpes), jnp.stack(rs), jnp.stack(shs), jnp.stack(hs)


def setup_inputs(seed: int = 0) -> dict:
    key = jax.random.key(seed)
    ks = jax.random.split(key, 48)
    counter = iter(range(48))
    f32 = jnp.float32

    def nk():
        return ks[next(counter)]

    def nrm(shape, scale):
        return jax.random.normal(nk(), shape, f32) * scale

    n_pages = PAST_LEN // PAGE_SIZE
    n_used = DEC_BATCH * n_pages
    n_pool = n_used + (n_used + 3) // 4
    page_table = jax.random.permutation(nk(), n_pool)[:n_used].reshape(DEC_BATCH, n_pages).astype(jnp.int32)
    D = D_MODEL
    return {
        'x_prompt': nrm((BATCH, SEQ, D), 1.0),
        'x_sample': nrm((DEC_BATCH, DEC_SEQ, D), 1.0),
        'c_prompt': nrm((BATCH, D), 1.0),
        'c_sample': nrm((DEC_BATCH, D), 1.0),
        'cache_ckv': nrm((DEPTH, n_pool, PAGE_SIZE, MLA_KV_RANK), 1.0),
        'cache_kpe': nrm((DEPTH, n_pool, PAGE_SIZE, MLA_ROPE), 1.0),
        'page_table': page_table,
        'state_rwkv': nrm((DEPTH, DEC_BATCH, RWKV_HEADS, RWKV_HEAD, RWKV_HEAD), 0.3),
        'state_rwkv_shift': nrm((DEPTH, DEC_BATCH, RWKV_IN), 1.0),
        'state_hgrn': nrm((DEPTH, DEC_BATCH, HGRN_HEADS, HGRN_HEAD, HGRN_HEAD), 0.3),
        'w_ada': nrm((DEPTH, D, 3 * D), 0.5 * D ** -0.5),
        'b_ada': nrm((DEPTH, 3 * D), 0.02),
        'pre_norm_g': 1.0 + nrm((DEPTH, D), 0.05),
        'post_norm_g': 1.0 + nrm((DEPTH, D), 0.05),
        'w_in': nrm((DEPTH, D, IN_COLS), D ** -0.5),
        'mla_q_norm_g': 1.0 + nrm((DEPTH, MLA_Q_RANK), 0.05),
        'mla_w_q_b': nrm((DEPTH, MLA_Q_RANK, MLA_HEADS * (MLA_NOPE + MLA_ROPE)), MLA_Q_RANK ** -0.5),
        'mla_kv_norm_g': 1.0 + nrm((DEPTH, MLA_KV_RANK), 0.05),
        'mla_w_kv_b': nrm((DEPTH, MLA_KV_RANK, MLA_HEADS * (MLA_NOPE + MLA_V)), MLA_KV_RANK ** -0.5),
        'rwkv_mu': jax.random.uniform(nk(), (DEPTH, RWKV_IN), f32),
        'rwkv_w0': jax.random.uniform(nk(), (DEPTH, RWKV_WIDTH), f32, minval=-6.0, maxval=-1.0),
        'rwkv_w_decay_b': nrm((DEPTH, DECAY_RANK, RWKV_WIDTH), DECAY_RANK ** -0.5),
        'rwkv_a0': nrm((DEPTH, RWKV_WIDTH), 0.1),
        'rwkv_w_iclr_b': nrm((DEPTH, ICLR_RANK, RWKV_WIDTH), ICLR_RANK ** -0.5),
        'rwkv_k_k': 0.85 + nrm((DEPTH, RWKV_WIDTH), 0.05),
        'rwkv_k_a': 1.0 + nrm((DEPTH, RWKV_WIDTH), 0.05),
        'rwkv_r_k': nrm((DEPTH, RWKV_HEADS, RWKV_HEAD), 0.1),
        'rwkv_gn_g': 1.0 + nrm((DEPTH, RWKV_WIDTH), 0.05),
        'rwkv_gn_b': nrm((DEPTH, RWKV_WIDTH), 0.02),
        'rwkv_v0': 1.0 + nrm((DEPTH - 1, RWKV_WIDTH), 0.1),
        'rwkv_w_vres_a': nrm((DEPTH - 1, RWKV_WIDTH, VRES_RANK), RWKV_WIDTH ** -0.5),
        'rwkv_w_vres_b': nrm((DEPTH - 1, VRES_RANK, RWKV_WIDTH), VRES_RANK ** -0.5),
        'hgrn_lb_raw': nrm((DEPTH, HGRN_WIDTH), 1.0),
        'hgrn_norm_g': 1.0 + nrm((DEPTH, HGRN_HEAD), 0.05),
        'w_out': nrm((DEPTH, D_MIX, D), D_MIX ** -0.5),
    }


def reference(x_prompt, x_sample, c_prompt, c_sample, cache_ckv, cache_kpe, page_table,
              state_rwkv, state_rwkv_shift, state_hgrn,
              w_ada, b_ada, pre_norm_g, post_norm_g, w_in,
              mla_q_norm_g, mla_w_q_b, mla_kv_norm_g, mla_w_kv_b,
              rwkv_mu, rwkv_w0, rwkv_w_decay_b, rwkv_a0, rwkv_w_iclr_b, rwkv_k_k, rwkv_k_a,
              rwkv_r_k, rwkv_gn_g, rwkv_gn_b, rwkv_v0, rwkv_w_vres_a, rwkv_w_vres_b,
              hgrn_lb_raw, hgrn_norm_g, w_out):
    P = {
        'w_ada': w_ada, 'b_ada': b_ada, 'pre_norm_g': pre_norm_g, 'post_norm_g': post_norm_g,
        'w_in': w_in, 'mla_q_norm_g': mla_q_norm_g, 'mla_w_q_b': mla_w_q_b,
        'mla_kv_norm_g': mla_kv_norm_g, 'mla_w_kv_b': mla_w_kv_b,
        'rwkv_mu': rwkv_mu, 'rwkv_w0': rwkv_w0, 'rwkv_w_decay_b': rwkv_w_decay_b,
        'rwkv_a0': rwkv_a0, 'rwkv_w_iclr_b': rwkv_w_iclr_b, 'rwkv_k_k': rwkv_k_k,
        'rwkv_k_a': rwkv_k_a, 'rwkv_r_k': rwkv_r_k, 'rwkv_gn_g': rwkv_gn_g, 'rwkv_gn_b': rwkv_gn_b,
        'rwkv_v0': rwkv_v0, 'rwkv_w_vres_a': rwkv_w_vres_a, 'rwkv_w_vres_b': rwkv_w_vres_b,
        'hgrn_lb_raw': hgrn_lb_raw, 'hgrn_norm_g': hgrn_norm_g, 'w_out': w_out,
    }
    B, T = x_prompt.shape[:2]
    pos_p = jnp.arange(T)
    shift0_p = jnp.zeros((DEPTH, B, RWKV_IN), x_prompt.dtype)
    rwkv0_p = jnp.zeros((DEPTH, B, RWKV_HEADS, RWKV_HEAD, RWKV_HEAD), jnp.float32)
    hgrn0_p = jnp.zeros((DEPTH, B, HGRN_HEADS, HGRN_HEAD, HGRN_HEAD), jnp.float32)
    y_prompt, p_ckv, p_kpe, p_rwkv, p_shift, p_hgrn = run_trunk(
        x_prompt, c_prompt, pos_p, shift0_p, rwkv0_p, hgrn0_p, None, P)
    Ts = x_sample.shape[1]
    pos_s = PAST_LEN + jnp.arange(Ts)
    y_sample, s_ckv, s_kpe, s_rwkv, s_shift, s_hgrn = run_trunk(
        x_sample, c_sample, pos_s, state_rwkv_shift, state_rwkv, state_hgrn,
        (cache_ckv, cache_kpe, page_table), P)
    return (y_prompt, y_sample, p_ckv, p_kpe, p_rwkv, p_shift, p_hgrn,
            s_ckv, s_kpe, s_rwkv, s_shift, s_hgrn)
```

```python
import functools
import math

import jax
import jax.numpy as jnp
from jax import lax
from jax.experimental import pallas as pl
from jax.experimental.pallas import tpu as pltpu

F32 = jnp.float32
BF16 = jnp.bfloat16

D_MODEL = 1024
DEPTH = 2
NORM_EPS = 1e-6
MASK_VALUE = -1e30
MIN_FORGET = 1e-20
PAGE_SIZE = 128
MLA_HEADS = 8
MLA_NOPE = 64
MLA_ROPE = 32
MLA_V = 64
MLA_Q_RANK = 384
MLA_KV_RANK = 256
MLA_WIDTH = MLA_HEADS * MLA_V
MLA_SCALE = (MLA_NOPE + MLA_ROPE) ** -0.5
ROPE_BASE = 10000.0
RWKV_WIDTH = 256
RWKV_HEAD = 64
RWKV_HEADS = 4
DECAY_RANK = 64
ICLR_RANK = 64
VRES_RANK = 32
RWKV_GN_EPS = 64e-5
RWKV_IN = 3 * RWKV_WIDTH + DECAY_RANK + ICLR_RANK
RWKV_CHUNK = 64
HGRN_WIDTH = 256
HGRN_HEADS = 4
HGRN_HEAD = 64
HGRN_IN = 3 * HGRN_WIDTH
HGRN_SUB = 16
LANE = 128
MLA_IN_PAD = MLA_Q_RANK + MLA_KV_RANK + LANE
QK_PAD = LANE
PE_LO = MLA_NOPE
IN_PAD = MLA_IN_PAD + RWKV_IN + HGRN_IN + D_MODEL
RWKV_FEAT = 7 * RWKV_WIDTH
VMEM_LIMIT = 56 * 1024 * 1024

NN = (((1,), (0,)), ((), ()))
NT = (((1,), (1,)), ((), ()))
TN = (((0,), (0,)), ((), ()))


def _bdot(a, b, dims=NN):
    return lax.dot_general(a.astype(BF16), b.astype(BF16), dims, preferred_element_type=F32)


def _split(a):
    hi = a.astype(BF16)
    lo = (a - hi.astype(F32)).astype(BF16)
    return hi, lo


def _dot3(a, b, dims=NN):
    ah, al = _split(a)
    bh, bl = _split(b)
    d = lambda x, y: lax.dot_general(x, y, dims, preferred_element_type=F32)
    return d(ah, bh) + (d(ah, bl) + d(al, bh))


def _dot_exact_rhs(a, b, dims=NN):
    a1 = a.astype(BF16)
    r1 = a - a1.astype(F32)
    a2 = r1.astype(BF16)
    a3 = (r1 - a2.astype(F32)).astype(BF16)
    bb = b.astype(BF16)
    d = lambda x: lax.dot_general(x, bb, dims, preferred_element_type=F32)
    return d(a1) + (d(a2) + d(a3))


def _dot_exact_lhs(m, x):
    x1 = x.astype(BF16)
    r1 = x - x1.astype(F32)
    x2 = r1.astype(BF16)
    x3 = (r1 - x2.astype(F32)).astype(BF16)
    d = lambda y: lax.dot_general(m, y, NN, preferred_element_type=F32)
    return d(x1) + (d(x2) + d(x3))


def _silu(x):
    return x * jax.nn.sigmoid(x)


def _rms(x, g):
    return x * lax.rsqrt(jnp.mean(x * x, axis=-1, keepdims=True) + NORM_EPS) * g


def _rows(m, ns, seg):
    w = m.shape[-1]
    return jnp.broadcast_to(m, (ns, seg, w)).reshape(ns * seg, w)


def _head_ones(width, head):
    r = lax.broadcasted_iota(jnp.int32, (width, width), 0) // head
    c = lax.broadcasted_iota(jnp.int32, (width, width), 1) // head
    return (r == c).astype(BF16)


def _cparams(sem):
    return pltpu.CompilerParams(dimension_semantics=sem, vmem_limit_bytes=VMEM_LIMIT)


def _seg_tiles(n_rows, seq_len, tile):
    tile = min(tile, n_rows)
    seg = min(seq_len, tile)
    ns = tile // seg
    assert ns * seg == tile and n_rows % tile == 0 and seq_len % seg == 0
    return tile, ns, seg


def _ada_kernel(c_ref, w_ref, b_ref, o_ref):
    o_ref[0] = _dot3(_silu(c_ref[...]), w_ref[0]) + b_ref[0]


def _ada(c_all, w_ada, b_ada):
    bt = c_all.shape[0]
    tn = 768
    return pl.pallas_call(
        _ada_kernel,
        out_shape=jax.ShapeDtypeStruct((DEPTH, bt, 3 * D_MODEL), F32),
        grid=(DEPTH, 3 * D_MODEL // tn),
        in_specs=[
            pl.BlockSpec((bt, D_MODEL), lambda l, j: (0, 0)),
            pl.BlockSpec((1, D_MODEL, tn), lambda l, j: (l, 0, j)),
            pl.BlockSpec((1, 1, tn), lambda l, j: (l, 0, j)),
        ],
        out_specs=pl.BlockSpec((1, bt, tn), lambda l, j: (l, 0, j)),
        compiler_params=_cparams(("arbitrary", "arbitrary")),
        name="ada",
    )(c_all, w_ada, b_ada.reshape(DEPTH, 1, 3 * D_MODEL))


def _in_proj_kernel(x_ref, mod_ref, g_ref, w_ref, o_mla, o_rwkv, o_hgrn, o_gate, *, ns, seg):
    mod = mod_ref[...]
    shift = _rows(mod[:, :, 0:D_MODEL], ns, seg)
    scale = _rows(mod[:, :, D_MODEL:2 * D_MODEL], ns, seg)
    h = (_rms(x_ref[...], g_ref[...]) * (1.0 + scale) + shift).astype(BF16)
    c0 = 0
    for o_ref in (o_mla, o_rwkv, o_hgrn, o_gate):
        c1 = c0 + o_ref.shape[1]
        o_ref[...] = lax.dot_general(h, w_ref[:, c0:c1], NN, preferred_element_type=F32)
        c0 = c1


def _in_proj(x, mod, g, w, seq_len, tile=512):
    n = x.shape[0]
    tm, ns, seg = _seg_tiles(n, seq_len, tile)
    per_seq = seq_len // seg
    widths = (MLA_IN_PAD, RWKV_IN, HGRN_IN, D_MODEL)
    return pl.pallas_call(
        functools.partial(_in_proj_kernel, ns=ns, seg=seg),
        out_shape=[jax.ShapeDtypeStruct((n, wd), F32) for wd in widths],
        grid=(n // tm,),
        in_specs=[
            pl.BlockSpec((tm, D_MODEL), lambda i: (i, 0)),
            pl.BlockSpec((ns, 1, 3 * D_MODEL), lambda i: (i // per_seq, 0, 0)),
            pl.BlockSpec((1, D_MODEL), lambda i: (0, 0)),
            pl.BlockSpec((D_MODEL, IN_PAD), lambda i: (0, 0)),
        ],
        out_specs=[pl.BlockSpec((tm, wd), lambda i: (i, 0)) for wd in widths],
        compiler_params=_cparams(("arbitrary",)),
        name="in_proj",
    )(x, mod, g, w)


def _out_proj_kernel(om_ref, or_ref, oh_ref, gate_ref, x_ref, mod_ref, g_ref, w_ref, o_ref, *, ns, seg):
    sg = _silu(gate_ref[...])
    a0, a1 = MLA_WIDTH, MLA_WIDTH + RWKV_WIDTH
    acc = _bdot(om_ref[...] * sg[:, 0:a0], w_ref[0:a0, :])
    acc += _bdot(or_ref[...] * sg[:, a0:a1], w_ref[a0:a1, :])
    acc += _bdot(oh_ref[...] * sg[:, a1:], w_ref[a1:, :])
    a_gate = _rows(mod_ref[...][:, :, 2 * D_MODEL:], ns, seg)
    o_ref[...] = x_ref[...] + a_gate * _rms(acc, g_ref[...])


def _out_proj(o_mla, o_rwkv, o_hgrn, gate, x, mod, g, w, seq_len, tile=512):
    n = x.shape[0]
    tm, ns, seg = _seg_tiles(n, seq_len, tile)
    per_seq = seq_len // seg
    row = lambda wd: pl.BlockSpec((tm, wd), lambda i: (i, 0))
    return pl.pallas_call(
        functools.partial(_out_proj_kernel, ns=ns, seg=seg),
        out_shape=jax.ShapeDtypeStruct((n, D_MODEL), F32),
        grid=(n // tm,),
        in_specs=[
            row(MLA_WIDTH), row(RWKV_WIDTH), row(HGRN_WIDTH), row(D_MODEL), row(D_MODEL),
            pl.BlockSpec((ns, 1, 3 * D_MODEL), lambda i: (i // per_seq, 0, 0)),
            pl.BlockSpec((1, D_MODEL), lambda i: (0, 0)),
            pl.BlockSpec((D_MODEL, D_MODEL), lambda i: (0, 0)),
        ],
        out_specs=row(D_MODEL),
        compiler_params=_cparams(("arbitrary",)),
        name="out_proj",
    )(o_mla, o_rwkv, o_hgrn, gate, x, mod, g, w)


def _rope_tile(x, cos, sin):
    lane = lax.broadcasted_iota(jnp.int32, x.shape, 1)
    half = MLA_ROPE // 2
    rot = jnp.where(lane < PE_LO + half, pltpu.roll(x, LANE - half, axis=1), pltpu.roll(x, half, axis=1))
    return x * cos + rot * sin


def _mla_prep_kernel(p_ref, cos_ref, sin_ref, qg_ref, kvg_ref, wq_ref, *rest, absorbed):
    p = p_ref[...]
    cos, sin = cos_ref[...], sin_ref[...]
    cq = _rms(p[:, 0:MLA_Q_RANK], qg_ref[...])
    ckv = _rms(p[:, MLA_Q_RANK:MLA_Q_RANK + MLA_KV_RANK], kvg_ref[...])
    kpe = _rope_tile(p[:, MLA_Q_RANK + MLA_KV_RANK:], cos, sin)
    q = _bdot(cq, wq_ref[...])
    if absorbed:
        wuk_ref, qx_out, ckv_out, kpe_out = rest
        ext = wuk_ref.shape[2]
        for h in range(MLA_HEADS):
            qh = _rope_tile(q[:, h * QK_PAD:(h + 1) * QK_PAD], cos, sin)
            qx_out[:, h * ext:(h + 1) * ext] = _bdot(qh, wuk_ref[h]).astype(BF16)
    else:
        wk_ref, wv_ref, q_out, k_out, v_out, ckv_out, kpe_out = rest
        kn = _bdot(ckv, wk_ref[...])
        for h in range(MLA_HEADS):
            sl = slice(h * QK_PAD, (h + 1) * QK_PAD)
            q_out[:, sl] = _rope_tile(q[:, sl], cos, sin).astype(BF16)
            k_out[:, sl] = (kn[:, sl] + kpe).astype(BF16)
        v_out[...] = _bdot(ckv, wv_ref[...]).astype(BF16)
    ckv_out[...] = ckv
    kpe_out[...] = kpe


def _mla_prep(p_mla, cos, sin, qg, kvg, wq, extra_w, seq_len, absorbed, tile=512):
    n = p_mla.shape[0]
    tm = min(tile, n)
    rep = max(seq_len // tm, 1)
    hw = MLA_HEADS * QK_PAD
    row = lambda wd: pl.BlockSpec((tm, wd), lambda i: (i, 0))
    full = lambda a: pl.BlockSpec(a.shape, lambda i: (0,) * a.ndim)
    tab = pl.BlockSpec((tm, LANE), lambda i: (i % rep, 0))
    if absorbed:
        ext = extra_w[0].shape[2]
        outs = [(MLA_HEADS * ext, BF16), (MLA_KV_RANK, F32), (LANE, F32)]
    else:
        outs = [(hw, BF16), (hw, BF16), (hw, BF16), (MLA_KV_RANK, F32), (LANE, F32)]
    return pl.pallas_call(
        functools.partial(_mla_prep_kernel, absorbed=absorbed),
        out_shape=[jax.ShapeDtypeStruct((n, wd), dt) for wd, dt in outs],
        grid=(n // tm,),
        in_specs=[row(MLA_IN_PAD), tab, tab, full(qg), full(kvg), full(wq)] + [full(a) for a in extra_w],
        out_specs=[row(wd) for wd, _ in outs],
        compiler_params=_cparams(("arbitrary",)),
        name="mla_prep_absorbed" if absorbed else "mla_prep",
    )(p_mla, cos, sin, qg, kvg, wq, *extra_w)


def _softmax_step(s, v, m, l, acc):
    c = MLA_SCALE * math.log2(math.e)
    m_new = jnp.maximum(m, jnp.max(s, axis=-1, keepdims=True))
    alpha = jnp.exp2((m - m_new) * c)
    p = jnp.exp2((s - m_new) * c)
    l = alpha * l + jnp.sum(p, axis=-1, keepdims=True)
    acc = alpha * acc + lax.dot_general(p.astype(BF16), v, NN, preferred_element_type=F32)
    return m_new, l, acc


def _attn_kernel(q_ref, k_ref, v_ref, o_ref, *, tq):
    i = pl.program_id(2)
    out = jnp.zeros((tq, LANE), F32)
    for hh in range(2):
        sl = slice(hh * QK_PAD, (hh + 1) * QK_PAD)
        q = q_ref[:, sl]

        def kv(j):
            rows = pl.ds(pl.multiple_of(j * tq, tq), tq)
            return k_ref[rows, sl], v_ref[rows, sl]

        def body(j, carry):
            k, v = kv(j)
            s = lax.dot_general(q, k, NT, preferred_element_type=F32)
            return _softmax_step(s, v, *carry)

        init = (jnp.full((tq, 1), MASK_VALUE, F32), jnp.zeros((tq, 1), F32), jnp.zeros((tq, LANE), F32))
        m, l, acc = lax.fori_loop(0, i, body, init)
        k, v = kv(i)
        s = lax.dot_general(q, k, NT, preferred_element_type=F32)
        r = lax.broadcasted_iota(jnp.int32, (tq, tq), 0)
        c = lax.broadcasted_iota(jnp.int32, (tq, tq), 1)
        m, l, acc = _softmax_step(jnp.where(c <= r, s, MASK_VALUE), v, m, l, acc)
        out = out + acc / l
    o_ref[...] = out


def _attention_prompt(q, k, v, batch, seq_len, tq=256):
    n = q.shape[0]
    tq = min(tq, seq_len)
    nq = seq_len // tq
    pair = 2 * QK_PAD
    return pl.pallas_call(
        functools.partial(_attn_kernel, tq=tq),
        out_shape=jax.ShapeDtypeStruct((n, MLA_WIDTH), F32),
        grid=(batch, MLA_HEADS // 2, nq),
        in_specs=[
            pl.BlockSpec((tq, pair), lambda b, h, i: (b * nq + i, h)),
            pl.BlockSpec((seq_len, pair), lambda b, h, i: (b, h)),
            pl.BlockSpec((seq_len, pair), lambda b, h, i: (b, h)),
        ],
        out_specs=pl.BlockSpec((tq, LANE), lambda b, h, i: (b * nq + i, h)),
        compiler_params=_cparams(("arbitrary", "arbitrary", "arbitrary")),
        name="attn_prompt",
    )(q, k, v)


def _attn_paged_kernel(pt_ref, q_ref, ckv_new_ref, kpe_new_ref, wuv_ref, *rest, pages, t_new):
    ckv_refs = rest[:pages]
    kpe_refs = rest[pages:2 * pages]
    o_ref, m_ref, l_ref, acc_ref = rest[2 * pages:]
    g = pl.program_id(1)
    rows = MLA_HEADS * t_new

    @pl.when(g == 0)
    def _():
        m_ref[...] = jnp.full(m_ref.shape, MASK_VALUE, F32)
        l_ref[...] = jnp.zeros(l_ref.shape, F32)
        acc_ref[...] = jnp.zeros(acc_ref.shape, F32)

    q = q_ref[0]
    q_lat = q[:, 0:MLA_KV_RANK]
    q_pe = q[:, MLA_KV_RANK:MLA_KV_RANK + MLA_ROPE]
    m, l, acc = m_ref[...], l_ref[...], acc_ref[...]
    for pg in range(pages):
        ckv = ckv_refs[pg][0, 0].astype(BF16)
        kpe = kpe_refs[pg][0, 0].astype(BF16)
        s = lax.dot_general(q_lat, ckv, NT, preferred_element_type=F32)
        s += lax.dot_general(q_pe, kpe, NT, preferred_element_type=F32)
        m, l, acc = _softmax_step(s, ckv, m, l, acc)
    m_ref[...], l_ref[...], acc_ref[...] = m, l, acc

    @pl.when(g == pl.num_programs(1) - 1)
    def _():
        ckv = ckv_new_ref[0].astype(BF16)
        kpe = kpe_new_ref[0].astype(BF16)
        s = lax.dot_general(q_lat, ckv, NT, preferred_element_type=F32)
        s += lax.dot_general(q_pe, kpe, NT, preferred_element_type=F32)
        t_q = lax.broadcasted_iota(jnp.int32, (rows, t_new), 0) % t_new
        t_k = lax.broadcasted_iota(jnp.int32, (rows, t_new), 1)
        m2, l2, acc2 = _softmax_step(jnp.where(t_k <= t_q, s, MASK_VALUE), ckv, m, l, acc)
        lat = acc2 / l2
        for h in range(MLA_HEADS):
            o_ref[0, h * t_new:(h + 1) * t_new, :] = _bdot(lat[h * t_new:(h + 1) * t_new, :], wuv_ref[h])


def _attention_paged(q_ext, ckv_new, kpe_new, w_uv, cache_ckv, cache_kpe, page_table, layer, pages=8):
    batch, rows, ext = q_ext.shape
    t_new = rows // MLA_HEADS
    n_pages = page_table.shape[1]
    pages = min(pages, n_pages)
    assert n_pages % pages == 0
    page = lambda w, pg: pl.BlockSpec(
        (1, 1, PAGE_SIZE, w), lambda b, g, pt: (layer, pt[b * n_pages + g * pages + pg], 0, 0))
    grid_spec = pltpu.PrefetchScalarGridSpec(
        num_scalar_prefetch=1,
        grid=(batch, n_pages // pages),
        in_specs=[
            pl.BlockSpec((1, rows, ext), lambda b, g, pt: (b, 0, 0)),
            pl.BlockSpec((1, t_new, MLA_KV_RANK), lambda b, g, pt: (b, 0, 0)),
            pl.BlockSpec((1, t_new, MLA_ROPE), lambda b, g, pt: (b, 0, 0)),
            pl.BlockSpec(w_uv.shape, lambda b, g, pt: (0, 0, 0)),
        ] + [page(MLA_KV_RANK, pg) for pg in range(pages)] + [page(MLA_ROPE, pg) for pg in range(pages)],
        out_specs=pl.BlockSpec((1, rows, MLA_V), lambda b, g, pt: (b, 0, 0)),
        scratch_shapes=[pltpu.VMEM((rows, 1), F32), pltpu.VMEM((rows, 1), F32), pltpu.VMEM((rows, MLA_KV_RANK), F32)],
    )
    return pl.pallas_call(
        functools.partial(_attn_paged_kernel, pages=pages, t_new=t_new),
        out_shape=jax.ShapeDtypeStruct((batch, rows, MLA_V), F32),
        grid_spec=grid_spec,
        compiler_params=_cparams(("arbitrary", "arbitrary")),
        name="attn_paged",
    )(page_table.reshape(-1), q_ext, ckv_new, kpe_new, w_uv, *([cache_ckv] * pages), *([cache_kpe] * pages))


def _rwkv_prep_kernel(p_ref, halo_ref, mu_ref, w2_ref, wa0_ref, kk_ref, ka_ref, rk_ref, *rest, ns, seg, first):
    if first:
        feat_ref, vfirst_out = rest
    else:
        vfirst_ref, v0_ref, wva_ref, wvb_ref, feat_ref = rest
    w = RWKV_WIDTH
    p = p_ref[...]
    rows = p.shape[0]
    row = lax.broadcasted_iota(jnp.int32, p.shape, 0)
    prev = jnp.where(row % seg == 0, _rows(halo_ref[0], ns, seg), pltpu.roll(p, 1, axis=0))
    pm = p + mu_ref[...] * (prev - p)
    r, k, v, xwa = pm[:, 0:w], pm[:, w:2 * w], pm[:, 2 * w:3 * w], pm[:, 3 * w:]
    lane = lax.broadcasted_iota(jnp.int32, xwa.shape, 1)
    da = _dot3(jnp.where(lane < DECAY_RANK, jnp.tanh(xwa), xwa), w2_ref[...]) + wa0_ref[...]
    z = -da[:, 0:w]
    softplus = jnp.maximum(z, 0.0) + jnp.log1p(jnp.exp(-jnp.abs(z)))
    log_decay = -jnp.exp(-softplus - 0.5)
    a = jax.nn.sigmoid(da[:, w:])
    if first:
        vfirst_out[...] = v
    else:
        mix = jax.nn.sigmoid(v0_ref[...] + _dot3(_dot3(v, wva_ref[...]), wvb_ref[...]))
        v = v + (vfirst_ref[...] - v) * mix
    kk = k * kk_ref[...]
    k = k * (1.0 + (a - 1.0) * ka_ref[...])
    ones = _head_ones(w, RWKV_HEAD)
    kk = kk / jnp.maximum(jnp.sqrt(_dot_exact_rhs(kk * kk, ones)), 1e-12)
    bonus = _dot_exact_rhs(r * k * rk_ref[...], ones) * v
    for j, t in enumerate((r, log_decay, k, v, -kk, kk * a, bonus)):
        feat_ref[:, j * w:(j + 1) * w] = t


def _rwkv_prep(p_rwkv, halo, consts, v_first, vres, seq_len, tile=256):
    n = p_rwkv.shape[0]
    tm, ns, seg = _seg_tiles(n, seq_len, tile)
    first = v_first is None
    row = lambda wd: pl.BlockSpec((tm, wd), lambda i: (i, 0))
    full = lambda a: pl.BlockSpec(a.shape, lambda i: (0,) * a.ndim)
    ins = [p_rwkv, halo] + list(consts)
    specs = [row(RWKV_IN), pl.BlockSpec((1, ns, 1, RWKV_IN), lambda i: (i, 0, 0, 0))] + [full(a) for a in consts]
    feat = jax.ShapeDtypeStruct((n, RWKV_FEAT), F32)
    if first:
        out_shape = [feat, jax.ShapeDtypeStruct((n, RWKV_WIDTH), F32)]
        out_specs = [row(RWKV_FEAT), row(RWKV_WIDTH)]
    else:
        ins += [v_first] + list(vres)
        specs += [row(RWKV_WIDTH)] + [full(a) for a in vres]
        out_shape, out_specs = feat, row(RWKV_FEAT)
    return pl.pallas_call(
        functools.partial(_rwkv_prep_kernel, ns=ns, seg=seg, first=first),
        out_shape=out_shape,
        grid=(n // tm,),
        in_specs=specs,
        out_specs=out_specs,
        compiler_params=_cparams(("arbitrary",)),
        name="rwkv_prep",
    )(*ins)


def _rwkv_chunk_head(r, lw_cum, lw, k, v, a, b, s0):
    c = r.shape[0]
    dec_in = jnp.exp(lw_cum)
    inv = jnp.exp(-lw_cum)
    total = lw_cum[c - 1:c, :]
    to_end = jnp.exp(total - lw_cum)
    ar = jnp.concatenate([a * jnp.exp(lw_cum - lw), r * dec_in], axis=0)
    bk = jnp.concatenate([b * inv, k * inv], axis=0)
    g = _dot3(ar, bk, NT)
    gr = lax.broadcasted_iota(jnp.int32, g.shape, 0)
    gc = lax.broadcasted_iota(jnp.int32, g.shape, 1) % c
    g = jnp.where(gc < gr - jnp.where(gr < c, 0, c - 1), g, 0.0)
    g_top, g_bot = g[0:c, :], g[c:, :]
    a_ab = g_top[:, 0:c]
    eye = (lax.broadcasted_iota(jnp.int32, (c, c), 0) == lax.broadcasted_iota(jnp.int32, (c, c), 1)).astype(F32)
    t_inv = eye + a_ab
    x = _dot3(a_ab, a_ab)
    for _ in range(int(math.log2(c)) - 2):
        tx = _dot3(jnp.concatenate([t_inv, x], axis=0), x)
        t_inv = t_inv + tx[0:c, :]
        x = tx[c:, :]
    t_inv = t_inv + _dot3(t_inv, x)
    ar_s0 = _dot3(ar, s0, NT)
    zero_v = jnp.concatenate([jnp.zeros_like(v), v], axis=0)
    u = _dot3(t_inv, ar_s0[0:c, :] + _dot3(g_top, zero_v))
    uv = jnp.concatenate([u, v], axis=0)
    o = ar_s0[c:, :] + _dot3(g_bot, uv)
    bk_end = jnp.concatenate([b * to_end, k * to_end], axis=0)
    s1 = s0 * jnp.exp(total) + _dot3(uv, bk_end, TN)
    return o, s1


def _rwkv_scan_kernel(feat_ref, s0_ref, gn_g_ref, gn_b_ref, o_ref, s_out_ref, s_ref, *, ns, chunk):
    j = pl.program_id(1)
    w = RWKV_WIDTH

    @pl.when(j == 0)
    def _():
        s_ref[...] = s0_ref[...]

    tri = (lax.broadcasted_iota(jnp.int32, (chunk, chunk), 1)
           <= lax.broadcasted_iota(jnp.int32, (chunk, chunk), 0)).astype(BF16)

    def seq_body(q, carry):
        rows = pl.ds(pl.multiple_of(q * chunk, chunk), chunk)
        f = feat_ref[rows, :]
        r, lw, k, v, a, b, bonus = (f[:, i * w:(i + 1) * w] for i in range(7))
        lw_cum = _dot_exact_lhs(tri, lw)
        outs = []
        for h in range(RWKV_HEADS):
            sl = slice(h * RWKV_HEAD, (h + 1) * RWKV_HEAD)
            o_h, s1 = _rwkv_chunk_head(r[:, sl], lw_cum[:, sl], lw[:, sl], k[:, sl], v[:, sl], a[:, sl], b[:, sl],
                                       s_ref[q, h])
            s_ref[q, h] = s1
            mean = jnp.mean(o_h, axis=-1, keepdims=True)
            var = jnp.mean(jnp.square(o_h - mean), axis=-1, keepdims=True)
            outs.append((o_h - mean) * lax.rsqrt(var + RWKV_GN_EPS))
        o_ref[rows, :] = jnp.concatenate(outs, axis=1) * gn_g_ref[...] + gn_b_ref[...] + bonus
        return carry

    lax.fori_loop(0, ns, seq_body, 0)

    @pl.when(j == pl.num_programs(1) - 1)
    def _():
        s_out_ref[...] = s_ref[...]


def _rwkv_scan(feat, s0, gn_g, gn_b, seq_len, ns_max=8):
    n = feat.shape[0]
    batch = n // seq_len
    chunk = min(RWKV_CHUNK, seq_len)
    nc = seq_len // chunk
    ns = min(ns_max, batch) if nc == 1 else 1
    assert batch % ns == 0 and seq_len % chunk == 0
    st = (ns, RWKV_HEADS, RWKV_HEAD, RWKV_HEAD)
    return pl.pallas_call(
        functools.partial(_rwkv_scan_kernel, ns=ns, chunk=chunk),
        out_shape=[jax.ShapeDtypeStruct((n, RWKV_WIDTH), F32), jax.ShapeDtypeStruct(s0.shape, F32)],
        grid=(batch // ns, nc),
        in_specs=[
            pl.BlockSpec((ns * chunk, RWKV_FEAT), lambda b, j: (b * nc + j, 0)),
            pl.BlockSpec(st, lambda b, j: (b, 0, 0, 0)),
            pl.BlockSpec((1, RWKV_WIDTH), lambda b, j: (0, 0)),
            pl.BlockSpec((1, RWKV_WIDTH), lambda b, j: (0, 0)),
        ],
        out_specs=[
            pl.BlockSpec((ns * chunk, RWKV_WIDTH), lambda b, j: (b * nc + j, 0)),
            pl.BlockSpec(st, lambda b, j: (b, 0, 0, 0)),
        ],
        scratch_shapes=[pltpu.VMEM(st, F32)],
        compiler_params=_cparams(("arbitrary", "arbitrary")),
        name="rwkv_scan",
    )(feat, s0, gn_g, gn_b)


def _hgrn_scan_kernel(p_ref, lb_ref, ng_ref, s0_ref, o_ref, s_out_ref, s_ref, q_s, k_s, v_s, b_s, *, ns, nsub, sub):
    j = pl.program_id(1)
    w = HGRN_WIDTH
    tile = ns * nsub * sub

    @pl.when(j == 0)
    def _():
        s_ref[...] = s0_ref[...]

    p = p_ref[...]
    lb = lb_ref[...]
    sig = jax.nn.sigmoid(p[:, w:2 * w])
    q_s[...] = _silu(p[:, 0:w]) * HGRN_HEAD ** -0.5
    k_s[...] = (1.0 - lb) * (1.0 - sig)
    v_s[...] = p[:, 2 * w:]
    log_f = jnp.log(jnp.maximum(lb + (1.0 - lb) * sig, MIN_FORGET))
    tr = lax.broadcasted_iota(jnp.int32, (tile, tile), 0)
    tc = lax.broadcasted_iota(jnp.int32, (tile, tile), 1)
    tri = ((tr // sub == tc // sub) & (tc <= tr)).astype(BF16)
    b_s[...] = _dot_exact_lhs(tri, log_f)
    ones = _head_ones(w, HGRN_HEAD)
    sub_row = lax.broadcasted_iota(jnp.int32, (sub, w), 0)

    def sub_body(idx, carry):
        seq = idx // nsub
        base = pl.multiple_of(idx * sub, sub)
        rows = pl.ds(base, sub)
        q, k, v, b = q_s[rows, :], k_s[rows, :], v_s[rows, :], b_s[rows, :]
        pair = [q * jnp.exp(jnp.minimum(b - b_s[pl.ds(base + s, 1), :], 0.0)) * k_s[pl.ds(base + s, 1), :]
                for s in range(sub)]
        att = _dot_exact_rhs(jnp.concatenate(pair, axis=0), ones)
        o = jnp.zeros((sub, w), F32)
        for s in range(sub):
            o = o + jnp.where(sub_row >= s, att[s * sub:(s + 1) * sub, :], 0.0) * v_s[pl.ds(base + s, 1), :]
        last = b_s[pl.ds(base + sub - 1, 1), :]
        k_end = k * jnp.exp(last - b)
        gamma = jnp.exp(last)
        q_in = q * jnp.exp(b)
        outs = []
        for h in range(HGRN_HEADS):
            sl = slice(h * HGRN_HEAD, (h + 1) * HGRN_HEAD)
            st = s_ref[seq, h]
            outs.append(_dot3(q_in[:, sl], st, NT))
            s_ref[seq, h] = st * gamma[:, sl] + _dot3(v[:, sl], k_end[:, sl], TN)
        o = o + jnp.concatenate(outs, axis=1)
        ms = _dot_exact_rhs(o * o, ones) * (1.0 / HGRN_HEAD)
        o_ref[rows, :] = o * lax.rsqrt(ms + NORM_EPS) * ng_ref[...]
        return carry

    lax.fori_loop(0, ns * nsub, sub_body, 0)

    @pl.when(j == pl.num_programs(1) - 1)
    def _():
        s_out_ref[...] = s_ref[...]


def _hgrn_scan(p_hgrn, lb, norm_g, s0, seq_len, tile=256, ns_max=16):
    n = p_hgrn.shape[0]
    batch = n // seq_len
    sub = min(HGRN_SUB, seq_len)
    if seq_len >= tile:
        ns, nsub = 1, tile // sub
    else:
        ns, nsub = min(ns_max, batch), seq_len // sub
    rows = ns * nsub * sub
    nt = seq_len // (nsub * sub)
    assert batch % ns == 0 and seq_len % (nsub * sub) == 0
    st = (ns, HGRN_HEADS, HGRN_HEAD, HGRN_HEAD)
    return pl.pallas_call(
        functools.partial(_hgrn_scan_kernel, ns=ns, nsub=nsub, sub=sub),
        out_shape=[jax.ShapeDtypeStruct((n, HGRN_WIDTH), F32), jax.ShapeDtypeStruct(s0.shape, F32)],
        grid=(batch // ns, nt),
        in_specs=[
            pl.BlockSpec((rows, HGRN_IN), lambda b, j: (b * nt + j, 0)),
            pl.BlockSpec((1, HGRN_WIDTH), lambda b, j: (0, 0)),
            pl.BlockSpec((1, HGRN_WIDTH), lambda b, j: (0, 0)),
            pl.BlockSpec(st, lambda b, j: (b, 0, 0, 0)),
        ],
        out_specs=[
            pl.BlockSpec((rows, HGRN_WIDTH), lambda b, j: (b * nt + j, 0)),
            pl.BlockSpec(st, lambda b, j: (b, 0, 0, 0)),
        ],
        scratch_shapes=[pltpu.VMEM(st, F32)] + [pltpu.VMEM((rows, HGRN_WIDTH), F32)] * 4,
        compiler_params=_cparams(("arbitrary", "arbitrary")),
        name="hgrn_scan",
    )(p_hgrn, lb, norm_g, s0)


def _layer_weights(P, l):
    w_in = P['w_in'][l]
    c_q, c_kv = MLA_Q_RANK, MLA_Q_RANK + MLA_KV_RANK
    c_pe = c_kv + MLA_ROPE
    kpe_tile = jnp.zeros((D_MODEL, LANE), F32).at[:, PE_LO:PE_LO + MLA_ROPE].set(w_in[:, c_kv:c_pe])
    w_in_pad = jnp.concatenate([w_in[:, :c_kv], kpe_tile, w_in[:, c_pe:]], axis=1).astype(BF16)

    hd = MLA_NOPE + MLA_ROPE
    wq = P['mla_w_q_b'][l].reshape(MLA_Q_RANK, MLA_HEADS, hd)
    wq = jnp.pad(wq, ((0, 0), (0, 0), (0, QK_PAD - hd))).reshape(MLA_Q_RANK, MLA_HEADS * QK_PAD).astype(BF16)
    wkv = P['mla_w_kv_b'][l].reshape(MLA_KV_RANK, MLA_HEADS, MLA_NOPE + MLA_V)
    w_uk, w_uv = wkv[..., :MLA_NOPE], wkv[..., MLA_NOPE:]
    wk = jnp.pad(w_uk, ((0, 0), (0, 0), (0, QK_PAD - MLA_NOPE))).reshape(MLA_KV_RANK, MLA_HEADS * QK_PAD).astype(BF16)
    wv = jnp.zeros((MLA_KV_RANK, MLA_HEADS, QK_PAD), F32)
    for h in range(MLA_HEADS):
        off = (h % 2) * MLA_V
        wv = wv.at[:, h, off:off + MLA_V].set(w_uv[:, h, :])
    wv = wv.reshape(MLA_KV_RANK, MLA_HEADS * QK_PAD).astype(BF16)
    ext = MLA_KV_RANK + LANE
    wuk_ext = jnp.zeros((MLA_HEADS, QK_PAD, ext), F32)
    wuk_ext = wuk_ext.at[:, :MLA_NOPE, :MLA_KV_RANK].set(jnp.transpose(w_uk, (1, 2, 0)))
    wuk_ext = wuk_ext.at[:, PE_LO:PE_LO + MLA_ROPE, MLA_KV_RANK:MLA_KV_RANK + MLA_ROPE].set(
        jnp.broadcast_to(jnp.eye(MLA_ROPE, dtype=F32), (MLA_HEADS, MLA_ROPE, MLA_ROPE)))
    wuk_ext = wuk_ext.astype(BF16)
    w_uv_h = jnp.transpose(w_uv, (1, 0, 2)).astype(BF16)

    w = RWKV_WIDTH
    w2 = jnp.zeros((DECAY_RANK + ICLR_RANK, 2 * w), F32)
    w2 = w2.at[:DECAY_RANK, :w].set(P['rwkv_w_decay_b'][l]).at[DECAY_RANK:, w:].set(P['rwkv_w_iclr_b'][l])
    wa0 = jnp.concatenate([P['rwkv_w0'][l], P['rwkv_a0'][l]])[None, :]
    row = lambda a: a.reshape(1, -1)
    rwkv_consts = (row(P['rwkv_mu'][l]), w2, wa0, row(P['rwkv_k_k'][l]), row(P['rwkv_k_a'][l]), row(P['rwkv_r_k'][l]))
    vres = None
    if l > 0:
        wva = jnp.pad(P['rwkv_w_vres_a'][l - 1], ((0, 0), (0, LANE - VRES_RANK)))
        wvb = jnp.pad(P['rwkv_w_vres_b'][l - 1], ((0, LANE - VRES_RANK), (0, 0)))
        vres = (row(P['rwkv_v0'][l - 1]), wva, wvb)
    return dict(
        w_in=w_in_pad, pre_g=row(P['pre_norm_g'][l]), post_g=row(P['post_norm_g'][l]),
        w_out=P['w_out'][l].astype(BF16),
        q_g=row(P['mla_q_norm_g'][l]), kv_g=row(P['mla_kv_norm_g'][l]),
        wq=wq, wk=wk, wv=wv, wuk_ext=wuk_ext, w_uv=w_uv_h,
        rwkv_consts=rwkv_consts, vres=vres,
        gn_g=row(P['rwkv_gn_g'][l]), gn_b=row(P['rwkv_gn_b'][l]),
        hgrn_g=jnp.tile(P['hgrn_norm_g'][l], HGRN_HEADS)[None, :],
    )


def _rope_tables(pos, rows):
    half = MLA_ROPE // 2
    inv_freq = ROPE_BASE ** (-jnp.arange(half, dtype=F32) / half)
    ang = pos.astype(F32)[:, None] * inv_freq[None, :]
    cos, sin = jnp.cos(ang), jnp.sin(ang)
    t = pos.shape[0]
    ones, zeros = jnp.ones((t, PE_LO), F32), jnp.zeros((t, PE_LO), F32)
    pad = jnp.zeros((t, LANE - PE_LO - MLA_ROPE), F32)
    cos_t = jnp.concatenate([ones, cos, cos, pad], axis=1)
    sin_t = jnp.concatenate([zeros, -sin, sin, pad], axis=1)
    rep = rows // t
    return jnp.tile(cos_t, (rep, 1)), jnp.tile(sin_t, (rep, 1))


def _trunk(x, mods, pos, shift0, rwkv0, hgrn0, cache, LW, lb_all):
    batch, seq_len, _ = x.shape
    n = batch * seq_len
    x = x.reshape(n, D_MODEL)
    prep_tile = min(512, n)
    cos, sin = _rope_tables(pos, max(seq_len, prep_tile))
    rw_tile, rw_ns, rw_seg = _seg_tiles(n, seq_len, 256)
    v_first = None
    ckvs, kpes, rs, shs, hs = [], [], [], [], []
    for l in range(DEPTH):
        W = LW[l]
        mod = mods[l][:, None, :]
        p_mla, p_rwkv, p_hgrn, gate = _in_proj(x, mod, W['pre_g'], W['w_in'], seq_len)
        if cache is None:
            q, k, v, ckv, kpe = _mla_prep(p_mla, cos, sin, W['q_g'], W['kv_g'], W['wq'], (W['wk'], W['wv']),
                                          seq_len, absorbed=False)
            o_mla = _attention_prompt(q, k, v, batch, seq_len)
            kpe = kpe[:, PE_LO:PE_LO + MLA_ROPE]
        else:
            cache_ckv, cache_kpe, page_table = cache
            qx, ckv, kpe = _mla_prep(p_mla, cos, sin, W['q_g'], W['kv_g'], W['wq'], (W['wuk_ext'],),
                                     seq_len, absorbed=True)
            kpe = kpe[:, PE_LO:PE_LO + MLA_ROPE]
            ext = qx.shape[1] // MLA_HEADS
            qx = qx.reshape(batch, seq_len, MLA_HEADS, ext).transpose(0, 2, 1, 3).reshape(batch, MLA_HEADS * seq_len, ext)
            o = _attention_paged(qx, ckv.reshape(batch, seq_len, MLA_KV_RANK), kpe.reshape(batch, seq_len, MLA_ROPE),
                                 W['w_uv'], cache_ckv, cache_kpe, page_table, l)
            o_mla = o.reshape(batch, MLA_HEADS, seq_len, MLA_V).transpose(0, 2, 1, 3).reshape(n, MLA_WIDTH)
        p3 = p_rwkv.reshape(batch, seq_len, RWKV_IN)
        tails = p3[:, rw_seg - 1::rw_seg, :]
        halo = jnp.concatenate([shift0[l][:, None, :], tails[:, :-1, :]], axis=1).reshape(n // rw_tile, rw_ns, 1, RWKV_IN)
        if l == 0:
            feat, v_first = _rwkv_prep(p_rwkv, halo, W['rwkv_consts'], None, None, seq_len)
        else:
            feat = _rwkv_prep(p_rwkv, halo, W['rwkv_consts'], v_first, W['vres'], seq_len)
        o_rwkv, s_r = _rwkv_scan(feat, rwkv0[l], W['gn_g'], W['gn_b'], seq_len)
        o_hgrn, s_h = _hgrn_scan(p_hgrn, lb_all[l][None, :], W['hgrn_g'], jnp.swapaxes(hgrn0[l], -1, -2), seq_len)
        s_h = jnp.swapaxes(s_h, -1, -2)
        x = _out_proj(o_mla, o_rwkv, o_hgrn, gate, x, mod, W['post_g'], W['w_out'], seq_len)
        ckvs.append(ckv.reshape(batch, seq_len, MLA_KV_RANK))
        kpes.append(kpe.reshape(batch, seq_len, MLA_ROPE))
        rs.append(s_r)
        shs.append(p3[:, -1, :])
        hs.append(s_h)
    return (x.reshape(batch, seq_len, D_MODEL), jnp.stack(ckvs), jnp.stack(kpes), jnp.stack(rs), jnp.stack(shs),
            jnp.stack(hs))


def kernel(x_prompt, x_sample, c_prompt, c_sample, cache_ckv, cache_kpe, page_table, state_rwkv, state_rwkv_shift, state_hgrn, w_ada, b_ada, pre_norm_g, post_norm_g, w_in, mla_q_norm_g, mla_w_q_b, mla_kv_norm_g, mla_w_kv_b, rwkv_mu, rwkv_w0, rwkv_w_decay_b, rwkv_a0, rwkv_w_iclr_b, rwkv_k_k, rwkv_k_a, rwkv_r_k, rwkv_gn_g, rwkv_gn_b, rwkv_v0, rwkv_w_vres_a, rwkv_w_vres_b, hgrn_lb_raw, hgrn_norm_g, w_out):
    P = {
        'w_in': w_in, 'pre_norm_g': pre_norm_g, 'post_norm_g': post_norm_g,
        'mla_q_norm_g': mla_q_norm_g, 'mla_w_q_b': mla_w_q_b, 'mla_kv_norm_g': mla_kv_norm_g, 'mla_w_kv_b': mla_w_kv_b,
        'rwkv_mu': rwkv_mu, 'rwkv_w0': rwkv_w0, 'rwkv_w_decay_b': rwkv_w_decay_b, 'rwkv_a0': rwkv_a0,
        'rwkv_w_iclr_b': rwkv_w_iclr_b, 'rwkv_k_k': rwkv_k_k, 'rwkv_k_a': rwkv_k_a, 'rwkv_r_k': rwkv_r_k,
        'rwkv_gn_g': rwkv_gn_g, 'rwkv_gn_b': rwkv_gn_b, 'rwkv_v0': rwkv_v0, 'rwkv_w_vres_a': rwkv_w_vres_a,
        'rwkv_w_vres_b': rwkv_w_vres_b, 'hgrn_norm_g': hgrn_norm_g, 'w_out': w_out,
    }
    LW = [_layer_weights(P, l) for l in range(DEPTH)]
    lb_p = jax.nn.softmax(hgrn_lb_raw.astype(F32), axis=0)
    lb_all = jnp.cumsum(lb_p, axis=0) - lb_p[0]
    bp, tp = x_prompt.shape[:2]
    bs, ts = x_sample.shape[:2]
    mods = _ada(jnp.concatenate([c_prompt, c_sample], axis=0), w_ada, b_ada)
    past_len = page_table.shape[1] * PAGE_SIZE
    zeros = lambda *s: jnp.zeros(s, F32)
    out_p = _trunk(x_prompt, mods[:, :bp], jnp.arange(tp), zeros(DEPTH, bp, RWKV_IN),
                   zeros(DEPTH, bp, RWKV_HEADS, RWKV_HEAD, RWKV_HEAD), zeros(DEPTH, bp, HGRN_HEADS, HGRN_HEAD, HGRN_HEAD),
                   None, LW, lb_all)
    out_s = _trunk(x_sample, mods[:, bp:], past_len + jnp.arange(ts), state_rwkv_shift, state_rwkv, state_hgrn,
                   (cache_ckv, cache_kpe, page_table), LW, lb_all)
    return (out_p[0], out_s[0]) + out_p[1:] + out_s[1:]
```

```python
import functools
import math

import jax
import jax.numpy as jnp
from jax import lax
from jax.experimental import pallas as pl
from jax.experimental.pallas import tpu as pltpu

F32 = jnp.float32
BF16 = jnp.bfloat16

D_MODEL = 1024
DEPTH = 2
NORM_EPS = 1e-6
MASK_VALUE = -1e30
MIN_FORGET = 1e-20
PAGE_SIZE = 128
MLA_HEADS = 8
MLA_NOPE = 64
MLA_ROPE = 32
MLA_V = 64
MLA_Q_RANK = 384
MLA_KV_RANK = 256
MLA_WIDTH = MLA_HEADS * MLA_V
MLA_SCALE = (MLA_NOPE + MLA_ROPE) ** -0.5
ROPE_BASE = 10000.0
RWKV_WIDTH = 256
RWKV_HEAD = 64
RWKV_HEADS = 4
DECAY_RANK = 64
ICLR_RANK = 64
VRES_RANK = 32
RWKV_GN_EPS = 64e-5
RWKV_IN = 3 * RWKV_WIDTH + DECAY_RANK + ICLR_RANK
RWKV_CHUNK = 64
HGRN_WIDTH = 256
HGRN_HEADS = 4
HGRN_HEAD = 64
HGRN_IN = 3 * HGRN_WIDTH
HGRN_SUB = 16
LANE = 128
MLA_IN_PAD = MLA_Q_RANK + MLA_KV_RANK + LANE
QK_PAD = LANE
PE_LO = MLA_NOPE
IN_PAD = MLA_IN_PAD + RWKV_IN + HGRN_IN + D_MODEL
RWKV_FEAT = 7 * RWKV_WIDTH
VMEM_LIMIT = 56 * 1024 * 1024

NN = (((1,), (0,)), ((), ()))
NT = (((1,), (1,)), ((), ()))
TN = (((0,), (0,)), ((), ()))


def _bdot(a, b, dims=NN):
    return lax.dot_general(a.astype(BF16), b.astype(BF16), dims, preferred_element_type=F32)


def _split(a):
    hi = a.astype(BF16)
    lo = (a - hi.astype(F32)).astype(BF16)
    return hi, lo


def _dot3(a, b, dims=NN):
    ah, al = _split(a)
    bh, bl = _split(b)
    d = lambda x, y: lax.dot_general(x, y, dims, preferred_element_type=F32)
    return d(ah, bh) + (d(ah, bl) + d(al, bh))


def _dot_exact_rhs(a, b, dims=NN):
    a1 = a.astype(BF16)
    r1 = a - a1.astype(F32)
    a2 = r1.astype(BF16)
    a3 = (r1 - a2.astype(F32)).astype(BF16)
    bb = b.astype(BF16)
    d = lambda x: lax.dot_general(x, bb, dims, preferred_element_type=F32)
    return d(a1) + (d(a2) + d(a3))


def _dot_exact_lhs(m, x):
    x1 = x.astype(BF16)
    r1 = x - x1.astype(F32)
    x2 = r1.astype(BF16)
    x3 = (r1 - x2.astype(F32)).astype(BF16)
    d = lambda y: lax.dot_general(m, y, NN, preferred_element_type=F32)
    return d(x1) + (d(x2) + d(x3))


def _silu(x):
    return x * jax.nn.sigmoid(x)


def _rms(x, g):
    return x * lax.rsqrt(jnp.mean(x * x, axis=-1, keepdims=True) + NORM_EPS) * g


def _rows(m, ns, seg):
    w = m.shape[-1]
    return jnp.broadcast_to(m, (ns, seg, w)).reshape(ns * seg, w)


def _head_ones(width, head):
    r = lax.broadcasted_iota(jnp.int32, (width, width), 0) // head
    c = lax.broadcasted_iota(jnp.int32, (width, width), 1) // head
    return (r == c).astype(BF16)


def _cparams(sem):
    return pltpu.CompilerParams(dimension_semantics=sem, vmem_limit_bytes=VMEM_LIMIT)


def _seg_tiles(n_rows, seq_len, tile):
    tile = min(tile, n_rows)
    seg = min(seq_len, tile)
    ns = tile // seg
    assert ns * seg == tile and n_rows % tile == 0 and seq_len % seg == 0
    return tile, ns, seg


def _ada_kernel(c_ref, w_ref, b_ref, o_ref):
    o_ref[0] = _dot3(_silu(c_ref[...]), w_ref[0]) + b_ref[0]


def _ada(c_all, w_ada, b_ada):
    bt = c_all.shape[0]
    tn = 768
    return pl.pallas_call(
        _ada_kernel,
        out_shape=jax.ShapeDtypeStruct((DEPTH, bt, 3 * D_MODEL), F32),
        grid=(DEPTH, 3 * D_MODEL // tn),
        in_specs=[
            pl.BlockSpec((bt, D_MODEL), lambda l, j: (0, 0)),
            pl.BlockSpec((1, D_MODEL, tn), lambda l, j: (l, 0, j)),
            pl.BlockSpec((1, 1, tn), lambda l, j: (l, 0, j)),
        ],
        out_specs=pl.BlockSpec((1, bt, tn), lambda l, j: (l, 0, j)),
        compiler_params=_cparams(("arbitrary", "arbitrary")),
        name="ada",
    )(c_all, w_ada, b_ada.reshape(DEPTH, 1, 3 * D_MODEL))


def _in_proj_kernel(x_ref, mod_ref, g_ref, w_ref, o_mla, o_rwkv, o_hgrn, o_gate, *, ns, seg):
    mod = mod_ref[...]
    shift = _rows(mod[:, :, 0:D_MODEL], ns, seg)
    scale = _rows(mod[:, :, D_MODEL:2 * D_MODEL], ns, seg)
    h = (_rms(x_ref[...], g_ref[...]) * (1.0 + scale) + shift).astype(BF16)
    c0 = 0
    for o_ref in (o_mla, o_rwkv, o_hgrn, o_gate):
        c1 = c0 + o_ref.shape[1]
        o_ref[...] = lax.dot_general(h, w_ref[:, c0:c1], NN, preferred_element_type=F32)
        c0 = c1


def _in_proj(x, mod, g, w, seq_len, tile=512):
    n = x.shape[0]
    tm, ns, seg = _seg_tiles(n, seq_len, tile)
    per_seq = seq_len // seg
    widths = (MLA_IN_PAD, RWKV_IN, HGRN_IN, D_MODEL)
    return pl.pallas_call(
        functools.partial(_in_proj_kernel, ns=ns, seg=seg),
        out_shape=[jax.ShapeDtypeStruct((n, wd), F32) for wd in widths],
        grid=(n // tm,),
        in_specs=[
            pl.BlockSpec((tm, D_MODEL), lambda i: (i, 0)),
            pl.BlockSpec((ns, 1, 3 * D_MODEL), lambda i: (i // per_seq, 0, 0)),
            pl.BlockSpec((1, D_MODEL), lambda i: (0, 0)),
            pl.BlockSpec((D_MODEL, IN_PAD), lambda i: (0, 0)),
        ],
        out_specs=[pl.BlockSpec((tm, wd), lambda i: (i, 0)) for wd in widths],
        compiler_params=_cparams(("arbitrary",)),
        name="in_proj",
    )(x, mod, g, w)


def _out_proj_kernel(om_ref, or_ref, oh_ref, gate_ref, x_ref, mod_ref, g_ref, w_ref, o_ref, *, ns, seg):
    sg = _silu(gate_ref[...])
    a0, a1 = MLA_WIDTH, MLA_WIDTH + RWKV_WIDTH
    acc = _bdot(om_ref[...] * sg[:, 0:a0], w_ref[0:a0, :])
    acc += _bdot(or_ref[...] * sg[:, a0:a1], w_ref[a0:a1, :])
    acc += _bdot(oh_ref[...] * sg[:, a1:], w_ref[a1:, :])
    a_gate = _rows(mod_ref[...][:, :, 2 * D_MODEL:], ns, seg)
    o_ref[...] = x_ref[...] + a_gate * _rms(acc, g_ref[...])


def _out_proj(o_mla, o_rwkv, o_hgrn, gate, x, mod, g, w, seq_len, tile=512):
    n = x.shape[0]
    tm, ns, seg = _seg_tiles(n, seq_len, tile)
    per_seq = seq_len // seg
    row = lambda wd: pl.BlockSpec((tm, wd), lambda i: (i, 0))
    return pl.pallas_call(
        functools.partial(_out_proj_kernel, ns=ns, seg=seg),
        out_shape=jax.ShapeDtypeStruct((n, D_MODEL), F32),
        grid=(n // tm,),
        in_specs=[
            row(MLA_WIDTH), row(RWKV_WIDTH), row(HGRN_WIDTH), row(D_MODEL), row(D_MODEL),
            pl.BlockSpec((ns, 1, 3 * D_MODEL), lambda i: (i // per_seq, 0, 0)),
            pl.BlockSpec((1, D_MODEL), lambda i: (0, 0)),
            pl.BlockSpec((D_MODEL, D_MODEL), lambda i: (0, 0)),
        ],
        out_specs=row(D_MODEL),
        compiler_params=_cparams(("arbitrary",)),
        name="out_proj",
    )(o_mla, o_rwkv, o_hgrn, gate, x, mod, g, w)


def _rope_tile(x, cos, sin):
    lane = lax.broadcasted_iota(jnp.int32, x.shape, 1)
    half = MLA_ROPE // 2
    rot = jnp.where(lane < PE_LO + half, pltpu.roll(x, LANE - half, axis=1), pltpu.roll(x, half, axis=1))
    return x * cos + rot * sin


def _mla_prep_kernel(p_ref, cos_ref, sin_ref, qg_ref, kvg_ref, wq_ref, *rest, absorbed):
    p = p_ref[...]
    cos, sin = cos_ref[...], sin_ref[...]
    cq = _rms(p[:, 0:MLA_Q_RANK], qg_ref[...])
    ckv = _rms(p[:, MLA_Q_RANK:MLA_Q_RANK + MLA_KV_RANK], kvg_ref[...])
    kpe = _rope_tile(p[:, MLA_Q_RANK + MLA_KV_RANK:], cos, sin)
    q = _bdot(cq, wq_ref[...])
    if absorbed:
        wuk_ref, qx_out, ckv_out, kpe_out = rest
        ext = wuk_ref.shape[2]
        for h in range(MLA_HEADS):
            qh = _rope_tile(q[:, h * QK_PAD:(h + 1) * QK_PAD], cos, sin)
            qx_out[:, h * ext:(h + 1) * ext] = _bdot(qh, wuk_ref[h]).astype(BF16)
    else:
        wk_ref, wv_ref, q_out, k_out, v_out, ckv_out, kpe_out = rest
        kn = _bdot(ckv, wk_ref[...])
        for h in range(MLA_HEADS):
            sl = slice(h * QK_PAD, (h + 1) * QK_PAD)
            q_out[:, sl] = _rope_tile(q[:, sl], cos, sin).astype(BF16)
            k_out[:, sl] = (kn[:, sl] + kpe).astype(BF16)
        v_out[...] = _bdot(ckv, wv_ref[...]).astype(BF16)
    ckv_out[...] = ckv
    kpe_out[...] = kpe


def _mla_prep(p_mla, cos, sin, qg, kvg, wq, extra_w, seq_len, absorbed, tile=512):
    n = p_mla.shape[0]
    tm = min(tile, n)
    rep = max(seq_len // tm, 1)
    hw = MLA_HEADS * QK_PAD
    row = lambda wd: pl.BlockSpec((tm, wd), lambda i: (i, 0))
    full = lambda a: pl.BlockSpec(a.shape, lambda i: (0,) * a.ndim)
    tab = pl.BlockSpec((tm, LANE), lambda i: (i % rep, 0))
    if absorbed:
        ext = extra_w[0].shape[2]
        outs = [(MLA_HEADS * ext, BF16), (MLA_KV_RANK, F32), (LANE, F32)]
    else:
        outs = [(hw, BF16), (hw, BF16), (hw, BF16), (MLA_KV_RANK, F32), (LANE, F32)]
    return pl.pallas_call(
        functools.partial(_mla_prep_kernel, absorbed=absorbed),
        out_shape=[jax.ShapeDtypeStruct((n, wd), dt) for wd, dt in outs],
        grid=(n // tm,),
        in_specs=[row(MLA_IN_PAD), tab, tab, full(qg), full(kvg), full(wq)] + [full(a) for a in extra_w],
        out_specs=[row(wd) for wd, _ in outs],
        compiler_params=_cparams(("arbitrary",)),
        name="mla_prep_absorbed" if absorbed else "mla_prep",
    )(p_mla, cos, sin, qg, kvg, wq, *extra_w)


def _softmax_step(s, v, m, l, acc):
    c = MLA_SCALE * math.log2(math.e)
    m_new = jnp.maximum(m, jnp.max(s, axis=-1, keepdims=True))
    alpha = jnp.exp2((m - m_new) * c)
    p = jnp.exp2((s - m_new) * c)
    l = alpha * l + jnp.sum(p, axis=-1, keepdims=True)
    acc = alpha * acc + lax.dot_general(p.astype(BF16), v, NN, preferred_element_type=F32)
    return m_new, l, acc


def _softmax_steps(ss, vs, ms, ls, accs):
    c = MLA_SCALE * math.log2(math.e)
    m_new = [jnp.maximum(m, jnp.max(s, axis=-1, keepdims=True)) for m, s in zip(ms, ss)]
    alpha = [jnp.exp2((m - mn) * c) for m, mn in zip(ms, m_new)]
    ps = [jnp.exp2((s - mn) * c) for s, mn in zip(ss, m_new)]
    ls = [a * l + jnp.sum(p, axis=-1, keepdims=True) for a, l, p in zip(alpha, ls, ps)]
    pv = [lax.dot_general(p.astype(BF16), v, NN, preferred_element_type=F32) for p, v in zip(ps, vs)]
    accs = [a * acc + x for a, acc, x in zip(alpha, accs, pv)]
    return m_new, ls, accs


def _attn_kernel(q_ref, k_ref, v_ref, o_ref, *, tq, heads):
    i = pl.program_id(2)
    sls = [slice(h * QK_PAD, (h + 1) * QK_PAD) for h in range(heads)]
    qs = [q_ref[:, sl] for sl in sls]

    def scores(j):
        rows = pl.ds(pl.multiple_of(j * tq, tq), tq)
        ss = [lax.dot_general(q, k_ref[rows, sl], NT, preferred_element_type=F32) for q, sl in zip(qs, sls)]
        return ss, [v_ref[rows, sl] for sl in sls]

    def body(j, carry):
        ss, vs = scores(j)
        return _softmax_steps(ss, vs, *carry)

    init = ([jnp.full((tq, 1), MASK_VALUE, F32)] * heads, [jnp.zeros((tq, 1), F32)] * heads,
            [jnp.zeros((tq, LANE), F32)] * heads)
    carry = lax.fori_loop(0, i, body, init)
    ss, vs = scores(i)
    r = lax.broadcasted_iota(jnp.int32, (tq, tq), 0)
    c = lax.broadcasted_iota(jnp.int32, (tq, tq), 1)
    _, ls, accs = _softmax_steps([jnp.where(c <= r, s, MASK_VALUE) for s in ss], vs, *carry)
    for hp in range(heads // 2):
        o_ref[:, hp * LANE:(hp + 1) * LANE] = accs[2 * hp] / ls[2 * hp] + accs[2 * hp + 1] / ls[2 * hp + 1]


def _attention_prompt(q, k, v, batch, seq_len, tq=512, heads=2):
    n = q.shape[0]
    tq = min(tq, seq_len)
    nq = seq_len // tq
    wide = heads * QK_PAD
    return pl.pallas_call(
        functools.partial(_attn_kernel, tq=tq, heads=heads),
        out_shape=jax.ShapeDtypeStruct((n, MLA_WIDTH), F32),
        grid=(batch, MLA_HEADS // heads, nq),
        in_specs=[
            pl.BlockSpec((tq, wide), lambda b, h, i: (b * nq + i, h)),
            pl.BlockSpec((seq_len, wide), lambda b, h, i: (b, h)),
            pl.BlockSpec((seq_len, wide), lambda b, h, i: (b, h)),
        ],
        out_specs=pl.BlockSpec((tq, wide // 2), lambda b, h, i: (b * nq + i, h)),
        compiler_params=_cparams(("arbitrary", "arbitrary", "arbitrary")),
        name="attn_prompt",
    )(q, k, v)


def _attn_paged_kernel(pt_ref, q_ref, ckv_new_ref, kpe_new_ref, wuv_ref, *rest, pages, t_new):
    ckv_refs = rest[:pages]
    kpe_refs = rest[pages:2 * pages]
    o_ref, m_ref, l_ref, acc_ref = rest[2 * pages:]
    g = pl.program_id(1)
    rows = MLA_HEADS * t_new

    @pl.when(g == 0)
    def _():
        m_ref[...] = jnp.full(m_ref.shape, MASK_VALUE, F32)
        l_ref[...] = jnp.zeros(l_ref.shape, F32)
        acc_ref[...] = jnp.zeros(acc_ref.shape, F32)

    q = q_ref[0]
    q_lat = q[:, 0:MLA_KV_RANK]
    q_pe = q[:, MLA_KV_RANK:MLA_KV_RANK + MLA_ROPE]
    c = MLA_SCALE * math.log2(math.e)
    ckvs, ss = [], []
    for pg in range(0, pages, 2):
        ckv = jnp.concatenate([ckv_refs[pg][0, 0].astype(BF16), ckv_refs[pg + 1][0, 0].astype(BF16)], axis=0)
        kpe = jnp.concatenate([kpe_refs[pg][0, 0].astype(BF16), kpe_refs[pg + 1][0, 0].astype(BF16)], axis=0)
        ckvs.append(ckv)
        ss.append(lax.dot_general(q_lat, ckv, NT, preferred_element_type=F32)
                  + lax.dot_general(q_pe, kpe, NT, preferred_element_type=F32))
    m = m_ref[...]
    m_new = m
    for s in ss:
        m_new = jnp.maximum(m_new, jnp.max(s, axis=-1, keepdims=True))
    alpha = jnp.exp2((m - m_new) * c)
    l = alpha * l_ref[...]
    acc = alpha * acc_ref[...]
    for s, ckv in zip(ss, ckvs):
        p = jnp.exp2((s - m_new) * c)
        l += jnp.sum(p, axis=-1, keepdims=True)
        acc += lax.dot_general(p.astype(BF16), ckv, NN, preferred_element_type=F32)
    m = m_new
    m_ref[...], l_ref[...], acc_ref[...] = m, l, acc

    @pl.when(g == pl.num_programs(1) - 1)
    def _():
        ckv = ckv_new_ref[0].astype(BF16)
        kpe = kpe_new_ref[0].astype(BF16)
        s = lax.dot_general(q_lat, ckv, NT, preferred_element_type=F32)
        s += lax.dot_general(q_pe, kpe, NT, preferred_element_type=F32)
        t_q = lax.broadcasted_iota(jnp.int32, (rows, t_new), 0) % t_new
        t_k = lax.broadcasted_iota(jnp.int32, (rows, t_new), 1)
        m2, l2, acc2 = _softmax_step(jnp.where(t_k <= t_q, s, MASK_VALUE), ckv, m, l, acc)
        lat = acc2 / l2
        for h in range(MLA_HEADS):
            o_ref[0, h * t_new:(h + 1) * t_new, :] = _bdot(lat[h * t_new:(h + 1) * t_new, :], wuv_ref[h])


def _attention_paged(q_ext, ckv_new, kpe_new, w_uv, cache_ckv, cache_kpe, page_table, layer, pages=16):
    batch, rows, ext = q_ext.shape
    t_new = rows // MLA_HEADS
    n_pages = page_table.shape[1]
    pages = min(pages, n_pages)
    assert n_pages % pages == 0
    page = lambda w, pg: pl.BlockSpec(
        (1, 1, PAGE_SIZE, w), lambda b, g, pt: (layer, pt[b * n_pages + g * pages + pg], 0, 0))
    grid_spec = pltpu.PrefetchScalarGridSpec(
        num_scalar_prefetch=1,
        grid=(batch, n_pages // pages),
        in_specs=[
            pl.BlockSpec((1, rows, ext), lambda b, g, pt: (b, 0, 0)),
            pl.BlockSpec((1, t_new, MLA_KV_RANK), lambda b, g, pt: (b, 0, 0)),
            pl.BlockSpec((1, t_new, MLA_ROPE), lambda b, g, pt: (b, 0, 0)),
            pl.BlockSpec(w_uv.shape, lambda b, g, pt: (0, 0, 0)),
        ] + [page(MLA_KV_RANK, pg) for pg in range(pages)] + [page(MLA_ROPE, pg) for pg in range(pages)],
        out_specs=pl.BlockSpec((1, rows, MLA_V), lambda b, g, pt: (b, 0, 0)),
        scratch_shapes=[pltpu.VMEM((rows, 1), F32), pltpu.VMEM((rows, 1), F32), pltpu.VMEM((rows, MLA_KV_RANK), F32)],
    )
    return pl.pallas_call(
        functools.partial(_attn_paged_kernel, pages=pages, t_new=t_new),
        out_shape=jax.ShapeDtypeStruct((batch, rows, MLA_V), F32),
        grid_spec=grid_spec,
        compiler_params=_cparams(("arbitrary", "arbitrary")),
        name="attn_paged",
    )(page_table.reshape(-1), q_ext, ckv_new, kpe_new, w_uv, *([cache_ckv] * pages), *([cache_kpe] * pages))


def _rwkv_prep_kernel(p_ref, halo_ref, mu_ref, w2_ref, wa0_ref, kk_ref, ka_ref, rk_ref, *rest, ns, seg, first):
    if first:
        feat_ref, vfirst_out = rest
    else:
        vfirst_ref, v0_ref, wva_ref, wvb_ref, feat_ref = rest
    w = RWKV_WIDTH
    p = p_ref[...]
    rows = p.shape[0]
    row = lax.broadcasted_iota(jnp.int32, p.shape, 0)
    prev = jnp.where(row % seg == 0, _rows(halo_ref[0], ns, seg), pltpu.roll(p, 1, axis=0))
    pm = p + mu_ref[...] * (prev - p)
    r, k, v, xwa = pm[:, 0:w], pm[:, w:2 * w], pm[:, 2 * w:3 * w], pm[:, 3 * w:]
    lane = lax.broadcasted_iota(jnp.int32, xwa.shape, 1)
    da = _dot3(jnp.where(lane < DECAY_RANK, jnp.tanh(xwa), xwa), w2_ref[...]) + wa0_ref[...]
    z = -da[:, 0:w]
    softplus = jnp.maximum(z, 0.0) + jnp.log1p(jnp.exp(-jnp.abs(z)))
    log_decay = -jnp.exp(-softplus - 0.5)
    a = jax.nn.sigmoid(da[:, w:])
    if first:
        vfirst_out[...] = v
    else:
        mix = jax.nn.sigmoid(v0_ref[...] + _dot3(_dot3(v, wva_ref[...]), wvb_ref[...]))
        v = v + (vfirst_ref[...] - v) * mix
    kk = k * kk_ref[...]
    k = k * (1.0 + (a - 1.0) * ka_ref[...])
    ones = _head_ones(w, RWKV_HEAD)
    kk = kk / jnp.maximum(jnp.sqrt(_dot_exact_rhs(kk * kk, ones)), 1e-12)
    bonus = _dot_exact_rhs(r * k * rk_ref[...], ones) * v
    for j, t in enumerate((r, log_decay, k, v, -kk, kk * a, bonus)):
        feat_ref[:, j * w:(j + 1) * w] = t


def _rwkv_prep(p_rwkv, halo, consts, v_first, vres, seq_len, tile=256):
    n = p_rwkv.shape[0]
    tm, ns, seg = _seg_tiles(n, seq_len, tile)
    first = v_first is None
    row = lambda wd: pl.BlockSpec((tm, wd), lambda i: (i, 0))
    full = lambda a: pl.BlockSpec(a.shape, lambda i: (0,) * a.ndim)
    ins = [p_rwkv, halo] + list(consts)
    specs = [row(RWKV_IN), pl.BlockSpec((1, ns, 1, RWKV_IN), lambda i: (i, 0, 0, 0))] + [full(a) for a in consts]
    feat = jax.ShapeDtypeStruct((n, RWKV_FEAT), F32)
    if first:
        out_shape = [feat, jax.ShapeDtypeStruct((n, RWKV_WIDTH), F32)]
        out_specs = [row(RWKV_FEAT), row(RWKV_WIDTH)]
    else:
        ins += [v_first] + list(vres)
        specs += [row(RWKV_WIDTH)] + [full(a) for a in vres]
        out_shape, out_specs = feat, row(RWKV_FEAT)
    return pl.pallas_call(
        functools.partial(_rwkv_prep_kernel, ns=ns, seg=seg, first=first),
        out_shape=out_shape,
        grid=(n // tm,),
        in_specs=specs,
        out_specs=out_specs,
        compiler_params=_cparams(("arbitrary",)),
        name="rwkv_prep",
    )(*ins)


def _rwkv_chunk_heads(r, lw_cum, lw, k, v, a, b, s0):
    n = len(r)
    c = r[0].shape[0]
    mm = _bdot
    each = lambda f, *xs: [f(*t) for t in zip(*xs)]
    ar = each(lambda a_, r_, cum, lw_: jnp.concatenate([a_ * jnp.exp(cum - lw_), r_ * jnp.exp(cum)], axis=0),
              a, r, lw_cum, lw)
    bk = each(lambda b_, k_, cum: jnp.concatenate([b_ * jnp.exp(-cum), k_ * jnp.exp(-cum)], axis=0),
              b, k, lw_cum)
    g = each(lambda x, y: mm(x, y, NT), ar, bk)
    gr = lax.broadcasted_iota(jnp.int32, (2 * c, 2 * c), 0)
    gc = lax.broadcasted_iota(jnp.int32, (2 * c, 2 * c), 1) % c
    keep = gc < gr - jnp.where(gr < c, 0, c - 1)
    g = [jnp.where(keep, x, 0.0) for x in g]
    a_ab = [x[0:c, 0:c] for x in g]
    eye = (lax.broadcasted_iota(jnp.int32, (c, c), 0) == lax.broadcasted_iota(jnp.int32, (c, c), 1)).astype(F32)
    t_inv = [eye + x for x in a_ab]
    x = each(mm, a_ab, a_ab)
    for _ in range(int(math.log2(c)) - 2):
        tx = each(lambda t_, x_: mm(jnp.concatenate([t_, x_], axis=0), x_), t_inv, x)
        t_inv = each(lambda t_, tx_: t_ + tx_[0:c, :], t_inv, tx)
        x = [tx_[c:, :] for tx_ in tx]
    t_inv = each(lambda t_, x_: t_ + mm(t_, x_), t_inv, x)
    ar_s0 = each(lambda x, y: mm(x, y, NT), ar, s0)
    g_v = each(lambda g_, v_: mm(g_[0:c, :], jnp.concatenate([jnp.zeros_like(v_), v_], axis=0)), g, v)
    u = each(lambda t_, x, y: mm(t_, x[0:c, :] + y), t_inv, ar_s0, g_v)
    uv = each(lambda u_, v_: jnp.concatenate([u_, v_], axis=0), u, v)
    o = each(lambda x, g_, uv_: x[c:, :] + mm(g_[c:, :], uv_), ar_s0, g, uv)
    bk_end = each(lambda b_, k_, cum: jnp.concatenate(
        [b_ * jnp.exp(cum[c - 1:c, :] - cum), k_ * jnp.exp(cum[c - 1:c, :] - cum)], axis=0), b, k, lw_cum)
    s1 = each(lambda s_, cum, uv_, e_: s_ * jnp.exp(cum[c - 1:c, :]) + mm(uv_, e_, TN), s0, lw_cum, uv, bk_end)
    assert len(o) == n
    return o, s1


def _rwkv_scan_kernel(feat_ref, s0_ref, gn_g_ref, gn_b_ref, o_ref, s_out_ref, s_ref, *, ns, chunk):
    j = pl.program_id(1)
    w = RWKV_WIDTH

    @pl.when(j == 0)
    def _():
        s_ref[...] = s0_ref[...]

    tri = (lax.broadcasted_iota(jnp.int32, (chunk, chunk), 1)
           <= lax.broadcasted_iota(jnp.int32, (chunk, chunk), 0)).astype(BF16)

    feats = [feat_ref[q] for q in range(ns)]
    cums = [_dot_exact_lhs(tri, f[:, w:2 * w]) for f in feats]
    inst = [(q, h) for q in range(ns) for h in range(RWKV_HEADS)]
    col = lambda i, h: slice(i * w + h * RWKV_HEAD, i * w + (h + 1) * RWKV_HEAD)
    part = lambda i: [feats[q][:, col(i, h)] for q, h in inst]
    o, s1 = _rwkv_chunk_heads(part(0), [cums[q][:, col(0, h)] for q, h in inst], part(1), part(2), part(3), part(4),
                              part(5), [s_ref[q, h] for q, h in inst])
    normed = []
    for (q, h), o_h, s_h in zip(inst, o, s1):
        s_ref[q, h] = s_h
        mean = jnp.mean(o_h, axis=-1, keepdims=True)
        var = jnp.mean(jnp.square(o_h - mean), axis=-1, keepdims=True)
        normed.append((o_h - mean) * lax.rsqrt(var + RWKV_GN_EPS))
    for q in range(ns):
        o_q = jnp.concatenate(normed[q * RWKV_HEADS:(q + 1) * RWKV_HEADS], axis=1)
        o_ref[q] = o_q * gn_g_ref[...] + gn_b_ref[...] + feats[q][:, 6 * w:]

    @pl.when(j == pl.num_programs(1) - 1)
    def _():
        s_out_ref[...] = s_ref[...]


def _rwkv_scan(feat, s0, gn_g, gn_b, seq_len, ns=4):
    n = feat.shape[0]
    batch = n // seq_len
    chunk = min(RWKV_CHUNK, seq_len)
    nc = seq_len // chunk
    ns = min(ns, batch)
    assert batch % ns == 0 and seq_len % chunk == 0
    st = (ns, RWKV_HEADS, RWKV_HEAD, RWKV_HEAD)
    o, s1 = pl.pallas_call(
        functools.partial(_rwkv_scan_kernel, ns=ns, chunk=chunk),
        out_shape=[jax.ShapeDtypeStruct((batch, seq_len, RWKV_WIDTH), F32), jax.ShapeDtypeStruct(s0.shape, F32)],
        grid=(batch // ns, nc),
        in_specs=[
            pl.BlockSpec((ns, chunk, RWKV_FEAT), lambda b, j: (b, j, 0)),
            pl.BlockSpec(st, lambda b, j: (b, 0, 0, 0)),
            pl.BlockSpec((1, RWKV_WIDTH), lambda b, j: (0, 0)),
            pl.BlockSpec((1, RWKV_WIDTH), lambda b, j: (0, 0)),
        ],
        out_specs=[
            pl.BlockSpec((ns, chunk, RWKV_WIDTH), lambda b, j: (b, j, 0)),
            pl.BlockSpec(st, lambda b, j: (b, 0, 0, 0)),
        ],
        scratch_shapes=[pltpu.VMEM(st, F32)],
        compiler_params=_cparams(("arbitrary", "arbitrary")),
        name="rwkv_scan",
    )(feat.reshape(batch, seq_len, RWKV_FEAT), s0, gn_g, gn_b)
    return o.reshape(n, RWKV_WIDTH), s1


def _hgrn_scan_kernel(p_ref, lb_ref, ng_ref, s0_ref, o_ref, s_out_ref, s_ref, q_s, k_s, v_s, b_s, *, ns, nsub, sub):
    j = pl.program_id(1)
    w = HGRN_WIDTH
    tile = ns * nsub * sub

    @pl.when(j == 0)
    def _():
        s_ref[...] = s0_ref[...]

    p = p_ref[...]
    lb = lb_ref[...]
    sig = jax.nn.sigmoid(p[:, w:2 * w])
    q_s[...] = _silu(p[:, 0:w]) * HGRN_HEAD ** -0.5
    k_s[...] = (1.0 - lb) * (1.0 - sig)
    v_s[...] = p[:, 2 * w:]
    log_f = jnp.log(jnp.maximum(lb + (1.0 - lb) * sig, MIN_FORGET))
    tr = lax.broadcasted_iota(jnp.int32, (tile, tile), 0)
    tc = lax.broadcasted_iota(jnp.int32, (tile, tile), 1)
    tri = ((tr // sub == tc // sub) & (tc <= tr)).astype(BF16)
    b_s[...] = _dot_exact_lhs(tri, log_f)
    ones = _head_ones(w, HGRN_HEAD)
    sub_row = lax.broadcasted_iota(jnp.int32, (sub, w), 0)
    nsc = ns * nsub
    blk = lambda ref, c: ref[c * sub:(c + 1) * sub, :]
    row = lambda ref, r: ref[r:r + 1, :]
    heads = [slice(h * HGRN_HEAD, (h + 1) * HGRN_HEAD) for h in range(HGRN_HEADS)]
    pair = [blk(q_s, c) * jnp.exp(jnp.minimum(blk(b_s, c) - row(b_s, c * sub + s), 0.0)) * row(k_s, c * sub + s)
            for c in range(nsc) for s in range(sub)]
    att = _bdot(jnp.concatenate(pair, axis=0), ones)
    intra = []
    for c in range(nsc):
        o = jnp.zeros((sub, w), F32)
        for s in range(sub):
            r = c * sub + s
            o = o + jnp.where(sub_row >= s, att[r * sub:(r + 1) * sub, :], 0.0) * row(v_s, r)
        intra.append(o)
    last = [row(b_s, (c + 1) * sub - 1) for c in range(nsc)]
    kv = [[_bdot(blk(v_s, c)[:, hs], (blk(k_s, c) * jnp.exp(last[c] - blk(b_s, c)))[:, hs], TN) for hs in heads]
          for c in range(nsc)]
    states = []
    for seq in range(ns):
        st = [s_ref[seq, h] for h in range(HGRN_HEADS)]
        for c in range(seq * nsub, (seq + 1) * nsub):
            states.append(st)
            gamma = jnp.exp(last[c])
            st = [st[h] * gamma[:, heads[h]] + kv[c][h] for h in range(HGRN_HEADS)]
        for h in range(HGRN_HEADS):
            s_ref[seq, h] = st[h]
    for c in range(nsc):
        q_in = blk(q_s, c) * jnp.exp(blk(b_s, c))
        o = intra[c] + jnp.concatenate([_bdot(q_in[:, heads[h]], states[c][h], NT) for h in range(HGRN_HEADS)], axis=1)
        intra[c] = o
    o = jnp.concatenate(intra, axis=0)
    ms = _dot_exact_rhs(o * o, ones) * (1.0 / HGRN_HEAD)
    o_ref[...] = o * lax.rsqrt(ms + NORM_EPS) * ng_ref[...]

    @pl.when(j == pl.num_programs(1) - 1)
    def _():
        s_out_ref[...] = s_ref[...]


def _hgrn_scan(p_hgrn, lb, norm_g, s0, seq_len, tile=256, ns_max=16):
    n = p_hgrn.shape[0]
    batch = n // seq_len
    sub = min(HGRN_SUB, seq_len)
    if seq_len >= tile:
        ns, nsub = 1, tile // sub
    else:
        ns, nsub = min(ns_max, batch), seq_len // sub
    rows = ns * nsub * sub
    nt = seq_len // (nsub * sub)
    assert batch % ns == 0 and seq_len % (nsub * sub) == 0
    st = (ns, HGRN_HEADS, HGRN_HEAD, HGRN_HEAD)
    return pl.pallas_call(
        functools.partial(_hgrn_scan_kernel, ns=ns, nsub=nsub, sub=sub),
        out_shape=[jax.ShapeDtypeStruct((n, HGRN_WIDTH), F32), jax.ShapeDtypeStruct(s0.shape, F32)],
        grid=(batch // ns, nt),
        in_specs=[
            pl.BlockSpec((rows, HGRN_IN), lambda b, j: (b * nt + j, 0)),
            pl.BlockSpec((1, HGRN_WIDTH), lambda b, j: (0, 0)),
            pl.BlockSpec((1, HGRN_WIDTH), lambda b, j: (0, 0)),
            pl.BlockSpec(st, lambda b, j: (b, 0, 0, 0)),
        ],
        out_specs=[
            pl.BlockSpec((rows, HGRN_WIDTH), lambda b, j: (b * nt + j, 0)),
            pl.BlockSpec(st, lambda b, j: (b, 0, 0, 0)),
        ],
        scratch_shapes=[pltpu.VMEM(st, F32)] + [pltpu.VMEM((rows, HGRN_WIDTH), F32)] * 4,
        compiler_params=_cparams(("arbitrary", "arbitrary")),
        name="hgrn_scan",
    )(p_hgrn, lb, norm_g, s0)


def _layer_weights(P, l):
    w_in = P['w_in'][l]
    c_q, c_kv = MLA_Q_RANK, MLA_Q_RANK + MLA_KV_RANK
    c_pe = c_kv + MLA_ROPE
    kpe_tile = jnp.zeros((D_MODEL, LANE), F32).at[:, PE_LO:PE_LO + MLA_ROPE].set(w_in[:, c_kv:c_pe])
    w_in_pad = jnp.concatenate([w_in[:, :c_kv], kpe_tile, w_in[:, c_pe:]], axis=1).astype(BF16)

    hd = MLA_NOPE + MLA_ROPE
    wq = P['mla_w_q_b'][l].reshape(MLA_Q_RANK, MLA_HEADS, hd)
    wq = jnp.pad(wq, ((0, 0), (0, 0), (0, QK_PAD - hd))).reshape(MLA_Q_RANK, MLA_HEADS * QK_PAD).astype(BF16)
    wkv = P['mla_w_kv_b'][l].reshape(MLA_KV_RANK, MLA_HEADS, MLA_NOPE + MLA_V)
    w_uk, w_uv = wkv[..., :MLA_NOPE], wkv[..., MLA_NOPE:]
    wk = jnp.pad(w_uk, ((0, 0), (0, 0), (0, QK_PAD - MLA_NOPE))).reshape(MLA_KV_RANK, MLA_HEADS * QK_PAD).astype(BF16)
    wv = jnp.zeros((MLA_KV_RANK, MLA_HEADS, QK_PAD), F32)
    for h in range(MLA_HEADS):
        off = (h % 2) * MLA_V
        wv = wv.at[:, h, off:off + MLA_V].set(w_uv[:, h, :])
    wv = wv.reshape(MLA_KV_RANK, MLA_HEADS * QK_PAD).astype(BF16)
    ext = MLA_KV_RANK + LANE
    wuk_ext = jnp.zeros((MLA_HEADS, QK_PAD, ext), F32)
    wuk_ext = wuk_ext.at[:, :MLA_NOPE, :MLA_KV_RANK].set(jnp.transpose(w_uk, (1, 2, 0)))
    wuk_ext = wuk_ext.at[:, PE_LO:PE_LO + MLA_ROPE, MLA_KV_RANK:MLA_KV_RANK + MLA_ROPE].set(
        jnp.broadcast_to(jnp.eye(MLA_ROPE, dtype=F32), (MLA_HEADS, MLA_ROPE, MLA_ROPE)))
    wuk_ext = wuk_ext.astype(BF16)
    w_uv_h = jnp.transpose(w_uv, (1, 0, 2)).astype(BF16)

    w = RWKV_WIDTH
    w2 = jnp.zeros((DECAY_RANK + ICLR_RANK, 2 * w), F32)
    w2 = w2.at[:DECAY_RANK, :w].set(P['rwkv_w_decay_b'][l]).at[DECAY_RANK:, w:].set(P['rwkv_w_iclr_b'][l])
    wa0 = jnp.concatenate([P['rwkv_w0'][l], P['rwkv_a0'][l]])[None, :]
    row = lambda a: a.reshape(1, -1)
    rwkv_consts = (row(P['rwkv_mu'][l]), w2, wa0, row(P['rwkv_k_k'][l]), row(P['rwkv_k_a'][l]), row(P['rwkv_r_k'][l]))
    vres = None
    if l > 0:
        wva = jnp.pad(P['rwkv_w_vres_a'][l - 1], ((0, 0), (0, LANE - VRES_RANK)))
        wvb = jnp.pad(P['rwkv_w_vres_b'][l - 1], ((0, LANE - VRES_RANK), (0, 0)))
        vres = (row(P['rwkv_v0'][l - 1]), wva, wvb)
    return dict(
        w_in=w_in_pad, pre_g=row(P['pre_norm_g'][l]), post_g=row(P['post_norm_g'][l]),
        w_out=P['w_out'][l].astype(BF16),
        q_g=row(P['mla_q_norm_g'][l]), kv_g=row(P['mla_kv_norm_g'][l]),
        wq=wq, wk=wk, wv=wv, wuk_ext=wuk_ext, w_uv=w_uv_h,
        rwkv_consts=rwkv_consts, vres=vres,
        gn_g=row(P['rwkv_gn_g'][l]), gn_b=row(P['rwkv_gn_b'][l]),
        hgrn_g=jnp.tile(P['hgrn_norm_g'][l], HGRN_HEADS)[None, :],
    )


def _rope_tables(pos, rows):
    half = MLA_ROPE // 2
    inv_freq = ROPE_BASE ** (-jnp.arange(half, dtype=F32) / half)
    ang = pos.astype(F32)[:, None] * inv_freq[None, :]
    cos, sin = jnp.cos(ang), jnp.sin(ang)
    t = pos.shape[0]
    ones, zeros = jnp.ones((t, PE_LO), F32), jnp.zeros((t, PE_LO), F32)
    pad = jnp.zeros((t, LANE - PE_LO - MLA_ROPE), F32)
    cos_t = jnp.concatenate([ones, cos, cos, pad], axis=1)
    sin_t = jnp.concatenate([zeros, -sin, sin, pad], axis=1)
    rep = rows // t
    return jnp.tile(cos_t, (rep, 1)), jnp.tile(sin_t, (rep, 1))


def _trunk(x, mods, pos, shift0, rwkv0, hgrn0, cache, LW, lb_all):
    batch, seq_len, _ = x.shape
    n = batch * seq_len
    x = x.reshape(n, D_MODEL)
    prep_tile = min(512, n)
    cos, sin = _rope_tables(pos, max(seq_len, prep_tile))
    rw_tile, rw_ns, rw_seg = _seg_tiles(n, seq_len, 256)
    v_first = None
    ckvs, kpes, rs, shs, hs = [], [], [], [], []
    for l in range(DEPTH):
        W = LW[l]
        mod = mods[l][:, None, :]
        p_mla, p_rwkv, p_hgrn, gate = _in_proj(x, mod, W['pre_g'], W['w_in'], seq_len)
        if cache is None:
            q, k, v, ckv, kpe = _mla_prep(p_mla, cos, sin, W['q_g'], W['kv_g'], W['wq'], (W['wk'], W['wv']),
                                          seq_len, absorbed=False)
            o_mla = _attention_prompt(q, k, v, batch, seq_len)
            kpe = kpe[:, PE_LO:PE_LO + MLA_ROPE]
        else:
            cache_ckv, cache_kpe, page_table = cache
            qx, ckv, kpe = _mla_prep(p_mla, cos, sin, W['q_g'], W['kv_g'], W['wq'], (W['wuk_ext'],),
                                     seq_len, absorbed=True)
            kpe = kpe[:, PE_LO:PE_LO + MLA_ROPE]
            ext = qx.shape[1] // MLA_HEADS
            qx = qx.reshape(batch, seq_len, MLA_HEADS, ext).transpose(0, 2, 1, 3).reshape(batch, MLA_HEADS * seq_len, ext)
            o = _attention_paged(qx, ckv.reshape(batch, seq_len, MLA_KV_RANK), kpe.reshape(batch, seq_len, MLA_ROPE),
                                 W['w_uv'], cache_ckv, cache_kpe, page_table, l)
            o_mla = o.reshape(batch, MLA_HEADS, seq_len, MLA_V).transpose(0, 2, 1, 3).reshape(n, MLA_WIDTH)
        p3 = p_rwkv.reshape(batch, seq_len, RWKV_IN)
        tails = p3[:, rw_seg - 1::rw_seg, :]
        halo = jnp.concatenate([shift0[l][:, None, :], tails[:, :-1, :]], axis=1).reshape(n // rw_tile, rw_ns, 1, RWKV_IN)
        if l == 0:
            feat, v_first = _rwkv_prep(p_rwkv, halo, W['rwkv_consts'], None, None, seq_len)
        else:
            feat = _rwkv_prep(p_rwkv, halo, W['rwkv_consts'], v_first, W['vres'], seq_len)
        o_rwkv, s_r = _rwkv_scan(feat, rwkv0[l], W['gn_g'], W['gn_b'], seq_len)
        o_hgrn, s_h = _hgrn_scan(p_hgrn, lb_all[l][None, :], W['hgrn_g'], jnp.swapaxes(hgrn0[l], -1, -2), seq_len)
        s_h = jnp.swapaxes(s_h, -1, -2)
        x = _out_proj(o_mla, o_rwkv, o_hgrn, gate, x, mod, W['post_g'], W['w_out'], seq_len)
        ckvs.append(ckv.reshape(batch, seq_len, MLA_KV_RANK))
        kpes.append(kpe.reshape(batch, seq_len, MLA_ROPE))
        rs.append(s_r)
        shs.append(p3[:, -1, :])
        hs.append(s_h)
    return (x.reshape(batch, seq_len, D_MODEL), jnp.stack(ckvs), jnp.stack(kpes), jnp.stack(rs), jnp.stack(shs),
            jnp.stack(hs))


def kernel(x_prompt, x_sample, c_prompt, c_sample, cache_ckv, cache_kpe, page_table, state_rwkv, state_rwkv_shift, state_hgrn, w_ada, b_ada, pre_norm_g, post_norm_g, w_in, mla_q_norm_g, mla_w_q_b, mla_kv_norm_g, mla_w_kv_b, rwkv_mu, rwkv_w0, rwkv_w_decay_b, rwkv_a0, rwkv_w_iclr_b, rwkv_k_k, rwkv_k_a, rwkv_r_k, rwkv_gn_g, rwkv_gn_b, rwkv_v0, rwkv_w_vres_a, rwkv_w_vres_b, hgrn_lb_raw, hgrn_norm_g, w_out):
    P = {
        'w_in': w_in, 'pre_norm_g': pre_norm_g, 'post_norm_g': post_norm_g,
        'mla_q_norm_g': mla_q_norm_g, 'mla_w_q_b': mla_w_q_b, 'mla_kv_norm_g': mla_kv_norm_g, 'mla_w_kv_b': mla_w_kv_b,
        'rwkv_mu': rwkv_mu, 'rwkv_w0': rwkv_w0, 'rwkv_w_decay_b': rwkv_w_decay_b, 'rwkv_a0': rwkv_a0,
        'rwkv_w_iclr_b': rwkv_w_iclr_b, 'rwkv_k_k': rwkv_k_k, 'rwkv_k_a': rwkv_k_a, 'rwkv_r_k': rwkv_r_k,
        'rwkv_gn_g': rwkv_gn_g, 'rwkv_gn_b': rwkv_gn_b, 'rwkv_v0': rwkv_v0, 'rwkv_w_vres_a': rwkv_w_vres_a,
        'rwkv_w_vres_b': rwkv_w_vres_b, 'hgrn_norm_g': hgrn_norm_g, 'w_out': w_out,
    }
    LW = [_layer_weights(P, l) for l in range(DEPTH)]
    lb_p = jax.nn.softmax(hgrn_lb_raw.astype(F32), axis=0)
    lb_all = jnp.cumsum(lb_p, axis=0) - lb_p[0]
    bp, tp = x_prompt.shape[:2]
    bs, ts = x_sample.shape[:2]
    mods = _ada(jnp.concatenate([c_prompt, c_sample], axis=0), w_ada, b_ada)
    past_len = page_table.shape[1] * PAGE_SIZE
    zeros = lambda *s: jnp.zeros(s, F32)
    out_p = _trunk(x_prompt, mods[:, :bp], jnp.arange(tp), zeros(DEPTH, bp, RWKV_IN),
                   zeros(DEPTH, bp, RWKV_HEADS, RWKV_HEAD, RWKV_HEAD), zeros(DEPTH, bp, HGRN_HEADS, HGRN_HEAD, HGRN_HEAD),
                   None, LW, lb_all)
    out_s = _trunk(x_sample, mods[:, bp:], past_len + jnp.arange(ts), state_rwkv_shift, state_rwkv, state_hgrn,
                   (cache_ckv, cache_kpe, page_table), LW, lb_all)
    return (out_p[0], out_s[0]) + out_p[1:] + out_s[1:]
```

```python
import functools
import math

import jax
import jax.numpy as jnp
from jax import lax
from jax.experimental import pallas as pl
from jax.experimental.pallas import tpu as pltpu

F32 = jnp.float32
BF16 = jnp.bfloat16

D_MODEL = 1024
DEPTH = 2
NORM_EPS = 1e-6
MASK_VALUE = -1e30
MIN_FORGET = 1e-20
PAGE_SIZE = 128
MLA_HEADS = 8
MLA_NOPE = 64
MLA_ROPE = 32
MLA_V = 64
MLA_Q_RANK = 384
MLA_KV_RANK = 256
MLA_WIDTH = MLA_HEADS * MLA_V
MLA_SCALE = (MLA_NOPE + MLA_ROPE) ** -0.5
ROPE_BASE = 10000.0
RWKV_WIDTH = 256
RWKV_HEAD = 64
RWKV_HEADS = 4
DECAY_RANK = 64
ICLR_RANK = 64
VRES_RANK = 32
RWKV_GN_EPS = 64e-5
RWKV_IN = 3 * RWKV_WIDTH + DECAY_RANK + ICLR_RANK
RWKV_CHUNK = 64
HGRN_WIDTH = 256
HGRN_HEADS = 4
HGRN_HEAD = 64
HGRN_IN = 3 * HGRN_WIDTH
HGRN_SUB = 16
LANE = 128
MLA_IN_PAD = MLA_Q_RANK + MLA_KV_RANK + LANE
QK_PAD = LANE
PE_LO = MLA_NOPE
IN_PAD = MLA_IN_PAD + RWKV_IN + HGRN_IN + D_MODEL
RWKV_FEAT = 7 * RWKV_WIDTH
VMEM_LIMIT = 56 * 1024 * 1024

NN = (((1,), (0,)), ((), ()))
NT = (((1,), (1,)), ((), ()))
TN = (((0,), (0,)), ((), ()))


def _bdot(a, b, dims=NN):
    return lax.dot_general(a.astype(BF16), b.astype(BF16), dims, preferred_element_type=F32)


def _split(a):
    hi = a.astype(BF16)
    lo = (a - hi.astype(F32)).astype(BF16)
    return hi, lo


def _dot3(a, b, dims=NN):
    ah, al = _split(a)
    bh, bl = _split(b)
    d = lambda x, y: lax.dot_general(x, y, dims, preferred_element_type=F32)
    return d(ah, bh) + (d(ah, bl) + d(al, bh))


def _dot_exact_rhs(a, b, dims=NN):
    a1 = a.astype(BF16)
    r1 = a - a1.astype(F32)
    a2 = r1.astype(BF16)
    a3 = (r1 - a2.astype(F32)).astype(BF16)
    bb = b.astype(BF16)
    d = lambda x: lax.dot_general(x, bb, dims, preferred_element_type=F32)
    return d(a1) + (d(a2) + d(a3))


def _dot_exact_lhs(m, x):
    x1 = x.astype(BF16)
    r1 = x - x1.astype(F32)
    x2 = r1.astype(BF16)
    x3 = (r1 - x2.astype(F32)).astype(BF16)
    d = lambda y: lax.dot_general(m, y, NN, preferred_element_type=F32)
    return d(x1) + (d(x2) + d(x3))


def _silu(x):
    return x * jax.nn.sigmoid(x)


def _rms(x, g):
    return x * lax.rsqrt(jnp.mean(x * x, axis=-1, keepdims=True) + NORM_EPS) * g


def _rows(m, ns, seg):
    w = m.shape[-1]
    return jnp.broadcast_to(m, (ns, seg, w)).reshape(ns * seg, w)


def _head_ones(width, head):
    r = lax.broadcasted_iota(jnp.int32, (width, width), 0) // head
    c = lax.broadcasted_iota(jnp.int32, (width, width), 1) // head
    return (r == c).astype(BF16)


def _cparams(sem):
    return pltpu.CompilerParams(dimension_semantics=sem, vmem_limit_bytes=VMEM_LIMIT)


def _seg_tiles(n_rows, seq_len, tile):
    tile = min(tile, n_rows)
    seg = min(seq_len, tile)
    ns = tile // seg
    assert ns * seg == tile and n_rows % tile == 0 and seq_len % seg == 0
    return tile, ns, seg


def _ada_kernel(c_ref, w_ref, b_ref, o_ref):
    o_ref[0] = _dot3(_silu(c_ref[...]), w_ref[0]) + b_ref[0]


def _ada(c_all, w_ada, b_ada):
    bt = c_all.shape[0]
    tn = 768
    return pl.pallas_call(
        _ada_kernel,
        out_shape=jax.ShapeDtypeStruct((DEPTH, bt, 3 * D_MODEL), F32),
        grid=(DEPTH, 3 * D_MODEL // tn),
        in_specs=[
            pl.BlockSpec((bt, D_MODEL), lambda l, j: (0, 0)),
            pl.BlockSpec((1, D_MODEL, tn), lambda l, j: (l, 0, j)),
            pl.BlockSpec((1, 1, tn), lambda l, j: (l, 0, j)),
        ],
        out_specs=pl.BlockSpec((1, bt, tn), lambda l, j: (l, 0, j)),
        compiler_params=_cparams(("arbitrary", "arbitrary")),
        name="ada",
    )(c_all, w_ada, b_ada.reshape(DEPTH, 1, 3 * D_MODEL))


def _in_proj_kernel(x_ref, mod_ref, g_ref, w_ref, o_mla, o_rwkv, o_hgrn, o_gate, *, ns, seg):
    mod = mod_ref[...]
    shift = _rows(mod[:, :, 0:D_MODEL], ns, seg)
    scale = _rows(mod[:, :, D_MODEL:2 * D_MODEL], ns, seg)
    h = (_rms(x_ref[...], g_ref[...]) * (1.0 + scale) + shift).astype(BF16)
    c0 = 0
    for o_ref in (o_mla, o_rwkv, o_hgrn, o_gate):
        c1 = c0 + o_ref.shape[1]
        o_ref[...] = lax.dot_general(h, w_ref[:, c0:c1], NN, preferred_element_type=F32)
        c0 = c1


def _in_proj(x, mod, g, w, seq_len, tile=512):
    n = x.shape[0]
    tm, ns, seg = _seg_tiles(n, seq_len, tile)
    per_seq = seq_len // seg
    widths = (MLA_IN_PAD, RWKV_IN, HGRN_IN, D_MODEL)
    return pl.pallas_call(
        functools.partial(_in_proj_kernel, ns=ns, seg=seg),
        out_shape=[jax.ShapeDtypeStruct((n, wd), F32) for wd in widths],
        grid=(n // tm,),
        in_specs=[
            pl.BlockSpec((tm, D_MODEL), lambda i: (i, 0)),
            pl.BlockSpec((ns, 1, 3 * D_MODEL), lambda i: (i // per_seq, 0, 0)),
            pl.BlockSpec((1, D_MODEL), lambda i: (0, 0)),
            pl.BlockSpec((D_MODEL, IN_PAD), lambda i: (0, 0)),
        ],
        out_specs=[pl.BlockSpec((tm, wd), lambda i: (i, 0)) for wd in widths],
        compiler_params=_cparams(("arbitrary",)),
        name="in_proj",
    )(x, mod, g, w)


def _out_proj_kernel(om_ref, or_ref, oh_ref, gate_ref, x_ref, mod_ref, g_ref, w_ref, o_ref, *, ns, seg):
    sg = _silu(gate_ref[...])
    a0, a1 = MLA_WIDTH, MLA_WIDTH + RWKV_WIDTH
    acc = _bdot(om_ref[...] * sg[:, 0:a0], w_ref[0:a0, :])
    acc += _bdot(or_ref[...] * sg[:, a0:a1], w_ref[a0:a1, :])
    acc += _bdot(oh_ref[...] * sg[:, a1:], w_ref[a1:, :])
    a_gate = _rows(mod_ref[...][:, :, 2 * D_MODEL:], ns, seg)
    o_ref[...] = x_ref[...] + a_gate * _rms(acc, g_ref[...])


def _out_proj(o_mla, o_rwkv, o_hgrn, gate, x, mod, g, w, seq_len, tile=512):
    n = x.shape[0]
    tm, ns, seg = _seg_tiles(n, seq_len, tile)
    per_seq = seq_len // seg
    row = lambda wd: pl.BlockSpec((tm, wd), lambda i: (i, 0))
    return pl.pallas_call(
        functools.partial(_out_proj_kernel, ns=ns, seg=seg),
        out_shape=jax.ShapeDtypeStruct((n, D_MODEL), F32),
        grid=(n // tm,),
        in_specs=[
            row(MLA_WIDTH), row(RWKV_WIDTH), row(HGRN_WIDTH), row(D_MODEL), row(D_MODEL),
            pl.BlockSpec((ns, 1, 3 * D_MODEL), lambda i: (i // per_seq, 0, 0)),
            pl.BlockSpec((1, D_MODEL), lambda i: (0, 0)),
            pl.BlockSpec((D_MODEL, D_MODEL), lambda i: (0, 0)),
        ],
        out_specs=row(D_MODEL),
        compiler_params=_cparams(("arbitrary",)),
        name="out_proj",
    )(o_mla, o_rwkv, o_hgrn, gate, x, mod, g, w)


def _rope_tile(x, cos, sin):
    lane = lax.broadcasted_iota(jnp.int32, x.shape, 1)
    half = MLA_ROPE // 2
    rot = jnp.where(lane < PE_LO + half, pltpu.roll(x, LANE - half, axis=1), pltpu.roll(x, half, axis=1))
    return x * cos + rot * sin


def _mla_prep_kernel(p_ref, cos_ref, sin_ref, qg_ref, kvg_ref, wq_ref, *rest, absorbed):
    p = p_ref[...]
    cos, sin = cos_ref[...], sin_ref[...]
    cq = _rms(p[:, 0:MLA_Q_RANK], qg_ref[...])
    ckv = _rms(p[:, MLA_Q_RANK:MLA_Q_RANK + MLA_KV_RANK], kvg_ref[...])
    kpe = _rope_tile(p[:, MLA_Q_RANK + MLA_KV_RANK:], cos, sin)
    q = _bdot(cq, wq_ref[...])
    if absorbed:
        wuk_ref, qx_out, ckv_out, kpe_out = rest
        ext = wuk_ref.shape[2]
        for h in range(MLA_HEADS):
            qh = _rope_tile(q[:, h * QK_PAD:(h + 1) * QK_PAD], cos, sin)
            qx_out[:, h * ext:(h + 1) * ext] = _bdot(qh, wuk_ref[h]).astype(BF16)
    else:
        wk_ref, wv_ref, q_out, k_out, v_out, ckv_out, kpe_out = rest
        kn = _bdot(ckv, wk_ref[...])
        for h in range(MLA_HEADS):
            sl = slice(h * QK_PAD, (h + 1) * QK_PAD)
            q_out[:, sl] = _rope_tile(q[:, sl], cos, sin).astype(BF16)
            k_out[:, sl] = (kn[:, sl] + kpe).astype(BF16)
        v_out[...] = _bdot(ckv, wv_ref[...]).astype(BF16)
    ckv_out[...] = ckv
    kpe_out[...] = kpe


def _mla_prep(p_mla, cos, sin, qg, kvg, wq, extra_w, seq_len, absorbed, tile=512):
    n = p_mla.shape[0]
    tm = min(tile, n)
    rep = max(seq_len // tm, 1)
    hw = MLA_HEADS * QK_PAD
    row = lambda wd: pl.BlockSpec((tm, wd), lambda i: (i, 0))
    full = lambda a: pl.BlockSpec(a.shape, lambda i: (0,) * a.ndim)
    tab = pl.BlockSpec((tm, LANE), lambda i: (i % rep, 0))
    if absorbed:
        ext = extra_w[0].shape[2]
        outs = [(MLA_HEADS * ext, BF16), (MLA_KV_RANK, F32), (LANE, F32)]
    else:
        outs = [(hw, BF16), (hw, BF16), (hw, BF16), (MLA_KV_RANK, F32), (LANE, F32)]
    return pl.pallas_call(
        functools.partial(_mla_prep_kernel, absorbed=absorbed),
        out_shape=[jax.ShapeDtypeStruct((n, wd), dt) for wd, dt in outs],
        grid=(n // tm,),
        in_specs=[row(MLA_IN_PAD), tab, tab, full(qg), full(kvg), full(wq)] + [full(a) for a in extra_w],
        out_specs=[row(wd) for wd, _ in outs],
        compiler_params=_cparams(("arbitrary",)),
        name="mla_prep_absorbed" if absorbed else "mla_prep",
    )(p_mla, cos, sin, qg, kvg, wq, *extra_w)


def _softmax_step(s, v, m, l, acc):
    c = MLA_SCALE * math.log2(math.e)
    m_new = jnp.maximum(m, jnp.max(s, axis=-1, keepdims=True))
    alpha = jnp.exp2((m - m_new) * c)
    p = jnp.exp2((s - m_new) * c)
    l = alpha * l + jnp.sum(p, axis=-1, keepdims=True)
    acc = alpha * acc + lax.dot_general(p.astype(BF16), v, NN, preferred_element_type=F32)
    return m_new, l, acc


def _softmax_steps(ss, vs, ms, ls, accs):
    c = MLA_SCALE * math.log2(math.e)
    m_new = [jnp.maximum(m, jnp.max(s, axis=-1, keepdims=True)) for m, s in zip(ms, ss)]
    alpha = [jnp.exp2((m - mn) * c) for m, mn in zip(ms, m_new)]
    ps = [jnp.exp2((s - mn) * c) for s, mn in zip(ss, m_new)]
    ls = [a * l + jnp.sum(p, axis=-1, keepdims=True) for a, l, p in zip(alpha, ls, ps)]
    pv = [lax.dot_general(p.astype(BF16), v, NN, preferred_element_type=F32) for p, v in zip(ps, vs)]
    accs = [a * acc + x for a, acc, x in zip(alpha, accs, pv)]
    return m_new, ls, accs


def _attn_kernel(q_ref, k_ref, v_ref, o_ref, *, tq, heads):
    i = pl.program_id(2)
    sls = [slice(h * QK_PAD, (h + 1) * QK_PAD) for h in range(heads)]
    qs = [q_ref[:, sl] for sl in sls]

    def scores(j):
        rows = pl.ds(pl.multiple_of(j * tq, tq), tq)
        ss = [lax.dot_general(q, k_ref[rows, sl], NT, preferred_element_type=F32) for q, sl in zip(qs, sls)]
        return ss, [v_ref[rows, sl] for sl in sls]

    def body(j, carry):
        ss, vs = scores(j)
        return _softmax_steps(ss, vs, *carry)

    init = ([jnp.full((tq, 1), MASK_VALUE, F32)] * heads, [jnp.zeros((tq, 1), F32)] * heads,
            [jnp.zeros((tq, LANE), F32)] * heads)
    carry = lax.fori_loop(0, i, body, init)
    ss, vs = scores(i)
    r = lax.broadcasted_iota(jnp.int32, (tq, tq), 0)
    c = lax.broadcasted_iota(jnp.int32, (tq, tq), 1)
    _, ls, accs = _softmax_steps([jnp.where(c <= r, s, MASK_VALUE) for s in ss], vs, *carry)
    for hp in range(heads // 2):
        o_ref[:, hp * LANE:(hp + 1) * LANE] = accs[2 * hp] / ls[2 * hp] + accs[2 * hp + 1] / ls[2 * hp + 1]


def _attention_prompt(q, k, v, batch, seq_len, tq=512, heads=2):
    n = q.shape[0]
    tq = min(tq, seq_len)
    nq = seq_len // tq
    wide = heads * QK_PAD
    return pl.pallas_call(
        functools.partial(_attn_kernel, tq=tq, heads=heads),
        out_shape=jax.ShapeDtypeStruct((n, MLA_WIDTH), F32),
        grid=(batch, MLA_HEADS // heads, nq),
        in_specs=[
            pl.BlockSpec((tq, wide), lambda b, h, i: (b * nq + i, h)),
            pl.BlockSpec((seq_len, wide), lambda b, h, i: (b, h)),
            pl.BlockSpec((seq_len, wide), lambda b, h, i: (b, h)),
        ],
        out_specs=pl.BlockSpec((tq, wide // 2), lambda b, h, i: (b * nq + i, h)),
        compiler_params=_cparams(("arbitrary", "arbitrary", "arbitrary")),
        name="attn_prompt",
    )(q, k, v)


def _attn_paged_kernel(pt_ref, q_ref, ckv_new_ref, kpe_new_ref, wuv_ref, ckv_hbm, kpe_hbm, o_ref,
                       ckv_buf, kpe_buf, sem, *, pages, group, t_new, layer):
    b = pl.program_id(0)
    slot = b % 2
    rows = MLA_HEADS * t_new

    def page_copies(elem, slot_idx, lookup):
        out = []
        for pg in range(pages):
            page = pt_ref[elem * pages + pg] if lookup else 0
            out.append(pltpu.make_async_copy(ckv_hbm.at[layer, page], ckv_buf.at[slot_idx, pg], sem.at[slot_idx, 0]))
            out.append(pltpu.make_async_copy(kpe_hbm.at[layer, page], kpe_buf.at[slot_idx, pg], sem.at[slot_idx, 1]))
        return out

    @pl.when(b == 0)
    def _():
        for cp in page_copies(0, 0, True):
            cp.start()

    @pl.when(b + 1 < pl.num_programs(0))
    def _():
        for cp in page_copies(b + 1, 1 - slot, True):
            cp.start()

    for cp in page_copies(b, slot, False):
        cp.wait()

    q = q_ref[0]
    q_lat = q[:, 0:MLA_KV_RANK]
    q_pe = q[:, MLA_KV_RANK:MLA_KV_RANK + MLA_ROPE]
    def scores(n):
        pg = n * group
        ckv = jnp.concatenate([ckv_buf[slot, pg + i].astype(BF16) for i in range(group)], axis=0)
        kpe_t = jnp.concatenate([kpe_buf[slot, pg + i].astype(BF16) for i in range(group)], axis=1)
        return (lax.dot_general(q_lat, ckv, NT, preferred_element_type=F32)
                + lax.dot_general(q_pe, kpe_t, NN, preferred_element_type=F32)), ckv

    ckv = ckv_new_ref[0].astype(BF16)
    s = lax.dot_general(q_lat, ckv, NT, preferred_element_type=F32)
    s += lax.dot_general(q_pe, kpe_new_ref[0].astype(BF16), NT, preferred_element_type=F32)
    t_q = lax.broadcasted_iota(jnp.int32, (rows, t_new), 0) % t_new
    t_k = lax.broadcasted_iota(jnp.int32, (rows, t_new), 1)
    ahead = scores(0)
    m, l, acc = _softmax_step(jnp.where(t_k <= t_q, s, MASK_VALUE), ckv, jnp.full((rows, 1), MASK_VALUE, F32),
                              jnp.zeros((rows, 1), F32), jnp.zeros((rows, MLA_KV_RANK), F32))
    n_chunks = pages // group
    for n in range(n_chunks):
        s, ckv = ahead
        if n + 1 < n_chunks:
            ahead = scores(n + 1)
        m, l, acc = _softmax_step(s, ckv, m, l, acc)
    lat = acc / l
    for h in range(MLA_HEADS):
        o_ref[0, h * t_new:(h + 1) * t_new, :] = _bdot(lat[h * t_new:(h + 1) * t_new, :], wuv_ref[h])


def _attention_paged(q_ext, ckv_new, kpe_new, w_uv, cache_ckv, cache_kpe, page_table, layer, group=16):
    batch, rows, ext = q_ext.shape
    t_new = rows // MLA_HEADS
    pages = page_table.shape[1]
    group = min(group, pages)
    assert pages % group == 0
    kpe_t = jnp.swapaxes(cache_kpe, 2, 3)
    grid_spec = pltpu.PrefetchScalarGridSpec(
        num_scalar_prefetch=1,
        grid=(batch,),
        in_specs=[
            pl.BlockSpec((1, rows, ext), lambda b, pt: (b, 0, 0)),
            pl.BlockSpec((1, t_new, MLA_KV_RANK), lambda b, pt: (b, 0, 0)),
            pl.BlockSpec((1, t_new, MLA_ROPE), lambda b, pt: (b, 0, 0)),
            pl.BlockSpec(w_uv.shape, lambda b, pt: (0, 0, 0)),
            pl.BlockSpec(memory_space=pl.ANY),
            pl.BlockSpec(memory_space=pl.ANY),
        ],
        out_specs=pl.BlockSpec((1, rows, MLA_V), lambda b, pt: (b, 0, 0)),
        scratch_shapes=[
            pltpu.VMEM((2, pages, PAGE_SIZE, MLA_KV_RANK), F32),
            pltpu.VMEM((2, pages, MLA_ROPE, PAGE_SIZE), F32),
            pltpu.SemaphoreType.DMA((2, 2)),
        ],
    )
    return pl.pallas_call(
        functools.partial(_attn_paged_kernel, pages=pages, group=group, t_new=t_new, layer=layer),
        out_shape=jax.ShapeDtypeStruct((batch, rows, MLA_V), F32),
        grid_spec=grid_spec,
        compiler_params=_cparams(("arbitrary",)),
        name="attn_paged",
    )(page_table.reshape(-1), q_ext, ckv_new, kpe_new, w_uv, cache_ckv, kpe_t)


def _rwkv_prep_kernel(p_ref, halo_ref, mu_ref, w2_ref, wa0_ref, kk_ref, ka_ref, rk_ref, *rest, ns, seg, first):
    if first:
        feat_ref, vfirst_out = rest
    else:
        vfirst_ref, v0_ref, wva_ref, wvb_ref, feat_ref = rest
    w = RWKV_WIDTH
    p = p_ref[...]
    rows = p.shape[0]
    row = lax.broadcasted_iota(jnp.int32, p.shape, 0)
    prev = jnp.where(row % seg == 0, _rows(halo_ref[0], ns, seg), pltpu.roll(p, 1, axis=0))
    pm = p + mu_ref[...] * (prev - p)
    r, k, v, xwa = pm[:, 0:w], pm[:, w:2 * w], pm[:, 2 * w:3 * w], pm[:, 3 * w:]
    lane = lax.broadcasted_iota(jnp.int32, xwa.shape, 1)
    da = _dot3(jnp.where(lane < DECAY_RANK, jnp.tanh(xwa), xwa), w2_ref[...]) + wa0_ref[...]
    z = -da[:, 0:w]
    softplus = jnp.maximum(z, 0.0) + jnp.log1p(jnp.exp(-jnp.abs(z)))
    log_decay = -jnp.exp(-softplus - 0.5)
    a = jax.nn.sigmoid(da[:, w:])
    if first:
        vfirst_out[...] = v
    else:
        mix = jax.nn.sigmoid(v0_ref[...] + _dot3(_dot3(v, wva_ref[...]), wvb_ref[...]))
        v = v + (vfirst_ref[...] - v) * mix
    kk = k * kk_ref[...]
    k = k * (1.0 + (a - 1.0) * ka_ref[...])
    ones = _head_ones(w, RWKV_HEAD)
    kk = kk / jnp.maximum(jnp.sqrt(_dot_exact_rhs(kk * kk, ones)), 1e-12)
    bonus = _dot_exact_rhs(r * k * rk_ref[...], ones) * v
    for j, t in enumerate((r, log_decay, k, v, -kk, kk * a, bonus)):
        feat_ref[:, j * w:(j + 1) * w] = t


def _rwkv_prep(p_rwkv, halo, consts, v_first, vres, seq_len, tile=256):
    n = p_rwkv.shape[0]
    tm, ns, seg = _seg_tiles(n, seq_len, tile)
    first = v_first is None
    row = lambda wd: pl.BlockSpec((tm, wd), lambda i: (i, 0))
    full = lambda a: pl.BlockSpec(a.shape, lambda i: (0,) * a.ndim)
    ins = [p_rwkv, halo] + list(consts)
    specs = [row(RWKV_IN), pl.BlockSpec((1, ns, 1, RWKV_IN), lambda i: (i, 0, 0, 0))] + [full(a) for a in consts]
    feat = jax.ShapeDtypeStruct((n, RWKV_FEAT), F32)
    if first:
        out_shape = [feat, jax.ShapeDtypeStruct((n, RWKV_WIDTH), F32)]
        out_specs = [row(RWKV_FEAT), row(RWKV_WIDTH)]
    else:
        ins += [v_first] + list(vres)
        specs += [row(RWKV_WIDTH)] + [full(a) for a in vres]
        out_shape, out_specs = feat, row(RWKV_FEAT)
    return pl.pallas_call(
        functools.partial(_rwkv_prep_kernel, ns=ns, seg=seg, first=first),
        out_shape=out_shape,
        grid=(n // tm,),
        in_specs=specs,
        out_specs=out_specs,
        compiler_params=_cparams(("arbitrary",)),
        name="rwkv_prep",
    )(*ins)


def _rwkv_chunk_heads(r, lw_cum, lw, k, v, a, b, s0):
    n = len(r)
    c = r[0].shape[0]
    mm = _bdot
    each = lambda f, *xs: [f(*t) for t in zip(*xs)]
    ar = each(lambda a_, r_, cum, lw_: jnp.concatenate([a_ * jnp.exp(cum - lw_), r_ * jnp.exp(cum)], axis=0),
              a, r, lw_cum, lw)
    bk = each(lambda b_, k_, cum: jnp.concatenate([b_ * jnp.exp(-cum), k_ * jnp.exp(-cum)], axis=0),
              b, k, lw_cum)
    g = each(lambda x, y: mm(x, y, NT), ar, bk)
    gr = lax.broadcasted_iota(jnp.int32, (2 * c, 2 * c), 0)
    gc = lax.broadcasted_iota(jnp.int32, (2 * c, 2 * c), 1) % c
    keep = gc < gr - jnp.where(gr < c, 0, c - 1)
    g = [jnp.where(keep, x, 0.0) for x in g]
    a_ab = [x[0:c, 0:c] for x in g]
    eye = (lax.broadcasted_iota(jnp.int32, (c, c), 0) == lax.broadcasted_iota(jnp.int32, (c, c), 1)).astype(F32)
    t_inv = [eye + x for x in a_ab]
    x = each(mm, a_ab, a_ab)
    for _ in range(int(math.log2(c)) - 2):
        tx = each(lambda t_, x_: mm(jnp.concatenate([t_, x_], axis=0), x_), t_inv, x)
        t_inv = each(lambda t_, tx_: t_ + tx_[0:c, :], t_inv, tx)
        x = [tx_[c:, :] for tx_ in tx]
    t_inv = each(lambda t_, x_: t_ + mm(t_, x_), t_inv, x)
    ar_s0 = each(lambda x, y: mm(x, y, NT), ar, s0)
    g_v = each(lambda g_, v_: mm(g_[0:c, :], jnp.concatenate([jnp.zeros_like(v_), v_], axis=0)), g, v)
    u = each(lambda t_, x, y: mm(t_, x[0:c, :] + y), t_inv, ar_s0, g_v)
    uv = each(lambda u_, v_: jnp.concatenate([u_, v_], axis=0), u, v)
    o = each(lambda x, g_, uv_: x[c:, :] + mm(g_[c:, :], uv_), ar_s0, g, uv)
    bk_end = each(lambda b_, k_, cum: jnp.concatenate(
        [b_ * jnp.exp(cum[c - 1:c, :] - cum), k_ * jnp.exp(cum[c - 1:c, :] - cum)], axis=0), b, k, lw_cum)
    s1 = each(lambda s_, cum, uv_, e_: s_ * jnp.exp(cum[c - 1:c, :]) + mm(uv_, e_, TN), s0, lw_cum, uv, bk_end)
    assert len(o) == n
    return o, s1


def _rwkv_scan_kernel(feat_ref, s0_ref, gn_g_ref, gn_b_ref, o_ref, s_out_ref, s_ref, *, ns, chunk):
    j = pl.program_id(1)
    w = RWKV_WIDTH

    @pl.when(j == 0)
    def _():
        s_ref[...] = s0_ref[...]

    tri = (lax.broadcasted_iota(jnp.int32, (chunk, chunk), 1)
           <= lax.broadcasted_iota(jnp.int32, (chunk, chunk), 0)).astype(BF16)

    feats = [feat_ref[q] for q in range(ns)]
    cums = [_dot_exact_lhs(tri, f[:, w:2 * w]) for f in feats]
    inst = [(q, h) for q in range(ns) for h in range(RWKV_HEADS)]
    col = lambda i, h: slice(i * w + h * RWKV_HEAD, i * w + (h + 1) * RWKV_HEAD)
    part = lambda i: [feats[q][:, col(i, h)] for q, h in inst]
    o, s1 = _rwkv_chunk_heads(part(0), [cums[q][:, col(0, h)] for q, h in inst], part(1), part(2), part(3), part(4),
                              part(5), [s_ref[q, h] for q, h in inst])
    normed = []
    for (q, h), o_h, s_h in zip(inst, o, s1):
        s_ref[q, h] = s_h
        mean = jnp.mean(o_h, axis=-1, keepdims=True)
        var = jnp.mean(jnp.square(o_h - mean), axis=-1, keepdims=True)
        normed.append((o_h - mean) * lax.rsqrt(var + RWKV_GN_EPS))
    for q in range(ns):
        o_q = jnp.concatenate(normed[q * RWKV_HEADS:(q + 1) * RWKV_HEADS], axis=1)
        o_ref[q] = o_q * gn_g_ref[...] + gn_b_ref[...] + feats[q][:, 6 * w:]

    @pl.when(j == pl.num_programs(1) - 1)
    def _():
        s_out_ref[...] = s_ref[...]


def _rwkv_scan(feat, s0, gn_g, gn_b, seq_len, ns=4):
    n = feat.shape[0]
    batch = n // seq_len
    chunk = min(RWKV_CHUNK, seq_len)
    nc = seq_len // chunk
    ns = min(ns, batch)
    assert batch % ns == 0 and seq_len % chunk == 0
    st = (ns, RWKV_HEADS, RWKV_HEAD, RWKV_HEAD)
    o, s1 = pl.pallas_call(
        functools.partial(_rwkv_scan_kernel, ns=ns, chunk=chunk),
        out_shape=[jax.ShapeDtypeStruct((batch, seq_len, RWKV_WIDTH), F32), jax.ShapeDtypeStruct(s0.shape, F32)],
        grid=(batch // ns, nc),
        in_specs=[
            pl.BlockSpec((ns, chunk, RWKV_FEAT), lambda b, j: (b, j, 0)),
            pl.BlockSpec(st, lambda b, j: (b, 0, 0, 0)),
            pl.BlockSpec((1, RWKV_WIDTH), lambda b, j: (0, 0)),
            pl.BlockSpec((1, RWKV_WIDTH), lambda b, j: (0, 0)),
        ],
        out_specs=[
            pl.BlockSpec((ns, chunk, RWKV_WIDTH), lambda b, j: (b, j, 0)),
            pl.BlockSpec(st, lambda b, j: (b, 0, 0, 0)),
        ],
        scratch_shapes=[pltpu.VMEM(st, F32)],
        compiler_params=_cparams(("arbitrary", "arbitrary")),
        name="rwkv_scan",
    )(feat.reshape(batch, seq_len, RWKV_FEAT), s0, gn_g, gn_b)
    return o.reshape(n, RWKV_WIDTH), s1


def _hgrn_scan_kernel(p_ref, lb_ref, ng_ref, s0_ref, o_ref, s_out_ref, s_ref, q_s, k_s, v_s, b_s, *, ns, nsub, sub):
    j = pl.program_id(1)
    w = HGRN_WIDTH
    tile = ns * nsub * sub

    @pl.when(j == 0)
    def _():
        s_ref[...] = s0_ref[...]

    p = p_ref[...]
    lb = lb_ref[...]
    sig = jax.nn.sigmoid(p[:, w:2 * w])
    q_s[...] = _silu(p[:, 0:w]) * HGRN_HEAD ** -0.5
    k_s[...] = (1.0 - lb) * (1.0 - sig)
    v_s[...] = p[:, 2 * w:]
    log_f = jnp.log(jnp.maximum(lb + (1.0 - lb) * sig, MIN_FORGET))
    tr = lax.broadcasted_iota(jnp.int32, (tile, tile), 0)
    tc = lax.broadcasted_iota(jnp.int32, (tile, tile), 1)
    tri = ((tr // sub == tc // sub) & (tc <= tr)).astype(BF16)
    b_s[...] = _dot_exact_lhs(tri, log_f)
    ones = _head_ones(w, HGRN_HEAD)
    sub_row = lax.broadcasted_iota(jnp.int32, (sub, w), 0)
    nsc = ns * nsub
    blk = lambda ref, c: ref[c * sub:(c + 1) * sub, :]
    row = lambda ref, r: ref[r:r + 1, :]
    heads = [slice(h * HGRN_HEAD, (h + 1) * HGRN_HEAD) for h in range(HGRN_HEADS)]
    pair = [blk(q_s, c) * jnp.exp(jnp.minimum(blk(b_s, c) - row(b_s, c * sub + s), 0.0)) * row(k_s, c * sub + s)
            for c in range(nsc) for s in range(sub)]
    att = _bdot(jnp.concatenate(pair, axis=0), ones)
    intra = []
    for c in range(nsc):
        o = jnp.zeros((sub, w), F32)
        for s in range(sub):
            r = c * sub + s
            o = o + jnp.where(sub_row >= s, att[r * sub:(r + 1) * sub, :], 0.0) * row(v_s, r)
        intra.append(o)
    last = [row(b_s, (c + 1) * sub - 1) for c in range(nsc)]
    kv = [[_bdot(blk(v_s, c)[:, hs], (blk(k_s, c) * jnp.exp(last[c] - blk(b_s, c)))[:, hs], TN) for hs in heads]
          for c in range(nsc)]
    states = []
    for seq in range(ns):
        st = [s_ref[seq, h] for h in range(HGRN_HEADS)]
        for c in range(seq * nsub, (seq + 1) * nsub):
            states.append(st)
            gamma = jnp.exp(last[c])
            st = [st[h] * gamma[:, heads[h]] + kv[c][h] for h in range(HGRN_HEADS)]
        for h in range(HGRN_HEADS):
            s_ref[seq, h] = st[h]
    for c in range(nsc):
        q_in = blk(q_s, c) * jnp.exp(blk(b_s, c))
        o = intra[c] + jnp.concatenate([_bdot(q_in[:, heads[h]], states[c][h], NT) for h in range(HGRN_HEADS)], axis=1)
        intra[c] = o
    o = jnp.concatenate(intra, axis=0)
    ms = _dot_exact_rhs(o * o, ones) * (1.0 / HGRN_HEAD)
    o_ref[...] = o * lax.rsqrt(ms + NORM_EPS) * ng_ref[...]

    @pl.when(j == pl.num_programs(1) - 1)
    def _():
        s_out_ref[...] = s_ref[...]


def _hgrn_scan(p_hgrn, lb, norm_g, s0, seq_len, tile=256, ns_max=16):
    n = p_hgrn.shape[0]
    batch = n // seq_len
    sub = min(HGRN_SUB, seq_len)
    if seq_len >= tile:
        ns, nsub = 1, tile // sub
    else:
        ns, nsub = min(ns_max, batch), seq_len // sub
    rows = ns * nsub * sub
    nt = seq_len // (nsub * sub)
    assert batch % ns == 0 and seq_len % (nsub * sub) == 0
    st = (ns, HGRN_HEADS, HGRN_HEAD, HGRN_HEAD)
    return pl.pallas_call(
        functools.partial(_hgrn_scan_kernel, ns=ns, nsub=nsub, sub=sub),
        out_shape=[jax.ShapeDtypeStruct((n, HGRN_WIDTH), F32), jax.ShapeDtypeStruct(s0.shape, F32)],
        grid=(batch // ns, nt),
        in_specs=[
            pl.BlockSpec((rows, HGRN_IN), lambda b, j: (b * nt + j, 0)),
            pl.BlockSpec((1, HGRN_WIDTH), lambda b, j: (0, 0)),
            pl.BlockSpec((1, HGRN_WIDTH), lambda b, j: (0, 0)),
            pl.BlockSpec(st, lambda b, j: (b, 0, 0, 0)),
        ],
        out_specs=[
            pl.BlockSpec((rows, HGRN_WIDTH), lambda b, j: (b * nt + j, 0)),
            pl.BlockSpec(st, lambda b, j: (b, 0, 0, 0)),
        ],
        scratch_shapes=[pltpu.VMEM(st, F32)] + [pltpu.VMEM((rows, HGRN_WIDTH), F32)] * 4,
        compiler_params=_cparams(("arbitrary", "arbitrary")),
        name="hgrn_scan",
    )(p_hgrn, lb, norm_g, s0)


def _layer_weights(P, l):
    w_in = P['w_in'][l]
    c_q, c_kv = MLA_Q_RANK, MLA_Q_RANK + MLA_KV_RANK
    c_pe = c_kv + MLA_ROPE
    kpe_tile = jnp.zeros((D_MODEL, LANE), F32).at[:, PE_LO:PE_LO + MLA_ROPE].set(w_in[:, c_kv:c_pe])
    w_in_pad = jnp.concatenate([w_in[:, :c_kv], kpe_tile, w_in[:, c_pe:]], axis=1).astype(BF16)

    hd = MLA_NOPE + MLA_ROPE
    wq = P['mla_w_q_b'][l].reshape(MLA_Q_RANK, MLA_HEADS, hd)
    wq = jnp.pad(wq, ((0, 0), (0, 0), (0, QK_PAD - hd))).reshape(MLA_Q_RANK, MLA_HEADS * QK_PAD).astype(BF16)
    wkv = P['mla_w_kv_b'][l].reshape(MLA_KV_RANK, MLA_HEADS, MLA_NOPE + MLA_V)
    w_uk, w_uv = wkv[..., :MLA_NOPE], wkv[..., MLA_NOPE:]
    wk = jnp.pad(w_uk, ((0, 0), (0, 0), (0, QK_PAD - MLA_NOPE))).reshape(MLA_KV_RANK, MLA_HEADS * QK_PAD).astype(BF16)
    wv = jnp.zeros((MLA_KV_RANK, MLA_HEADS, QK_PAD), F32)
    for h in range(MLA_HEADS):
        off = (h % 2) * MLA_V
        wv = wv.at[:, h, off:off + MLA_V].set(w_uv[:, h, :])
    wv = wv.reshape(MLA_KV_RANK, MLA_HEADS * QK_PAD).astype(BF16)
    ext = MLA_KV_RANK + LANE
    wuk_ext = jnp.zeros((MLA_HEADS, QK_PAD, ext), F32)
    wuk_ext = wuk_ext.at[:, :MLA_NOPE, :MLA_KV_RANK].set(jnp.transpose(w_uk, (1, 2, 0)))
    wuk_ext = wuk_ext.at[:, PE_LO:PE_LO + MLA_ROPE, MLA_KV_RANK:MLA_KV_RANK + MLA_ROPE].set(
        jnp.broadcast_to(jnp.eye(MLA_ROPE, dtype=F32), (MLA_HEADS, MLA_ROPE, MLA_ROPE)))
    wuk_ext = wuk_ext.astype(BF16)
    w_uv_h = jnp.transpose(w_uv, (1, 0, 2)).astype(BF16)

    w = RWKV_WIDTH
    w2 = jnp.zeros((DECAY_RANK + ICLR_RANK, 2 * w), F32)
    w2 = w2.at[:DECAY_RANK, :w].set(P['rwkv_w_decay_b'][l]).at[DECAY_RANK:, w:].set(P['rwkv_w_iclr_b'][l])
    wa0 = jnp.concatenate([P['rwkv_w0'][l], P['rwkv_a0'][l]])[None, :]
    row = lambda a: a.reshape(1, -1)
    rwkv_consts = (row(P['rwkv_mu'][l]), w2, wa0, row(P['rwkv_k_k'][l]), row(P['rwkv_k_a'][l]), row(P['rwkv_r_k'][l]))
    vres = None
    if l > 0:
        wva = jnp.pad(P['rwkv_w_vres_a'][l - 1], ((0, 0), (0, LANE - VRES_RANK)))
        wvb = jnp.pad(P['rwkv_w_vres_b'][l - 1], ((0, LANE - VRES_RANK), (0, 0)))
        vres = (row(P['rwkv_v0'][l - 1]), wva, wvb)
    return dict(
        w_in=w_in_pad, pre_g=row(P['pre_norm_g'][l]), post_g=row(P['post_norm_g'][l]),
        w_out=P['w_out'][l].astype(BF16),
        q_g=row(P['mla_q_norm_g'][l]), kv_g=row(P['mla_kv_norm_g'][l]),
        wq=wq, wk=wk, wv=wv, wuk_ext=wuk_ext, w_uv=w_uv_h,
        rwkv_consts=rwkv_consts, vres=vres,
        gn_g=row(P['rwkv_gn_g'][l]), gn_b=row(P['rwkv_gn_b'][l]),
        hgrn_g=jnp.tile(P['hgrn_norm_g'][l], HGRN_HEADS)[None, :],
    )


def _rope_tables(pos, rows):
    half = MLA_ROPE // 2
    inv_freq = ROPE_BASE ** (-jnp.arange(half, dtype=F32) / half)
    ang = pos.astype(F32)[:, None] * inv_freq[None, :]
    cos, sin = jnp.cos(ang), jnp.sin(ang)
    t = pos.shape[0]
    ones, zeros = jnp.ones((t, PE_LO), F32), jnp.zeros((t, PE_LO), F32)
    pad = jnp.zeros((t, LANE - PE_LO - MLA_ROPE), F32)
    cos_t = jnp.concatenate([ones, cos, cos, pad], axis=1)
    sin_t = jnp.concatenate([zeros, -sin, sin, pad], axis=1)
    rep = rows // t
    return jnp.tile(cos_t, (rep, 1)), jnp.tile(sin_t, (rep, 1))


def _trunk(x, mods, pos, shift0, rwkv0, hgrn0, cache, LW, lb_all):
    batch, seq_len, _ = x.shape
    n = batch * seq_len
    x = x.reshape(n, D_MODEL)
    prep_tile = min(512, n)
    cos, sin = _rope_tables(pos, max(seq_len, prep_tile))
    rw_tile, rw_ns, rw_seg = _seg_tiles(n, seq_len, 256)
    v_first = None
    ckvs, kpes, rs, shs, hs = [], [], [], [], []
    for l in range(DEPTH):
        W = LW[l]
        mod = mods[l][:, None, :]
        p_mla, p_rwkv, p_hgrn, gate = _in_proj(x, mod, W['pre_g'], W['w_in'], seq_len)
        if cache is None:
            q, k, v, ckv, kpe = _mla_prep(p_mla, cos, sin, W['q_g'], W['kv_g'], W['wq'], (W['wk'], W['wv']),
                                          seq_len, absorbed=False)
            o_mla = _attention_prompt(q, k, v, batch, seq_len)
            kpe = kpe[:, PE_LO:PE_LO + MLA_ROPE]
        else:
            cache_ckv, cache_kpe, page_table = cache
            qx, ckv, kpe = _mla_prep(p_mla, cos, sin, W['q_g'], W['kv_g'], W['wq'], (W['wuk_ext'],),
                                     seq_len, absorbed=True)
            kpe = kpe[:, PE_LO:PE_LO + MLA_ROPE]
            ext = qx.shape[1] // MLA_HEADS
            qx = qx.reshape(batch, seq_len, MLA_HEADS, ext).transpose(0, 2, 1, 3).reshape(batch, MLA_HEADS * seq_len, ext)
            o = _attention_paged(qx, ckv.reshape(batch, seq_len, MLA_KV_RANK), kpe.reshape(batch, seq_len, MLA_ROPE),
                                 W['w_uv'], cache_ckv, cache_kpe, page_table, l)
            o_mla = o.reshape(batch, MLA_HEADS, seq_len, MLA_V).transpose(0, 2, 1, 3).reshape(n, MLA_WIDTH)
        p3 = p_rwkv.reshape(batch, seq_len, RWKV_IN)
        tails = p3[:, rw_seg - 1::rw_seg, :]
        halo = jnp.concatenate([shift0[l][:, None, :], tails[:, :-1, :]], axis=1).reshape(n // rw_tile, rw_ns, 1, RWKV_IN)
        if l == 0:
            feat, v_first = _rwkv_prep(p_rwkv, halo, W['rwkv_consts'], None, None, seq_len)
        else:
            feat = _rwkv_prep(p_rwkv, halo, W['rwkv_consts'], v_first, W['vres'], seq_len)
        o_rwkv, s_r = _rwkv_scan(feat, rwkv0[l], W['gn_g'], W['gn_b'], seq_len)
        o_hgrn, s_h = _hgrn_scan(p_hgrn, lb_all[l][None, :], W['hgrn_g'], jnp.swapaxes(hgrn0[l], -1, -2), seq_len)
        s_h = jnp.swapaxes(s_h, -1, -2)
        x = _out_proj(o_mla, o_rwkv, o_hgrn, gate, x, mod, W['post_g'], W['w_out'], seq_len)
        ckvs.append(ckv.reshape(batch, seq_len, MLA_KV_RANK))
        kpes.append(kpe.reshape(batch, seq_len, MLA_ROPE))
        rs.append(s_r)
        shs.append(p3[:, -1, :])
        hs.append(s_h)
    return (x.reshape(batch, seq_len, D_MODEL), jnp.stack(ckvs), jnp.stack(kpes), jnp.stack(rs), jnp.stack(shs),
            jnp.stack(hs))


def kernel(x_prompt, x_sample, c_prompt, c_sample, cache_ckv, cache_kpe, page_table, state_rwkv, state_rwkv_shift, state_hgrn, w_ada, b_ada, pre_norm_g, post_norm_g, w_in, mla_q_norm_g, mla_w_q_b, mla_kv_norm_g, mla_w_kv_b, rwkv_mu, rwkv_w0, rwkv_w_decay_b, rwkv_a0, rwkv_w_iclr_b, rwkv_k_k, rwkv_k_a, rwkv_r_k, rwkv_gn_g, rwkv_gn_b, rwkv_v0, rwkv_w_vres_a, rwkv_w_vres_b, hgrn_lb_raw, hgrn_norm_g, w_out):
    P = {
        'w_in': w_in, 'pre_norm_g': pre_norm_g, 'post_norm_g': post_norm_g,
        'mla_q_norm_g': mla_q_norm_g, 'mla_w_q_b': mla_w_q_b, 'mla_kv_norm_g': mla_kv_norm_g, 'mla_w_kv_b': mla_w_kv_b,
        'rwkv_mu': rwkv_mu, 'rwkv_w0': rwkv_w0, 'rwkv_w_decay_b': rwkv_w_decay_b, 'rwkv_a0': rwkv_a0,
        'rwkv_w_iclr_b': rwkv_w_iclr_b, 'rwkv_k_k': rwkv_k_k, 'rwkv_k_a': rwkv_k_a, 'rwkv_r_k': rwkv_r_k,
        'rwkv_gn_g': rwkv_gn_g, 'rwkv_gn_b': rwkv_gn_b, 'rwkv_v0': rwkv_v0, 'rwkv_w_vres_a': rwkv_w_vres_a,
        'rwkv_w_vres_b': rwkv_w_vres_b, 'hgrn_norm_g': hgrn_norm_g, 'w_out': w_out,
    }
    LW = [_layer_weights(P, l) for l in range(DEPTH)]
    lb_p = jax.nn.softmax(hgrn_lb_raw.astype(F32), axis=0)
    lb_all = jnp.cumsum(lb_p, axis=0) - lb_p[0]
    bp, tp = x_prompt.shape[:2]
    bs, ts = x_sample.shape[:2]
    mods = _ada(jnp.concatenate([c_prompt, c_sample], axis=0), w_ada, b_ada)
    past_len = page_table.shape[1] * PAGE_SIZE
    zeros = lambda *s: jnp.zeros(s, F32)
    out_p = _trunk(x_prompt, mods[:, :bp], jnp.arange(tp), zeros(DEPTH, bp, RWKV_IN),
                   zeros(DEPTH, bp, RWKV_HEADS, RWKV_HEAD, RWKV_HEAD), zeros(DEPTH, bp, HGRN_HEADS, HGRN_HEAD, HGRN_HEAD),
                   None, LW, lb_all)
    out_s = _trunk(x_sample, mods[:, bp:], past_len + jnp.arange(ts), state_rwkv_shift, state_rwkv, state_hgrn,
                   (cache_ckv, cache_kpe, page_table), LW, lb_all)
    return (out_p[0], out_s[0]) + out_p[1:] + out_s[1:]
```

```python
import functools
import math

import jax
import jax.numpy as jnp
from jax import lax
from jax.experimental import pallas as pl
from jax.experimental.pallas import tpu as pltpu

F32 = jnp.float32
BF16 = jnp.bfloat16

D_MODEL = 1024
DEPTH = 2
NORM_EPS = 1e-6
MASK_VALUE = -1e30
MIN_FORGET = 1e-20
PAGE_SIZE = 128
MLA_HEADS = 8
MLA_NOPE = 64
MLA_ROPE = 32
MLA_V = 64
MLA_Q_RANK = 384
MLA_KV_RANK = 256
MLA_WIDTH = MLA_HEADS * MLA_V
MLA_SCALE = (MLA_NOPE + MLA_ROPE) ** -0.5
ROPE_BASE = 10000.0
RWKV_WIDTH = 256
RWKV_HEAD = 64
RWKV_HEADS = 4
DECAY_RANK = 64
ICLR_RANK = 64
VRES_RANK = 32
RWKV_GN_EPS = 64e-5
RWKV_IN = 3 * RWKV_WIDTH + DECAY_RANK + ICLR_RANK
RWKV_CHUNK = 64
HGRN_WIDTH = 256
HGRN_HEADS = 4
HGRN_HEAD = 64
HGRN_IN = 3 * HGRN_WIDTH
HGRN_SUB = 16
LANE = 128
MLA_IN_PAD = MLA_Q_RANK + MLA_KV_RANK + LANE
QK_PAD = LANE
PE_LO = MLA_NOPE
IN_PAD = MLA_IN_PAD + RWKV_IN + HGRN_IN + D_MODEL
RWKV_FEAT = 7 * RWKV_WIDTH
VMEM_LIMIT = 56 * 1024 * 1024

NN = (((1,), (0,)), ((), ()))
NT = (((1,), (1,)), ((), ()))
TN = (((0,), (0,)), ((), ()))


def _bdot(a, b, dims=NN):
    return lax.dot_general(a.astype(BF16), b.astype(BF16), dims, preferred_element_type=F32)


def _split(a):
    hi = a.astype(BF16)
    lo = (a - hi.astype(F32)).astype(BF16)
    return hi, lo


def _dot3(a, b, dims=NN):
    ah, al = _split(a)
    bh, bl = _split(b)
    d = lambda x, y: lax.dot_general(x, y, dims, preferred_element_type=F32)
    return d(ah, bh) + (d(ah, bl) + d(al, bh))


def _dot_exact_rhs(a, b, dims=NN):
    a1 = a.astype(BF16)
    r1 = a - a1.astype(F32)
    a2 = r1.astype(BF16)
    a3 = (r1 - a2.astype(F32)).astype(BF16)
    bb = b.astype(BF16)
    d = lambda x: lax.dot_general(x, bb, dims, preferred_element_type=F32)
    return d(a1) + (d(a2) + d(a3))


def _dot_exact_lhs(m, x):
    x1 = x.astype(BF16)
    r1 = x - x1.astype(F32)
    x2 = r1.astype(BF16)
    x3 = (r1 - x2.astype(F32)).astype(BF16)
    d = lambda y: lax.dot_general(m, y, NN, preferred_element_type=F32)
    return d(x1) + (d(x2) + d(x3))


def _silu(x):
    return x * jax.nn.sigmoid(x)


def _rms(x, g):
    return x * lax.rsqrt(jnp.mean(x * x, axis=-1, keepdims=True) + NORM_EPS) * g


def _rows(m, ns, seg):
    w = m.shape[-1]
    return jnp.broadcast_to(m, (ns, seg, w)).reshape(ns * seg, w)


def _head_ones(width, head):
    r = lax.broadcasted_iota(jnp.int32, (width, width), 0) // head
    c = lax.broadcasted_iota(jnp.int32, (width, width), 1) // head
    return (r == c).astype(BF16)


def _cparams(sem):
    return pltpu.CompilerParams(dimension_semantics=sem, vmem_limit_bytes=VMEM_LIMIT)


def _seg_tiles(n_rows, seq_len, tile):
    tile = min(tile, n_rows)
    seg = min(seq_len, tile)
    ns = tile // seg
    assert ns * seg == tile and n_rows % tile == 0 and seq_len % seg == 0
    return tile, ns, seg


def _ada_kernel(c_ref, w_ref, b_ref, o_ref):
    o_ref[0] = _dot3(_silu(c_ref[...]), w_ref[0]) + b_ref[0]


def _ada(c_all, w_ada, b_ada):
    bt = c_all.shape[0]
    tn = 768
    return pl.pallas_call(
        _ada_kernel,
        out_shape=jax.ShapeDtypeStruct((DEPTH, bt, 3 * D_MODEL), F32),
        grid=(DEPTH, 3 * D_MODEL // tn),
        in_specs=[
            pl.BlockSpec((bt, D_MODEL), lambda l, j: (0, 0)),
            pl.BlockSpec((1, D_MODEL, tn), lambda l, j: (l, 0, j)),
            pl.BlockSpec((1, 1, tn), lambda l, j: (l, 0, j)),
        ],
        out_specs=pl.BlockSpec((1, bt, tn), lambda l, j: (l, 0, j)),
        compiler_params=_cparams(("arbitrary", "arbitrary")),
        name="ada",
    )(c_all, w_ada, b_ada.reshape(DEPTH, 1, 3 * D_MODEL))


def _in_proj_kernel(x_ref, mod_ref, g_ref, w_ref, o_mla, o_rwkv, o_hgrn, o_gate, *, ns, seg):
    mod = mod_ref[...]
    shift = _rows(mod[:, :, 0:D_MODEL], ns, seg)
    scale = _rows(mod[:, :, D_MODEL:2 * D_MODEL], ns, seg)
    h = (_rms(x_ref[...], g_ref[...]) * (1.0 + scale) + shift).astype(BF16)
    c0 = 0
    for o_ref in (o_mla, o_rwkv, o_hgrn, o_gate):
        c1 = c0 + o_ref.shape[1]
        o_ref[...] = lax.dot_general(h, w_ref[:, c0:c1], NN, preferred_element_type=F32)
        c0 = c1


def _in_proj(x, mod, g, w, seq_len, tile=512):
    n = x.shape[0]
    tm, ns, seg = _seg_tiles(n, seq_len, tile)
    per_seq = seq_len // seg
    widths = (MLA_IN_PAD, RWKV_IN, HGRN_IN, D_MODEL)
    return pl.pallas_call(
        functools.partial(_in_proj_kernel, ns=ns, seg=seg),
        out_shape=[jax.ShapeDtypeStruct((n, wd), F32) for wd in widths],
        grid=(n // tm,),
        in_specs=[
            pl.BlockSpec((tm, D_MODEL), lambda i: (i, 0)),
            pl.BlockSpec((ns, 1, 3 * D_MODEL), lambda i: (i // per_seq, 0, 0)),
            pl.BlockSpec((1, D_MODEL), lambda i: (0, 0)),
            pl.BlockSpec((D_MODEL, IN_PAD), lambda i: (0, 0)),
        ],
        out_specs=[pl.BlockSpec((tm, wd), lambda i: (i, 0)) for wd in widths],
        compiler_params=_cparams(("arbitrary",)),
        name="in_proj",
    )(x, mod, g, w)


def _out_proj_kernel(om_ref, or_ref, oh_ref, gate_ref, x_ref, mod_ref, g_ref, w_ref, o_ref, *, ns, seg):
    sg = _silu(gate_ref[...])
    a0, a1 = MLA_WIDTH, MLA_WIDTH + RWKV_WIDTH
    acc = _bdot(om_ref[...] * sg[:, 0:a0], w_ref[0:a0, :])
    acc += _bdot(or_ref[...] * sg[:, a0:a1], w_ref[a0:a1, :])
    acc += _bdot(oh_ref[...] * sg[:, a1:], w_ref[a1:, :])
    a_gate = _rows(mod_ref[...][:, :, 2 * D_MODEL:], ns, seg)
    o_ref[...] = x_ref[...] + a_gate * _rms(acc, g_ref[...])


def _out_proj(o_mla, o_rwkv, o_hgrn, gate, x, mod, g, w, seq_len, tile=512):
    n = x.shape[0]
    tm, ns, seg = _seg_tiles(n, seq_len, tile)
    per_seq = seq_len // seg
    row = lambda wd: pl.BlockSpec((tm, wd), lambda i: (i, 0))
    return pl.pallas_call(
        functools.partial(_out_proj_kernel, ns=ns, seg=seg),
        out_shape=jax.ShapeDtypeStruct((n, D_MODEL), F32),
        grid=(n // tm,),
        in_specs=[
            row(MLA_WIDTH), row(RWKV_WIDTH), row(HGRN_WIDTH), row(D_MODEL), row(D_MODEL),
            pl.BlockSpec((ns, 1, 3 * D_MODEL), lambda i: (i // per_seq, 0, 0)),
            pl.BlockSpec((1, D_MODEL), lambda i: (0, 0)),
            pl.BlockSpec((D_MODEL, D_MODEL), lambda i: (0, 0)),
        ],
        out_specs=row(D_MODEL),
        compiler_params=_cparams(("arbitrary",)),
        name="out_proj",
    )(o_mla, o_rwkv, o_hgrn, gate, x, mod, g, w)


def _rope_tile(x, cos, sin):
    lane = lax.broadcasted_iota(jnp.int32, x.shape, 1)
    half = MLA_ROPE // 2
    rot = jnp.where(lane < PE_LO + half, pltpu.roll(x, LANE - half, axis=1), pltpu.roll(x, half, axis=1))
    return x * cos + rot * sin


def _mla_prep_kernel(p_ref, cos_ref, sin_ref, qg_ref, kvg_ref, wq_ref, *rest, absorbed):
    p = p_ref[...]
    cos, sin = cos_ref[...], sin_ref[...]
    cq = _rms(p[:, 0:MLA_Q_RANK], qg_ref[...])
    ckv = _rms(p[:, MLA_Q_RANK:MLA_Q_RANK + MLA_KV_RANK], kvg_ref[...])
    kpe = _rope_tile(p[:, MLA_Q_RANK + MLA_KV_RANK:], cos, sin)
    q = _bdot(cq, wq_ref[...])
    if absorbed:
        wuk_ref, qx_out, ckv_out, kpe_out = rest
        ext = wuk_ref.shape[2]
        for h in range(MLA_HEADS):
            qh = _rope_tile(q[:, h * QK_PAD:(h + 1) * QK_PAD], cos, sin)
            qx_out[:, h * ext:(h + 1) * ext] = _bdot(qh, wuk_ref[h]).astype(BF16)
    else:
        wk_ref, wv_ref, q_out, k_out, v_out, ckv_out, kpe_out = rest
        kn = _bdot(ckv, wk_ref[...])
        for h in range(MLA_HEADS):
            sl = slice(h * QK_PAD, (h + 1) * QK_PAD)
            q_out[:, sl] = _rope_tile(q[:, sl], cos, sin).astype(BF16)
            k_out[:, sl] = (kn[:, sl] + kpe).astype(BF16)
        v = _bdot(ckv, wv_ref[...])
        col = lax.broadcasted_iota(jnp.int32, v.shape, 1)
        ones_col = jnp.where((col // QK_PAD) % 2 == 0, MLA_V, 0)
        v_out[...] = jnp.where(col % QK_PAD == ones_col, 1.0, v).astype(BF16)
    ckv_out[...] = ckv
    kpe_out[...] = kpe[:, PE_LO:PE_LO + MLA_ROPE]


def _mla_prep(p_mla, cos, sin, qg, kvg, wq, extra_w, seq_len, absorbed, tile=512):
    n = p_mla.shape[0]
    tm = min(tile, n)
    rep = max(seq_len // tm, 1)
    hw = MLA_HEADS * QK_PAD
    row = lambda wd: pl.BlockSpec((tm, wd), lambda i: (i, 0))
    full = lambda a: pl.BlockSpec(a.shape, lambda i: (0,) * a.ndim)
    tab = pl.BlockSpec((tm, LANE), lambda i: (i % rep, 0))
    if absorbed:
        ext = extra_w[0].shape[2]
        outs = [(MLA_HEADS * ext, BF16), (MLA_KV_RANK, F32), (MLA_ROPE, F32)]
    else:
        outs = [(hw, BF16), (hw, BF16), (hw, BF16), (MLA_KV_RANK, F32), (MLA_ROPE, F32)]
    return pl.pallas_call(
        functools.partial(_mla_prep_kernel, absorbed=absorbed),
        out_shape=[jax.ShapeDtypeStruct((n, wd), dt) for wd, dt in outs],
        grid=(n // tm,),
        in_specs=[row(MLA_IN_PAD), tab, tab, full(qg), full(kvg), full(wq)] + [full(a) for a in extra_w],
        out_specs=[row(wd) for wd, _ in outs],
        compiler_params=_cparams(("arbitrary",)),
        name="mla_prep_absorbed" if absorbed else "mla_prep",
    )(p_mla, cos, sin, qg, kvg, wq, *extra_w)


def _softmax_step(s, v, m, l, acc):
    c = MLA_SCALE * math.log2(math.e)
    m_new = jnp.maximum(m, jnp.max(s, axis=-1, keepdims=True))
    alpha = jnp.exp2((m - m_new) * c)
    p = jnp.exp2((s - m_new) * c)
    l = alpha * l + jnp.sum(p, axis=-1, keepdims=True)
    acc = alpha * acc + lax.dot_general(p.astype(BF16), v, NN, preferred_element_type=F32)
    return m_new, l, acc


def _softmax_steps(ss, vs, ms, accs):
    c = MLA_SCALE * math.log2(math.e)
    m_new = [jnp.maximum(m, jnp.max(s, axis=-1, keepdims=True)) for m, s in zip(ms, ss)]
    alpha = [jnp.exp2((m - mn) * c) for m, mn in zip(ms, m_new)]
    ps = [jnp.exp2((s - mn) * c).astype(BF16) for s, mn in zip(ss, m_new)]
    pv = [lax.dot_general(p, v, NN, preferred_element_type=F32) for p, v in zip(ps, vs)]
    accs = [a * acc + x for a, acc, x in zip(alpha, accs, pv)]
    return m_new, accs


def _attn_kernel(q_ref, k_ref, v_ref, o_ref, *, tq, heads):
    i = pl.program_id(2)
    sls = [slice(h * QK_PAD, (h + 1) * QK_PAD) for h in range(heads)]
    qs = [q_ref[:, sl] for sl in sls]

    def scores(j):
        rows = pl.ds(pl.multiple_of(j * tq, tq), tq)
        ss = [lax.dot_general(q, k_ref[rows, sl], NT, preferred_element_type=F32) for q, sl in zip(qs, sls)]
        return ss, [v_ref[rows, sl] for sl in sls]

    def body(j, carry):
        ss, vs = scores(j)
        return _softmax_steps(ss, vs, *carry)

    init = ([jnp.full((tq, 1), MASK_VALUE, F32)] * heads, [jnp.zeros((tq, LANE), F32)] * heads)
    carry = lax.fori_loop(0, i, body, init)
    ss, vs = scores(i)
    r = lax.broadcasted_iota(jnp.int32, (tq, tq), 0)
    c = lax.broadcasted_iota(jnp.int32, (tq, tq), 1)
    _, accs = _softmax_steps([jnp.where(c <= r, s, MASK_VALUE) for s in ss], vs, *carry)
    low = lax.broadcasted_iota(jnp.int32, (tq, LANE), 1) < MLA_V
    for hp in range(heads // 2):
        even, odd = accs[2 * hp], accs[2 * hp + 1]
        o_ref[:, hp * LANE:(hp + 1) * LANE] = jnp.where(low, even / even[:, MLA_V:MLA_V + 1], odd / odd[:, 0:1])


def _attention_prompt(q, k, v, batch, seq_len, tq=512, heads=4):
    n = q.shape[0]
    tq = min(tq, seq_len)
    nq = seq_len // tq
    wide = heads * QK_PAD
    return pl.pallas_call(
        functools.partial(_attn_kernel, tq=tq, heads=heads),
        out_shape=jax.ShapeDtypeStruct((n, MLA_WIDTH), F32),
        grid=(batch, MLA_HEADS // heads, nq),
        in_specs=[
            pl.BlockSpec((tq, wide), lambda b, h, i: (b * nq + i, h)),
            pl.BlockSpec((seq_len, wide), lambda b, h, i: (b, h)),
            pl.BlockSpec((seq_len, wide), lambda b, h, i: (b, h)),
        ],
        out_specs=pl.BlockSpec((tq, wide // 2), lambda b, h, i: (b * nq + i, h)),
        compiler_params=_cparams(("arbitrary", "arbitrary", "arbitrary")),
        name="attn_prompt",
    )(q, k, v)


def _attn_paged_kernel(pt_ref, q_ref, ckv_new_ref, kpe_new_ref, wuv_ref, ckv_hbm, kpe_hbm, o_ref,
                       ckv_buf, kpe_buf, sem, *, pages, group, t_new, layer):
    b = pl.program_id(0)
    slot = b % 2
    rows = MLA_HEADS * t_new

    def page_copies(elem, slot_idx, lookup):
        out = []
        for pg in range(pages):
            page = pt_ref[elem * pages + pg] if lookup else 0
            out.append(pltpu.make_async_copy(ckv_hbm.at[layer, page], ckv_buf.at[slot_idx, pg], sem.at[slot_idx, 0]))
            out.append(pltpu.make_async_copy(kpe_hbm.at[layer, page], kpe_buf.at[slot_idx, pg], sem.at[slot_idx, 1]))
        return out

    @pl.when(b == 0)
    def _():
        for cp in page_copies(0, 0, True):
            cp.start()

    @pl.when(b + 1 < pl.num_programs(0))
    def _():
        for cp in page_copies(b + 1, 1 - slot, True):
            cp.start()

    for cp in page_copies(b, slot, False):
        cp.wait()

    q = q_ref[0]
    q_lat = q[:, 0:MLA_KV_RANK]
    q_pe = q[:, MLA_KV_RANK:MLA_KV_RANK + MLA_ROPE]
    def scores(n):
        pg = n * group
        ckv = jnp.concatenate([ckv_buf[slot, pg + i].astype(BF16) for i in range(group)], axis=0)
        kpe_t = jnp.concatenate([kpe_buf[slot, pg + i].astype(BF16) for i in range(group)], axis=1)
        return (lax.dot_general(q_lat, ckv, NT, preferred_element_type=F32)
                + lax.dot_general(q_pe, kpe_t, NN, preferred_element_type=F32)), ckv

    ckv = ckv_new_ref[0].astype(BF16)
    s = lax.dot_general(q_lat, ckv, NT, preferred_element_type=F32)
    s += lax.dot_general(q_pe, kpe_new_ref[0].astype(BF16), NT, preferred_element_type=F32)
    t_q = lax.broadcasted_iota(jnp.int32, (rows, t_new), 0) % t_new
    t_k = lax.broadcasted_iota(jnp.int32, (rows, t_new), 1)
    ahead = scores(0)
    m, l, acc = _softmax_step(jnp.where(t_k <= t_q, s, MASK_VALUE), ckv, jnp.full((rows, 1), MASK_VALUE, F32),
                              jnp.zeros((rows, 1), F32), jnp.zeros((rows, MLA_KV_RANK), F32))
    n_chunks = pages // group
    for n in range(n_chunks):
        s, ckv = ahead
        if n + 1 < n_chunks:
            ahead = scores(n + 1)
        m, l, acc = _softmax_step(s, ckv, m, l, acc)
    lat = acc / l
    for h in range(MLA_HEADS):
        o_ref[0, h * t_new:(h + 1) * t_new, :] = _bdot(lat[h * t_new:(h + 1) * t_new, :], wuv_ref[h])


def _attention_paged(q_ext, ckv_new, kpe_new, w_uv, cache_ckv, cache_kpe, page_table, layer, group=16):
    batch, rows, ext = q_ext.shape
    t_new = rows // MLA_HEADS
    pages = page_table.shape[1]
    group = min(group, pages)
    assert pages % group == 0
    kpe_t = jnp.swapaxes(cache_kpe, 2, 3)
    grid_spec = pltpu.PrefetchScalarGridSpec(
        num_scalar_prefetch=1,
        grid=(batch,),
        in_specs=[
            pl.BlockSpec((1, rows, ext), lambda b, pt: (b, 0, 0)),
            pl.BlockSpec((1, t_new, MLA_KV_RANK), lambda b, pt: (b, 0, 0)),
            pl.BlockSpec((1, t_new, MLA_ROPE), lambda b, pt: (b, 0, 0)),
            pl.BlockSpec(w_uv.shape, lambda b, pt: (0, 0, 0)),
            pl.BlockSpec(memory_space=pl.ANY),
            pl.BlockSpec(memory_space=pl.ANY),
        ],
        out_specs=pl.BlockSpec((1, rows, MLA_V), lambda b, pt: (b, 0, 0)),
        scratch_shapes=[
            pltpu.VMEM((2, pages, PAGE_SIZE, MLA_KV_RANK), F32),
            pltpu.VMEM((2, pages, MLA_ROPE, PAGE_SIZE), F32),
            pltpu.SemaphoreType.DMA((2, 2)),
        ],
    )
    return pl.pallas_call(
        functools.partial(_attn_paged_kernel, pages=pages, group=group, t_new=t_new, layer=layer),
        out_shape=jax.ShapeDtypeStruct((batch, rows, MLA_V), F32),
        grid_spec=grid_spec,
        compiler_params=_cparams(("arbitrary",)),
        name="attn_paged",
    )(page_table.reshape(-1), q_ext, ckv_new, kpe_new, w_uv, cache_ckv, kpe_t)


def _rwkv_prep_kernel(p_ref, shift_ref, *rest, ns, seg, per_seq, first):
    if per_seq:
        tail_ref, *rest = rest
    mu_ref, w2_ref, wa0_ref, kk_ref, ka_ref, rk_ref, *rest = rest
    if first:
        feat_ref, vfirst_out = rest
    else:
        vfirst_ref, v0_ref, wva_ref, wvb_ref, feat_ref = rest
    w = RWKV_WIDTH
    p = p_ref[...]
    row = lax.broadcasted_iota(jnp.int32, p.shape, 0)
    before = shift_ref[...]
    if per_seq:
        before = jnp.where(pl.program_id(0) % per_seq == 0, before, tail_ref[7:8, :][None])
    prev = jnp.where(row % seg == 0, _rows(before, ns, seg), pltpu.roll(p, 1, axis=0))
    pm = p + mu_ref[...] * (prev - p)
    r, k, v, xwa = pm[:, 0:w], pm[:, w:2 * w], pm[:, 2 * w:3 * w], pm[:, 3 * w:]
    lane = lax.broadcasted_iota(jnp.int32, xwa.shape, 1)
    da = _dot3(jnp.where(lane < DECAY_RANK, jnp.tanh(xwa), xwa), w2_ref[...]) + wa0_ref[...]
    z = -da[:, 0:w]
    softplus = jnp.maximum(z, 0.0) + jnp.log1p(jnp.exp(-jnp.abs(z)))
    log_decay = -jnp.exp(-softplus - 0.5)
    a = jax.nn.sigmoid(da[:, w:])
    if first:
        vfirst_out[...] = v
    else:
        mix = jax.nn.sigmoid(v0_ref[...] + _dot3(_dot3(v, wva_ref[...]), wvb_ref[...]))
        v = v + (vfirst_ref[...] - v) * mix
    kk = k * kk_ref[...]
    k = k * (1.0 + (a - 1.0) * ka_ref[...])
    ones = _head_ones(w, RWKV_HEAD)
    kk = kk / jnp.maximum(jnp.sqrt(_dot_exact_rhs(kk * kk, ones)), 1e-12)
    bonus = _dot_exact_rhs(r * k * rk_ref[...], ones) * v
    for j, t in enumerate((r, log_decay, k, v, -kk, kk * a, bonus)):
        feat_ref[:, j * w:(j + 1) * w] = t


def _rwkv_prep(p_rwkv, shift_prev, consts, v_first, vres, seq_len, tile=256):
    n = p_rwkv.shape[0]
    tm, ns, seg = _seg_tiles(n, seq_len, tile)
    per_seq = seq_len // seg if ns == 1 else 0
    assert per_seq or seg == seq_len
    first = v_first is None
    row = lambda wd: pl.BlockSpec((tm, wd), lambda i: (i, 0))
    full = lambda a: pl.BlockSpec(a.shape, lambda i: (0,) * a.ndim)
    ins = [p_rwkv, shift_prev[:, None, :]]
    specs = [row(RWKV_IN), pl.BlockSpec((ns, 1, RWKV_IN), lambda i: (i // max(per_seq, 1), 0, 0))]
    if per_seq:
        sub_tiles = tm // 8
        ins.append(p_rwkv)
        specs.append(pl.BlockSpec((8, RWKV_IN), lambda i: (jnp.maximum(i * sub_tiles - 1, 0), 0)))
    ins += list(consts)
    specs += [full(a) for a in consts]
    feat = jax.ShapeDtypeStruct((n, RWKV_FEAT), F32)
    if first:
        out_shape = [feat, jax.ShapeDtypeStruct((n, RWKV_WIDTH), F32)]
        out_specs = [row(RWKV_FEAT), row(RWKV_WIDTH)]
    else:
        ins += [v_first] + list(vres)
        specs += [row(RWKV_WIDTH)] + [full(a) for a in vres]
        out_shape, out_specs = feat, row(RWKV_FEAT)
    return pl.pallas_call(
        functools.partial(_rwkv_prep_kernel, ns=ns, seg=seg, per_seq=per_seq, first=first),
        out_shape=out_shape,
        grid=(n // tm,),
        in_specs=specs,
        out_specs=out_specs,
        compiler_params=_cparams(("arbitrary",)),
        name="rwkv_prep",
    )(*ins)


def _rwkv_chunk_heads(r, lw_cum, lw, k, v, a, b, s0):
    n = len(r)
    c = r[0].shape[0]
    mm = _bdot
    each = lambda f, *xs: [f(*t) for t in zip(*xs)]
    ar = each(lambda a_, r_, cum, lw_: jnp.concatenate([a_ * jnp.exp(cum - lw_), r_ * jnp.exp(cum)], axis=0),
              a, r, lw_cum, lw)
    bk = each(lambda b_, k_, cum: jnp.concatenate([b_ * jnp.exp(-cum), k_ * jnp.exp(-cum)], axis=0),
              b, k, lw_cum)
    g = each(lambda x, y: mm(x, y, NT), ar, bk)
    gr = lax.broadcasted_iota(jnp.int32, (2 * c, 2 * c), 0)
    gc = lax.broadcasted_iota(jnp.int32, (2 * c, 2 * c), 1) % c
    keep = gc < gr - jnp.where(gr < c, 0, c - 1)
    g = [jnp.where(keep, x, 0.0) for x in g]
    a_ab = [x[0:c, 0:c] for x in g]
    eye = (lax.broadcasted_iota(jnp.int32, (c, c), 0) == lax.broadcasted_iota(jnp.int32, (c, c), 1)).astype(F32)
    t_inv = [eye + x for x in a_ab]
    x = each(mm, a_ab, a_ab)
    for _ in range(int(math.log2(c)) - 2):
        tx = each(lambda t_, x_: mm(jnp.concatenate([t_, x_], axis=0), x_), t_inv, x)
        t_inv = each(lambda t_, tx_: t_ + tx_[0:c, :], t_inv, tx)
        x = [tx_[c:, :] for tx_ in tx]
    t_inv = each(lambda t_, x_: t_ + mm(t_, x_), t_inv, x)
    ar_s0 = each(lambda x, y: mm(x, y, NT), ar, s0)
    g_v = each(lambda g_, v_: mm(g_[0:c, :], jnp.concatenate([jnp.zeros_like(v_), v_], axis=0)), g, v)
    u = each(lambda t_, x, y: mm(t_, x[0:c, :] + y), t_inv, ar_s0, g_v)
    uv = each(lambda u_, v_: jnp.concatenate([u_, v_], axis=0), u, v)
    o = each(lambda x, g_, uv_: x[c:, :] + mm(g_[c:, :], uv_), ar_s0, g, uv)
    bk_end = each(lambda b_, k_, cum: jnp.concatenate(
        [b_ * jnp.exp(cum[c - 1:c, :] - cum), k_ * jnp.exp(cum[c - 1:c, :] - cum)], axis=0), b, k, lw_cum)
    s1 = each(lambda s_, cum, uv_, e_: s_ * jnp.exp(cum[c - 1:c, :]) + mm(uv_, e_, TN), s0, lw_cum, uv, bk_end)
    assert len(o) == n
    return o, s1


def _rwkv_scan_kernel(feat_ref, s0_ref, gn_g_ref, gn_b_ref, o_ref, s_out_ref, s_ref, *, ns, chunk):
    j = pl.program_id(1)
    w = RWKV_WIDTH

    @pl.when(j == 0)
    def _():
        s_ref[...] = s0_ref[...]

    tri = (lax.broadcasted_iota(jnp.int32, (chunk, chunk), 1)
           <= lax.broadcasted_iota(jnp.int32, (chunk, chunk), 0)).astype(BF16)

    feats = [feat_ref[q] for q in range(ns)]
    cums = [_dot_exact_lhs(tri, f[:, w:2 * w]) for f in feats]
    inst = [(q, h) for q in range(ns) for h in range(RWKV_HEADS)]
    col = lambda i, h: slice(i * w + h * RWKV_HEAD, i * w + (h + 1) * RWKV_HEAD)
    part = lambda i: [feats[q][:, col(i, h)] for q, h in inst]
    o, s1 = _rwkv_chunk_heads(part(0), [cums[q][:, col(0, h)] for q, h in inst], part(1), part(2), part(3), part(4),
                              part(5), [s_ref[q, h] for q, h in inst])
    normed = []
    for (q, h), o_h, s_h in zip(inst, o, s1):
        s_ref[q, h] = s_h
        mean = jnp.mean(o_h, axis=-1, keepdims=True)
        var = jnp.mean(jnp.square(o_h - mean), axis=-1, keepdims=True)
        normed.append((o_h - mean) * lax.rsqrt(var + RWKV_GN_EPS))
    for q in range(ns):
        o_q = jnp.concatenate(normed[q * RWKV_HEADS:(q + 1) * RWKV_HEADS], axis=1)
        o_ref[q] = o_q * gn_g_ref[...] + gn_b_ref[...] + feats[q][:, 6 * w:]

    @pl.when(j == pl.num_programs(1) - 1)
    def _():
        s_out_ref[...] = s_ref[...]


def _rwkv_scan(feat, s0, gn_g, gn_b, seq_len, ns=4):
    n = feat.shape[0]
    batch = n // seq_len
    chunk = min(RWKV_CHUNK, seq_len)
    nc = seq_len // chunk
    ns = min(ns, batch)
    assert batch % ns == 0 and seq_len % chunk == 0
    st = (ns, RWKV_HEADS, RWKV_HEAD, RWKV_HEAD)
    o, s1 = pl.pallas_call(
        functools.partial(_rwkv_scan_kernel, ns=ns, chunk=chunk),
        out_shape=[jax.ShapeDtypeStruct((batch, seq_len, RWKV_WIDTH), F32), jax.ShapeDtypeStruct(s0.shape, F32)],
        grid=(batch // ns, nc),
        in_specs=[
            pl.BlockSpec((ns, chunk, RWKV_FEAT), lambda b, j: (b, j, 0)),
            pl.BlockSpec(st, lambda b, j: (b, 0, 0, 0)),
            pl.BlockSpec((1, RWKV_WIDTH), lambda b, j: (0, 0)),
            pl.BlockSpec((1, RWKV_WIDTH), lambda b, j: (0, 0)),
        ],
        out_specs=[
            pl.BlockSpec((ns, chunk, RWKV_WIDTH), lambda b, j: (b, j, 0)),
            pl.BlockSpec(st, lambda b, j: (b, 0, 0, 0)),
        ],
        scratch_shapes=[pltpu.VMEM(st, F32)],
        compiler_params=_cparams(("arbitrary", "arbitrary")),
        name="rwkv_scan",
    )(feat.reshape(batch, seq_len, RWKV_FEAT), s0, gn_g, gn_b)
    return o.reshape(n, RWKV_WIDTH), s1


def _hgrn_scan_kernel(p_ref, lb_ref, ng_ref, s0_ref, o_ref, s_out_ref, s_ref, q_s, k_s, v_s, b_s, *, ns, nsub, sub):
    j = pl.program_id(1)
    w = HGRN_WIDTH
    tile = ns * nsub * sub

    @pl.when(j == 0)
    def _():
        s_ref[...] = s0_ref[...]

    p = p_ref[...]
    lb = lb_ref[...]
    sig = jax.nn.sigmoid(p[:, w:2 * w])
    q_s[...] = _silu(p[:, 0:w]) * HGRN_HEAD ** -0.5
    k_s[...] = (1.0 - lb) * (1.0 - sig)
    v_s[...] = p[:, 2 * w:]
    log_f = jnp.log(jnp.maximum(lb + (1.0 - lb) * sig, MIN_FORGET))
    tr = lax.broadcasted_iota(jnp.int32, (tile, tile), 0)
    tc = lax.broadcasted_iota(jnp.int32, (tile, tile), 1)
    tri = ((tr // sub == tc // sub) & (tc <= tr)).astype(BF16)
    b_s[...] = _dot_exact_lhs(tri, log_f) * math.log2(math.e)
    ones = _head_ones(w, HGRN_HEAD)
    sub_row = lax.broadcasted_iota(jnp.int32, (sub, w), 0)
    nsc = ns * nsub
    blk = lambda ref, c: ref[c * sub:(c + 1) * sub, :]
    row = lambda ref, r: ref[r:r + 1, :]
    heads = [slice(h * HGRN_HEAD, (h + 1) * HGRN_HEAD) for h in range(HGRN_HEADS)]
    pair, spans = [], []
    for c in range(nsc):
        for s in range(sub):
            t0 = c * sub + (s // 8) * 8
            t1 = (c + 1) * sub
            dec = jnp.exp2(jnp.minimum(b_s[t0:t1, :] - row(b_s, c * sub + s), 0.0))
            pair.append(q_s[t0:t1, :] * dec * row(k_s, c * sub + s))
            spans.append(t1 - t0)
    att = _bdot(jnp.concatenate(pair, axis=0), ones)
    intra, off = [], 0
    for c in range(nsc):
        parts = [jnp.zeros((8, w), F32)] * (sub // 8)
        for s in range(sub):
            n = spans[c * sub + s]
            t_row = lax.broadcasted_iota(jnp.int32, (n, w), 0) + (sub - n)
            a = jnp.where(t_row >= s, att[off:off + n, :], 0.0) * row(v_s, c * sub + s)
            g0 = (sub - n) // 8
            for g in range(g0, sub // 8):
                parts[g] = parts[g] + a[(g - g0) * 8:(g - g0 + 1) * 8, :]
            off += n
        intra.append(parts[0] if len(parts) == 1 else jnp.concatenate(parts, axis=0))
    last = [row(b_s, (c + 1) * sub - 1) for c in range(nsc)]
    kv = [[_bdot(blk(v_s, c)[:, hs], (blk(k_s, c) * jnp.exp2(last[c] - blk(b_s, c)))[:, hs], TN) for hs in heads]
          for c in range(nsc)]
    states = []
    for seq in range(ns):
        st = [s_ref[seq, h] for h in range(HGRN_HEADS)]
        for c in range(seq * nsub, (seq + 1) * nsub):
            states.append(st)
            gamma = jnp.exp2(last[c])
            st = [st[h] * gamma[:, heads[h]] + kv[c][h] for h in range(HGRN_HEADS)]
        for h in range(HGRN_HEADS):
            s_ref[seq, h] = st[h]
    for c in range(nsc):
        q_in = blk(q_s, c) * jnp.exp2(blk(b_s, c))
        o = intra[c] + jnp.concatenate([_bdot(q_in[:, heads[h]], states[c][h], NT) for h in range(HGRN_HEADS)], axis=1)
        intra[c] = o
    o = jnp.concatenate(intra, axis=0)
    ms = _dot_exact_rhs(o * o, ones) * (1.0 / HGRN_HEAD)
    o_ref[...] = o * lax.rsqrt(ms + NORM_EPS) * ng_ref[...]

    @pl.when(j == pl.num_programs(1) - 1)
    def _():
        s_out_ref[...] = s_ref[...]


def _hgrn_scan(p_hgrn, lb, norm_g, s0, seq_len, tile=256, ns_max=16):
    n = p_hgrn.shape[0]
    batch = n // seq_len
    sub = min(HGRN_SUB, seq_len)
    if seq_len >= tile:
        ns, nsub = 1, tile // sub
    else:
        ns, nsub = min(ns_max, batch), seq_len // sub
    rows = ns * nsub * sub
    nt = seq_len // (nsub * sub)
    assert batch % ns == 0 and seq_len % (nsub * sub) == 0
    st = (ns, HGRN_HEADS, HGRN_HEAD, HGRN_HEAD)
    return pl.pallas_call(
        functools.partial(_hgrn_scan_kernel, ns=ns, nsub=nsub, sub=sub),
        out_shape=[jax.ShapeDtypeStruct((n, HGRN_WIDTH), F32), jax.ShapeDtypeStruct(s0.shape, F32)],
        grid=(batch // ns, nt),
        in_specs=[
            pl.BlockSpec((rows, HGRN_IN), lambda b, j: (b * nt + j, 0)),
            pl.BlockSpec((1, HGRN_WIDTH), lambda b, j: (0, 0)),
            pl.BlockSpec((1, HGRN_WIDTH), lambda b, j: (0, 0)),
            pl.BlockSpec(st, lambda b, j: (b, 0, 0, 0)),
        ],
        out_specs=[
            pl.BlockSpec((rows, HGRN_WIDTH), lambda b, j: (b * nt + j, 0)),
            pl.BlockSpec(st, lambda b, j: (b, 0, 0, 0)),
        ],
        scratch_shapes=[pltpu.VMEM(st, F32)] + [pltpu.VMEM((rows, HGRN_WIDTH), F32)] * 4,
        compiler_params=_cparams(("arbitrary", "arbitrary")),
        name="hgrn_scan",
    )(p_hgrn, lb, norm_g, s0)


def _layer_weights(P, l):
    w_in = P['w_in'][l]
    c_q, c_kv = MLA_Q_RANK, MLA_Q_RANK + MLA_KV_RANK
    c_pe = c_kv + MLA_ROPE
    kpe_tile = jnp.zeros((D_MODEL, LANE), F32).at[:, PE_LO:PE_LO + MLA_ROPE].set(w_in[:, c_kv:c_pe])
    w_in_pad = jnp.concatenate([w_in[:, :c_kv], kpe_tile, w_in[:, c_pe:]], axis=1).astype(BF16)

    hd = MLA_NOPE + MLA_ROPE
    wq = P['mla_w_q_b'][l].reshape(MLA_Q_RANK, MLA_HEADS, hd)
    wq = jnp.pad(wq, ((0, 0), (0, 0), (0, QK_PAD - hd))).reshape(MLA_Q_RANK, MLA_HEADS * QK_PAD).astype(BF16)
    wkv = P['mla_w_kv_b'][l].reshape(MLA_KV_RANK, MLA_HEADS, MLA_NOPE + MLA_V)
    w_uk, w_uv = wkv[..., :MLA_NOPE], wkv[..., MLA_NOPE:]
    wk = jnp.pad(w_uk, ((0, 0), (0, 0), (0, QK_PAD - MLA_NOPE))).reshape(MLA_KV_RANK, MLA_HEADS * QK_PAD).astype(BF16)
    wv = jnp.zeros((MLA_KV_RANK, MLA_HEADS, QK_PAD), F32)
    for h in range(MLA_HEADS):
        off = (h % 2) * MLA_V
        wv = wv.at[:, h, off:off + MLA_V].set(w_uv[:, h, :])
    wv = wv.reshape(MLA_KV_RANK, MLA_HEADS * QK_PAD).astype(BF16)
    ext = MLA_KV_RANK + LANE
    wuk_ext = jnp.zeros((MLA_HEADS, QK_PAD, ext), F32)
    wuk_ext = wuk_ext.at[:, :MLA_NOPE, :MLA_KV_RANK].set(jnp.transpose(w_uk, (1, 2, 0)))
    wuk_ext = wuk_ext.at[:, PE_LO:PE_LO + MLA_ROPE, MLA_KV_RANK:MLA_KV_RANK + MLA_ROPE].set(
        jnp.broadcast_to(jnp.eye(MLA_ROPE, dtype=F32), (MLA_HEADS, MLA_ROPE, MLA_ROPE)))
    wuk_ext = wuk_ext.astype(BF16)
    w_uv_h = jnp.transpose(w_uv, (1, 0, 2)).astype(BF16)

    w = RWKV_WIDTH
    w2 = jnp.zeros((DECAY_RANK + ICLR_RANK, 2 * w), F32)
    w2 = w2.at[:DECAY_RANK, :w].set(P['rwkv_w_decay_b'][l]).at[DECAY_RANK:, w:].set(P['rwkv_w_iclr_b'][l])
    wa0 = jnp.concatenate([P['rwkv_w0'][l], P['rwkv_a0'][l]])[None, :]
    row = lambda a: a.reshape(1, -1)
    rwkv_consts = (row(P['rwkv_mu'][l]), w2, wa0, row(P['rwkv_k_k'][l]), row(P['rwkv_k_a'][l]), row(P['rwkv_r_k'][l]))
    vres = None
    if l > 0:
        wva = jnp.pad(P['rwkv_w_vres_a'][l - 1], ((0, 0), (0, LANE - VRES_RANK)))
        wvb = jnp.pad(P['rwkv_w_vres_b'][l - 1], ((0, LANE - VRES_RANK), (0, 0)))
        vres = (row(P['rwkv_v0'][l - 1]), wva, wvb)
    return dict(
        w_in=w_in_pad, pre_g=row(P['pre_norm_g'][l]), post_g=row(P['post_norm_g'][l]),
        w_out=P['w_out'][l].astype(BF16),
        q_g=row(P['mla_q_norm_g'][l]), kv_g=row(P['mla_kv_norm_g'][l]),
        wq=wq, wk=wk, wv=wv, wuk_ext=wuk_ext, w_uv=w_uv_h,
        rwkv_consts=rwkv_consts, vres=vres,
        gn_g=row(P['rwkv_gn_g'][l]), gn_b=row(P['rwkv_gn_b'][l]),
        hgrn_g=jnp.tile(P['hgrn_norm_g'][l], HGRN_HEADS)[None, :],
    )


def _rope_tables(pos, rows):
    half = MLA_ROPE // 2
    inv_freq = ROPE_BASE ** (-jnp.arange(half, dtype=F32) / half)
    ang = pos.astype(F32)[:, None] * inv_freq[None, :]
    cos, sin = jnp.cos(ang), jnp.sin(ang)
    t = pos.shape[0]
    ones, zeros = jnp.ones((t, PE_LO), F32), jnp.zeros((t, PE_LO), F32)
    pad = jnp.zeros((t, LANE - PE_LO - MLA_ROPE), F32)
    cos_t = jnp.concatenate([ones, cos, cos, pad], axis=1)
    sin_t = jnp.concatenate([zeros, -sin, sin, pad], axis=1)
    rep = rows // t
    return jnp.tile(cos_t, (rep, 1)), jnp.tile(sin_t, (rep, 1))


def _trunk(x, mods, pos, shift0, rwkv0, hgrn0, cache, LW, lb_all):
    batch, seq_len, _ = x.shape
    n = batch * seq_len
    x = x.reshape(n, D_MODEL)
    prep_tile = min(512, n)
    cos, sin = _rope_tables(pos, max(seq_len, prep_tile))
    v_first = None
    ckvs, kpes, rs, shs, hs = [], [], [], [], []
    for l in range(DEPTH):
        W = LW[l]
        mod = mods[l][:, None, :]
        p_mla, p_rwkv, p_hgrn, gate = _in_proj(x, mod, W['pre_g'], W['w_in'], seq_len)
        if cache is None:
            q, k, v, ckv, kpe = _mla_prep(p_mla, cos, sin, W['q_g'], W['kv_g'], W['wq'], (W['wk'], W['wv']),
                                          seq_len, absorbed=False)
            o_mla = _attention_prompt(q, k, v, batch, seq_len)
        else:
            cache_ckv, cache_kpe, page_table = cache
            qx, ckv, kpe = _mla_prep(p_mla, cos, sin, W['q_g'], W['kv_g'], W['wq'], (W['wuk_ext'],),
                                     seq_len, absorbed=True)
            ext = qx.shape[1] // MLA_HEADS
            qx = qx.reshape(batch, seq_len, MLA_HEADS, ext).transpose(0, 2, 1, 3).reshape(batch, MLA_HEADS * seq_len, ext)
            o = _attention_paged(qx, ckv.reshape(batch, seq_len, MLA_KV_RANK), kpe.reshape(batch, seq_len, MLA_ROPE),
                                 W['w_uv'], cache_ckv, cache_kpe, page_table, l)
            o_mla = o.reshape(batch, MLA_HEADS, seq_len, MLA_V).transpose(0, 2, 1, 3).reshape(n, MLA_WIDTH)
        p3 = p_rwkv.reshape(batch, seq_len, RWKV_IN)
        if l == 0:
            feat, v_first = _rwkv_prep(p_rwkv, shift0[l], W['rwkv_consts'], None, None, seq_len)
        else:
            feat = _rwkv_prep(p_rwkv, shift0[l], W['rwkv_consts'], v_first, W['vres'], seq_len)
        o_rwkv, s_r = _rwkv_scan(feat, rwkv0[l], W['gn_g'], W['gn_b'], seq_len)
        o_hgrn, s_h = _hgrn_scan(p_hgrn, lb_all[l][None, :], W['hgrn_g'], jnp.swapaxes(hgrn0[l], -1, -2), seq_len)
        s_h = jnp.swapaxes(s_h, -1, -2)
        x = _out_proj(o_mla, o_rwkv, o_hgrn, gate, x, mod, W['post_g'], W['w_out'], seq_len)
        ckvs.append(ckv.reshape(batch, seq_len, MLA_KV_RANK))
        kpes.append(kpe.reshape(batch, seq_len, MLA_ROPE))
        rs.append(s_r)
        shs.append(p3[:, -1, :])
        hs.append(s_h)
    return (x.reshape(batch, seq_len, D_MODEL), jnp.stack(ckvs), jnp.stack(kpes), jnp.stack(rs), jnp.stack(shs),
            jnp.stack(hs))


def kernel(x_prompt, x_sample, c_prompt, c_sample, cache_ckv, cache_kpe, page_table, state_rwkv, state_rwkv_shift, state_hgrn, w_ada, b_ada, pre_norm_g, post_norm_g, w_in, mla_q_norm_g, mla_w_q_b, mla_kv_norm_g, mla_w_kv_b, rwkv_mu, rwkv_w0, rwkv_w_decay_b, rwkv_a0, rwkv_w_iclr_b, rwkv_k_k, rwkv_k_a, rwkv_r_k, rwkv_gn_g, rwkv_gn_b, rwkv_v0, rwkv_w_vres_a, rwkv_w_vres_b, hgrn_lb_raw, hgrn_norm_g, w_out):
    P = {
        'w_in': w_in, 'pre_norm_g': pre_norm_g, 'post_norm_g': post_norm_g,
        'mla_q_norm_g': mla_q_norm_g, 'mla_w_q_b': mla_w_q_b, 'mla_kv_norm_g': mla_kv_norm_g, 'mla_w_kv_b': mla_w_kv_b,
        'rwkv_mu': rwkv_mu, 'rwkv_w0': rwkv_w0, 'rwkv_w_decay_b': rwkv_w_decay_b, 'rwkv_a0': rwkv_a0,
        'rwkv_w_iclr_b': rwkv_w_iclr_b, 'rwkv_k_k': rwkv_k_k, 'rwkv_k_a': rwkv_k_a, 'rwkv_r_k': rwkv_r_k,
        'rwkv_gn_g': rwkv_gn_g, 'rwkv_gn_b': rwkv_gn_b, 'rwkv_v0': rwkv_v0, 'rwkv_w_vres_a': rwkv_w_vres_a,
        'rwkv_w_vres_b': rwkv_w_vres_b, 'hgrn_norm_g': hgrn_norm_g, 'w_out': w_out,
    }
    LW = [_layer_weights(P, l) for l in range(DEPTH)]
    lb_p = jax.nn.softmax(hgrn_lb_raw.astype(F32), axis=0)
    lb_all = jnp.cumsum(lb_p, axis=0) - lb_p[0]
    bp, tp = x_prompt.shape[:2]
    bs, ts = x_sample.shape[:2]
    mods = _ada(jnp.concatenate([c_prompt, c_sample], axis=0), w_ada, b_ada)
    past_len = page_table.shape[1] * PAGE_SIZE
    zeros = lambda *s: jnp.zeros(s, F32)
    out_p = _trunk(x_prompt, mods[:, :bp], jnp.arange(tp), zeros(DEPTH, bp, RWKV_IN),
                   zeros(DEPTH, bp, RWKV_HEADS, RWKV_HEAD, RWKV_HEAD), zeros(DEPTH, bp, HGRN_HEADS, HGRN_HEAD, HGRN_HEAD),
                   None, LW, lb_all)
    out_s = _trunk(x_sample, mods[:, bp:], past_len + jnp.arange(ts), state_rwkv_shift, state_rwkv, state_hgrn,
                   (cache_ckv, cache_kpe, page_table), LW, lb_all)
    return (out_p[0], out_s[0]) + out_p[1:] + out_s[1:]
```

```python
import functools
import math

import jax
import jax.numpy as jnp
from jax import lax
from jax.experimental import pallas as pl
from jax.experimental.pallas import tpu as pltpu

F32 = jnp.float32
BF16 = jnp.bfloat16

D_MODEL = 1024
DEPTH = 2
NORM_EPS = 1e-6
MASK_VALUE = -1e30
MIN_FORGET = 1e-20
PAGE_SIZE = 128
MLA_HEADS = 8
MLA_NOPE = 64
MLA_ROPE = 32
MLA_V = 64
MLA_Q_RANK = 384
MLA_KV_RANK = 256
MLA_WIDTH = MLA_HEADS * MLA_V
MLA_SCALE = (MLA_NOPE + MLA_ROPE) ** -0.5
ROPE_BASE = 10000.0
RWKV_WIDTH = 256
RWKV_HEAD = 64
RWKV_HEADS = 4
DECAY_RANK = 64
ICLR_RANK = 64
VRES_RANK = 32
RWKV_GN_EPS = 64e-5
RWKV_IN = 3 * RWKV_WIDTH + DECAY_RANK + ICLR_RANK
RWKV_CHUNK = 64
HGRN_WIDTH = 256
HGRN_HEADS = 4
HGRN_HEAD = 64
HGRN_IN = 3 * HGRN_WIDTH
HGRN_SUB = 16
LANE = 128
MLA_IN_PAD = MLA_Q_RANK + MLA_KV_RANK + LANE
QK_PAD = LANE
PE_LO = MLA_NOPE
IN_PAD = MLA_IN_PAD + RWKV_IN + HGRN_IN + D_MODEL
RWKV_FEAT = 7 * RWKV_WIDTH
VMEM_LIMIT = 56 * 1024 * 1024

NN = (((1,), (0,)), ((), ()))
NT = (((1,), (1,)), ((), ()))
TN = (((0,), (0,)), ((), ()))


def _bdot(a, b, dims=NN):
    return lax.dot_general(a.astype(BF16), b.astype(BF16), dims, preferred_element_type=F32)


def _split(a):
    hi = a.astype(BF16)
    lo = (a - hi.astype(F32)).astype(BF16)
    return hi, lo


def _dot3(a, b, dims=NN):
    ah, al = _split(a)
    bh, bl = _split(b)
    d = lambda x, y: lax.dot_general(x, y, dims, preferred_element_type=F32)
    return d(ah, bh) + (d(ah, bl) + d(al, bh))


def _dot_exact_lhs(m, x):
    x1 = x.astype(BF16)
    r1 = x - x1.astype(F32)
    x2 = r1.astype(BF16)
    x3 = (r1 - x2.astype(F32)).astype(BF16)
    d = lambda y: lax.dot_general(m, y, NN, preferred_element_type=F32)
    return d(x1) + (d(x2) + d(x3))


def _silu(x):
    return x * jax.nn.sigmoid(x)


def _rms(x, g):
    return x * lax.rsqrt(jnp.mean(x * x, axis=-1, keepdims=True) + NORM_EPS) * g


def _rows(m, ns, seg):
    w = m.shape[-1]
    return jnp.broadcast_to(m, (ns, seg, w)).reshape(ns * seg, w)


def _head_ones(width, head):
    r = lax.broadcasted_iota(jnp.int32, (width, width), 0) // head
    c = lax.broadcasted_iota(jnp.int32, (width, width), 1) // head
    return (r == c).astype(BF16)


def _cparams(sem):
    return pltpu.CompilerParams(dimension_semantics=sem, vmem_limit_bytes=VMEM_LIMIT)


def _seg_tiles(n_rows, seq_len, tile):
    tile = min(tile, n_rows)
    seg = min(seq_len, tile)
    ns = tile // seg
    assert ns * seg == tile and n_rows % tile == 0 and seq_len % seg == 0
    return tile, ns, seg


def _ada_kernel(c_ref, w_ref, b_ref, o_ref):
    o_ref[0] = _dot3(_silu(c_ref[...]), w_ref[0]) + b_ref[0]


def _ada(c_all, w_ada, b_ada):
    bt = c_all.shape[0]
    tn = 768
    return pl.pallas_call(
        _ada_kernel,
        out_shape=jax.ShapeDtypeStruct((DEPTH, bt, 3 * D_MODEL), F32),
        grid=(DEPTH, 3 * D_MODEL // tn),
        in_specs=[
            pl.BlockSpec((bt, D_MODEL), lambda l, j: (0, 0)),
            pl.BlockSpec((1, D_MODEL, tn), lambda l, j: (l, 0, j)),
            pl.BlockSpec((1, 1, tn), lambda l, j: (l, 0, j)),
        ],
        out_specs=pl.BlockSpec((1, bt, tn), lambda l, j: (l, 0, j)),
        compiler_params=_cparams(("arbitrary", "arbitrary")),
        name="ada",
    )(c_all, w_ada, b_ada.reshape(DEPTH, 1, 3 * D_MODEL))


def _in_proj_kernel(x_ref, mod_ref, g_ref, w_ref, o_mla, o_rwkv, o_hgrn, o_gate, *, ns, seg):
    mod = mod_ref[...]
    shift = _rows(mod[:, :, 0:D_MODEL], ns, seg)
    scale = _rows(mod[:, :, D_MODEL:2 * D_MODEL], ns, seg)
    h = (_rms(x_ref[...], g_ref[...]) * (1.0 + scale) + shift).astype(BF16)
    c0 = 0
    for o_ref in (o_mla, o_rwkv, o_hgrn, o_gate):
        c1 = c0 + o_ref.shape[1]
        o_ref[...] = lax.dot_general(h, w_ref[:, c0:c1], NN, preferred_element_type=F32)
        c0 = c1


def _in_proj(x, mod, g, w, seq_len, tile=512):
    n = x.shape[0]
    tm, ns, seg = _seg_tiles(n, seq_len, tile)
    per_seq = seq_len // seg
    widths = (MLA_IN_PAD, RWKV_IN, HGRN_IN, D_MODEL)
    return pl.pallas_call(
        functools.partial(_in_proj_kernel, ns=ns, seg=seg),
        out_shape=[jax.ShapeDtypeStruct((n, wd), F32) for wd in widths],
        grid=(n // tm,),
        in_specs=[
            pl.BlockSpec((tm, D_MODEL), lambda i: (i, 0)),
            pl.BlockSpec((ns, 1, 3 * D_MODEL), lambda i: (i // per_seq, 0, 0)),
            pl.BlockSpec((1, D_MODEL), lambda i: (0, 0)),
            pl.BlockSpec((D_MODEL, IN_PAD), lambda i: (0, 0)),
        ],
        out_specs=[pl.BlockSpec((tm, wd), lambda i: (i, 0)) for wd in widths],
        compiler_params=_cparams(("arbitrary",)),
        name="in_proj",
    )(x, mod, g, w)


def _out_proj_kernel(om_ref, or_ref, oh_ref, gate_ref, x_ref, mod_ref, g_ref, w_ref, o_ref, *, ns, seg):
    sg = _silu(gate_ref[...])
    a0, a1 = MLA_WIDTH, MLA_WIDTH + RWKV_WIDTH
    acc = _bdot(om_ref[...] * sg[:, 0:a0], w_ref[0:a0, :])
    acc += _bdot(or_ref[...] * sg[:, a0:a1], w_ref[a0:a1, :])
    acc += _bdot(oh_ref[...] * sg[:, a1:], w_ref[a1:, :])
    a_gate = _rows(mod_ref[...][:, :, 2 * D_MODEL:], ns, seg)
    o_ref[...] = x_ref[...] + a_gate * _rms(acc, g_ref[...])


def _out_proj(o_mla, o_rwkv, o_hgrn, gate, x, mod, g, w, seq_len, tile=512):
    n = x.shape[0]
    tm, ns, seg = _seg_tiles(n, seq_len, tile)
    per_seq = seq_len // seg
    row = lambda wd: pl.BlockSpec((tm, wd), lambda i: (i, 0))
    return pl.pallas_call(
        functools.partial(_out_proj_kernel, ns=ns, seg=seg),
        out_shape=jax.ShapeDtypeStruct((n, D_MODEL), F32),
        grid=(n // tm,),
        in_specs=[
            row(MLA_WIDTH), row(RWKV_WIDTH), row(HGRN_WIDTH), row(D_MODEL), row(D_MODEL),
            pl.BlockSpec((ns, 1, 3 * D_MODEL), lambda i: (i // per_seq, 0, 0)),
            pl.BlockSpec((1, D_MODEL), lambda i: (0, 0)),
            pl.BlockSpec((D_MODEL, D_MODEL), lambda i: (0, 0)),
        ],
        out_specs=row(D_MODEL),
        compiler_params=_cparams(("arbitrary",)),
        name="out_proj",
    )(o_mla, o_rwkv, o_hgrn, gate, x, mod, g, w)


def _rope_tile(x, cos, sin):
    return x * cos + pltpu.roll(x, LANE - MLA_ROPE // 2, axis=1) * sin


def _mla_prep_kernel(p_ref, cos_ref, sin_ref, qg_ref, kvg_ref, wq_ref, *rest, absorbed):
    p = p_ref[...]
    cos, sin = cos_ref[...], sin_ref[...]
    cq = _rms(p[:, 0:MLA_Q_RANK], qg_ref[...])
    ckv = _rms(p[:, MLA_Q_RANK:MLA_Q_RANK + MLA_KV_RANK], kvg_ref[...])
    kpe = _rope_tile(p[:, MLA_Q_RANK + MLA_KV_RANK:], cos, sin)
    q = _bdot(cq, wq_ref[...])
    if absorbed:
        wuk_ref, qx_out, ckv_out, kpe_out = rest
        ext = wuk_ref.shape[2]
        for h in range(MLA_HEADS):
            qh = _rope_tile(q[:, h * QK_PAD:(h + 1) * QK_PAD], cos, sin)
            qx_out[:, h * ext:(h + 1) * ext] = _bdot(qh, wuk_ref[h]).astype(BF16)
    else:
        wk_ref, wv_ref, q_out, k_out, v_out, ckv_out, kpe_out = rest
        kn = _bdot(ckv, wk_ref[...])
        for h in range(MLA_HEADS):
            sl = slice(h * QK_PAD, (h + 1) * QK_PAD)
            q_out[:, sl] = _rope_tile(q[:, sl], cos, sin).astype(BF16)
            k_out[:, sl] = (kn[:, sl] + kpe).astype(BF16)
        v = _bdot(ckv, wv_ref[...])
        col = lax.broadcasted_iota(jnp.int32, v.shape, 1)
        ones_col = jnp.where((col // QK_PAD) % 2 == 0, MLA_V, 0)
        v_out[...] = jnp.where(col % QK_PAD == ones_col, 1.0, v).astype(BF16)
    ckv_out[...] = ckv
    kpe_out[...] = kpe[:, PE_LO:PE_LO + MLA_ROPE]


def _mla_prep(p_mla, cos, sin, qg, kvg, wq, extra_w, seq_len, absorbed, tile=512):
    n = p_mla.shape[0]
    tm = min(tile, n)
    rep = max(seq_len // tm, 1)
    hw = MLA_HEADS * QK_PAD
    row = lambda wd: pl.BlockSpec((tm, wd), lambda i: (i, 0))
    full = lambda a: pl.BlockSpec(a.shape, lambda i: (0,) * a.ndim)
    tab = pl.BlockSpec((tm, LANE), lambda i: (i % rep, 0))
    if absorbed:
        ext = extra_w[0].shape[2]
        outs = [(MLA_HEADS * ext, BF16), (MLA_KV_RANK, F32), (MLA_ROPE, F32)]
    else:
        outs = [(hw, BF16), (hw, BF16), (hw, BF16), (MLA_KV_RANK, F32), (MLA_ROPE, F32)]
    return pl.pallas_call(
        functools.partial(_mla_prep_kernel, absorbed=absorbed),
        out_shape=[jax.ShapeDtypeStruct((n, wd), dt) for wd, dt in outs],
        grid=(n // tm,),
        in_specs=[row(MLA_IN_PAD), tab, tab, full(qg), full(kvg), full(wq)] + [full(a) for a in extra_w],
        out_specs=[row(wd) for wd, _ in outs],
        compiler_params=_cparams(("arbitrary",)),
        name="mla_prep_absorbed" if absorbed else "mla_prep",
    )(p_mla, cos, sin, qg, kvg, wq, *extra_w)


def _softmax_step(s, v, m, l, acc):
    c = MLA_SCALE * math.log2(math.e)
    m_new = jnp.maximum(m, jnp.max(s, axis=-1, keepdims=True))
    alpha = jnp.exp2((m - m_new) * c)
    p = jnp.exp2((s - m_new) * c)
    l = alpha * l + jnp.sum(p, axis=-1, keepdims=True)
    acc = alpha * acc + lax.dot_general(p.astype(BF16), v, NN, preferred_element_type=F32)
    return m_new, l, acc


def _softmax_steps(ss, vs, ms, accs):
    c = MLA_SCALE * math.log2(math.e)
    m_new = [jnp.maximum(m, jnp.max(s, axis=-1, keepdims=True)) for m, s in zip(ms, ss)]
    alpha = [jnp.exp2((m - mn) * c) for m, mn in zip(ms, m_new)]
    ps = [jnp.exp2((s - mn) * c).astype(BF16) for s, mn in zip(ss, m_new)]
    pv = [lax.dot_general(p, v, NN, preferred_element_type=F32) for p, v in zip(ps, vs)]
    accs = [a * acc + x for a, acc, x in zip(alpha, accs, pv)]
    return m_new, accs


def _attn_kernel(q_ref, k_ref, v_ref, o_ref, *, tq, heads):
    i = pl.program_id(2)
    sls = [slice(h * QK_PAD, (h + 1) * QK_PAD) for h in range(heads)]
    qs = [q_ref[:, sl] for sl in sls]

    def scores(j):
        rows = pl.ds(pl.multiple_of(j * tq, tq), tq)
        ss = [lax.dot_general(q, k_ref[rows, sl], NT, preferred_element_type=F32) for q, sl in zip(qs, sls)]
        return ss, [v_ref[rows, sl] for sl in sls]

    def body(j, carry):
        ss, vs = scores(j)
        return _softmax_steps(ss, vs, *carry)

    init = ([jnp.full((tq, 1), MASK_VALUE, F32)] * heads, [jnp.zeros((tq, LANE), F32)] * heads)
    carry = lax.fori_loop(0, i, body, init)
    ss, vs = scores(i)
    r = lax.broadcasted_iota(jnp.int32, (tq, tq), 0)
    c = lax.broadcasted_iota(jnp.int32, (tq, tq), 1)
    _, accs = _softmax_steps([jnp.where(c <= r, s, MASK_VALUE) for s in ss], vs, *carry)
    low = lax.broadcasted_iota(jnp.int32, (tq, LANE), 1) < MLA_V
    for hp in range(heads // 2):
        even, odd = accs[2 * hp], accs[2 * hp + 1]
        o_ref[:, hp * LANE:(hp + 1) * LANE] = jnp.where(low, even / even[:, MLA_V:MLA_V + 1], odd / odd[:, 0:1])


def _attention_prompt(q, k, v, batch, seq_len, tq=512, heads=4):
    n = q.shape[0]
    tq = min(tq, seq_len)
    nq = seq_len // tq
    wide = heads * QK_PAD
    return pl.pallas_call(
        functools.partial(_attn_kernel, tq=tq, heads=heads),
        out_shape=jax.ShapeDtypeStruct((n, MLA_WIDTH), F32),
        grid=(batch, MLA_HEADS // heads, nq),
        in_specs=[
            pl.BlockSpec((tq, wide), lambda b, h, i: (b * nq + i, h)),
            pl.BlockSpec((seq_len, wide), lambda b, h, i: (b, h)),
            pl.BlockSpec((seq_len, wide), lambda b, h, i: (b, h)),
        ],
        out_specs=pl.BlockSpec((tq, wide // 2), lambda b, h, i: (b * nq + i, h)),
        compiler_params=_cparams(("arbitrary", "arbitrary", "arbitrary")),
        name="attn_prompt",
    )(q, k, v)


def _attn_paged_kernel(pt_ref, q_ref, ckv_new_ref, kpe_new_ref, wuv_ref, ckv_hbm, kpe_hbm, o_ref,
                       ckv_buf, kpe_buf, sem, *, pages, group, t_new, layer):
    b = pl.program_id(0)
    slot = b % 2
    rows = MLA_HEADS * t_new

    def page_copies(elem, slot_idx, lookup):
        out = []
        for pg in range(pages):
            page = pt_ref[elem * pages + pg] if lookup else 0
            out.append(pltpu.make_async_copy(ckv_hbm.at[layer, page], ckv_buf.at[slot_idx, pg], sem.at[slot_idx, 0]))
            out.append(pltpu.make_async_copy(kpe_hbm.at[layer, page], kpe_buf.at[slot_idx, pg], sem.at[slot_idx, 1]))
        return out

    @pl.when(b == 0)
    def _():
        for cp in page_copies(0, 0, True):
            cp.start()

    @pl.when(b + 1 < pl.num_programs(0))
    def _():
        for cp in page_copies(b + 1, 1 - slot, True):
            cp.start()

    for cp in page_copies(b, slot, False):
        cp.wait()

    q = q_ref[0]
    q_lat = q[:, 0:MLA_KV_RANK]
    q_pe = q[:, MLA_KV_RANK:MLA_KV_RANK + MLA_ROPE]
    def scores(n):
        pg = n * group
        ckv = jnp.concatenate([ckv_buf[slot, pg + i].astype(BF16) for i in range(group)], axis=0)
        kpe_t = jnp.concatenate([kpe_buf[slot, pg + i].astype(BF16) for i in range(group)], axis=1)
        return (lax.dot_general(q_lat, ckv, NT, preferred_element_type=F32)
                + lax.dot_general(q_pe, kpe_t, NN, preferred_element_type=F32)), ckv

    ckv = ckv_new_ref[0].astype(BF16)
    s = lax.dot_general(q_lat, ckv, NT, preferred_element_type=F32)
    s += lax.dot_general(q_pe, kpe_new_ref[0].astype(BF16), NT, preferred_element_type=F32)
    t_q = lax.broadcasted_iota(jnp.int32, (rows, t_new), 0) % t_new
    t_k = lax.broadcasted_iota(jnp.int32, (rows, t_new), 1)
    ahead = scores(0)
    m, l, acc = _softmax_step(jnp.where(t_k <= t_q, s, MASK_VALUE), ckv, jnp.full((rows, 1), MASK_VALUE, F32),
                              jnp.zeros((rows, 1), F32), jnp.zeros((rows, MLA_KV_RANK), F32))
    n_chunks = pages // group
    for n in range(n_chunks):
        s, ckv = ahead
        if n + 1 < n_chunks:
            ahead = scores(n + 1)
        m, l, acc = _softmax_step(s, ckv, m, l, acc)
    lat = acc / l
    for h in range(MLA_HEADS):
        o_ref[0, h * t_new:(h + 1) * t_new, :] = _bdot(lat[h * t_new:(h + 1) * t_new, :], wuv_ref[h])


def _attention_paged(q_ext, ckv_new, kpe_new, w_uv, cache_ckv, cache_kpe, page_table, layer, group=16):
    batch, rows, ext = q_ext.shape
    t_new = rows // MLA_HEADS
    pages = page_table.shape[1]
    group = min(group, pages)
    assert pages % group == 0
    kpe_t = jnp.swapaxes(cache_kpe, 2, 3)
    grid_spec = pltpu.PrefetchScalarGridSpec(
        num_scalar_prefetch=1,
        grid=(batch,),
        in_specs=[
            pl.BlockSpec((1, rows, ext), lambda b, pt: (b, 0, 0)),
            pl.BlockSpec((1, t_new, MLA_KV_RANK), lambda b, pt: (b, 0, 0)),
            pl.BlockSpec((1, t_new, MLA_ROPE), lambda b, pt: (b, 0, 0)),
            pl.BlockSpec(w_uv.shape, lambda b, pt: (0, 0, 0)),
            pl.BlockSpec(memory_space=pl.ANY),
            pl.BlockSpec(memory_space=pl.ANY),
        ],
        out_specs=pl.BlockSpec((1, rows, MLA_V), lambda b, pt: (b, 0, 0)),
        scratch_shapes=[
            pltpu.VMEM((2, pages, PAGE_SIZE, MLA_KV_RANK), F32),
            pltpu.VMEM((2, pages, MLA_ROPE, PAGE_SIZE), F32),
            pltpu.SemaphoreType.DMA((2, 2)),
        ],
    )
    return pl.pallas_call(
        functools.partial(_attn_paged_kernel, pages=pages, group=group, t_new=t_new, layer=layer),
        out_shape=jax.ShapeDtypeStruct((batch, rows, MLA_V), F32),
        grid_spec=grid_spec,
        compiler_params=_cparams(("arbitrary",)),
        name="attn_paged",
    )(page_table.reshape(-1), q_ext, ckv_new, kpe_new, w_uv, cache_ckv, kpe_t)


def _rwkv_prep_kernel(p_ref, shift_ref, *rest, ns, seg, per_seq, first):
    if per_seq:
        tail_ref, *rest = rest
    mu_ref, w2_ref, wa0_ref, kk_ref, ka_ref, rk_ref, *rest = rest
    if first:
        feat_ref, vfirst_out = rest
    else:
        vfirst_ref, v0_ref, wva_ref, wvb_ref, feat_ref = rest
    w = RWKV_WIDTH
    p = p_ref[...]
    row = lax.broadcasted_iota(jnp.int32, p.shape, 0)
    before = shift_ref[...]
    if per_seq:
        before = jnp.where(pl.program_id(0) % per_seq == 0, before, tail_ref[7:8, :][None])
    prev = jnp.where(row % seg == 0, _rows(before, ns, seg), pltpu.roll(p, 1, axis=0))
    pm = p + mu_ref[...] * (prev - p)
    r, k, v, xwa = pm[:, 0:w], pm[:, w:2 * w], pm[:, 2 * w:3 * w], pm[:, 3 * w:]
    lane = lax.broadcasted_iota(jnp.int32, xwa.shape, 1)
    da = _bdot(jnp.where(lane < DECAY_RANK, jnp.tanh(xwa), xwa), w2_ref[...]) + wa0_ref[...]
    z = -da[:, 0:w]
    softplus = jnp.maximum(z, 0.0) + jnp.log1p(jnp.exp(-jnp.abs(z)))
    log_decay = -jnp.exp(-softplus - 0.5)
    a = jax.nn.sigmoid(da[:, w:])
    if first:
        vfirst_out[...] = v
    else:
        mix = jax.nn.sigmoid(v0_ref[...] + _bdot(_bdot(v, wva_ref[...]), wvb_ref[...]))
        v = v + (vfirst_ref[...] - v) * mix
    kk = k * kk_ref[...]
    k = k * (1.0 + (a - 1.0) * ka_ref[...])
    ones = _head_ones(w, RWKV_HEAD)
    kk = kk / jnp.maximum(jnp.sqrt(_bdot(kk * kk, ones)), 1e-12)
    bonus = _bdot(r * k * rk_ref[...], ones) * v
    for j, t in enumerate((r, log_decay, k, v, -kk, kk * a, bonus)):
        feat_ref[:, j * w:(j + 1) * w] = t


def _rwkv_prep(p_rwkv, shift_prev, consts, v_first, vres, seq_len, tile=256):
    n = p_rwkv.shape[0]
    tm, ns, seg = _seg_tiles(n, seq_len, tile)
    per_seq = seq_len // seg if ns == 1 else 0
    assert per_seq or seg == seq_len
    first = v_first is None
    row = lambda wd: pl.BlockSpec((tm, wd), lambda i: (i, 0))
    full = lambda a: pl.BlockSpec(a.shape, lambda i: (0,) * a.ndim)
    ins = [p_rwkv, shift_prev[:, None, :]]
    specs = [row(RWKV_IN), pl.BlockSpec((ns, 1, RWKV_IN), lambda i: (i // max(per_seq, 1), 0, 0))]
    if per_seq:
        sub_tiles = tm // 8
        ins.append(p_rwkv)
        specs.append(pl.BlockSpec((8, RWKV_IN), lambda i: (jnp.maximum(i * sub_tiles - 1, 0), 0)))
    ins += list(consts)
    specs += [full(a) for a in consts]
    feat = jax.ShapeDtypeStruct((n, RWKV_FEAT), F32)
    if first:
        out_shape = [feat, jax.ShapeDtypeStruct((n, RWKV_WIDTH), F32)]
        out_specs = [row(RWKV_FEAT), row(RWKV_WIDTH)]
    else:
        ins += [v_first] + list(vres)
        specs += [row(RWKV_WIDTH)] + [full(a) for a in vres]
        out_shape, out_specs = feat, row(RWKV_FEAT)
    return pl.pallas_call(
        functools.partial(_rwkv_prep_kernel, ns=ns, seg=seg, per_seq=per_seq, first=first),
        out_shape=out_shape,
        grid=(n // tm,),
        in_specs=specs,
        out_specs=out_specs,
        compiler_params=_cparams(("arbitrary",)),
        name="rwkv_prep",
    )(*ins)


def _rwkv_chunk_heads(r, lw_cum, lw, k, v, a, b, s0):
    n = len(r)
    c = r[0].shape[0]
    mm = _bdot
    each = lambda f, *xs: [f(*t) for t in zip(*xs)]
    ar = each(lambda a_, r_, cum, lw_: jnp.concatenate([a_ * jnp.exp(cum - lw_), r_ * jnp.exp(cum)], axis=0),
              a, r, lw_cum, lw)
    bk = each(lambda b_, k_, cum: jnp.concatenate([b_ * jnp.exp(-cum), k_ * jnp.exp(-cum)], axis=0),
              b, k, lw_cum)
    g = each(lambda x, y: mm(x, y, NT), ar, bk)
    gr = lax.broadcasted_iota(jnp.int32, (2 * c, 2 * c), 0)
    gc = lax.broadcasted_iota(jnp.int32, (2 * c, 2 * c), 1) % c
    keep = gc < gr - jnp.where(gr < c, 0, c - 1)
    g = [jnp.where(keep, x, 0.0) for x in g]
    a_ab = [x[0:c, 0:c] for x in g]
    eye = (lax.broadcasted_iota(jnp.int32, (c, c), 0) == lax.broadcasted_iota(jnp.int32, (c, c), 1)).astype(F32)
    t_inv = [eye + x for x in a_ab]
    x = each(mm, a_ab, a_ab)
    for _ in range(int(math.log2(c)) - 2):
        tx = each(lambda t_, x_: mm(jnp.concatenate([t_, x_], axis=0), x_), t_inv, x)
        t_inv = each(lambda t_, tx_: t_ + tx_[0:c, :], t_inv, tx)
        x = [tx_[c:, :] for tx_ in tx]
    t_inv = each(lambda t_, x_: t_ + mm(t_, x_), t_inv, x)
    ar_s0 = each(lambda x, y: mm(x, y, NT), ar, s0)
    g_v = each(lambda g_, v_: mm(g_[0:c, :], jnp.concatenate([jnp.zeros_like(v_), v_], axis=0)), g, v)
    u = each(lambda t_, x, y: mm(t_, x[0:c, :] + y), t_inv, ar_s0, g_v)
    uv = each(lambda u_, v_: jnp.concatenate([u_, v_], axis=0), u, v)
    o = each(lambda x, g_, uv_: x[c:, :] + mm(g_[c:, :], uv_), ar_s0, g, uv)
    bk_end = each(lambda b_, k_, cum: jnp.concatenate(
        [b_ * jnp.exp(cum[c - 1:c, :] - cum), k_ * jnp.exp(cum[c - 1:c, :] - cum)], axis=0), b, k, lw_cum)
    s1 = each(lambda s_, cum, uv_, e_: s_ * jnp.exp(cum[c - 1:c, :]) + mm(uv_, e_, TN), s0, lw_cum, uv, bk_end)
    assert len(o) == n
    return o, s1


def _rwkv_scan_kernel(feat_ref, s0_ref, gn_g_ref, gn_b_ref, o_ref, s_out_ref, s_ref, *, ns, chunk):
    j = pl.program_id(1)
    w = RWKV_WIDTH

    @pl.when(j == 0)
    def _():
        s_ref[...] = s0_ref[...]

    tri = (lax.broadcasted_iota(jnp.int32, (chunk, chunk), 1)
           <= lax.broadcasted_iota(jnp.int32, (chunk, chunk), 0)).astype(BF16)

    feats = [feat_ref[q] for q in range(ns)]
    cums = [_dot_exact_lhs(tri, f[:, w:2 * w]) for f in feats]
    inst = [(q, h) for q in range(ns) for h in range(RWKV_HEADS)]
    col = lambda i, h: slice(i * w + h * RWKV_HEAD, i * w + (h + 1) * RWKV_HEAD)
    part = lambda i: [feats[q][:, col(i, h)] for q, h in inst]
    o, s1 = _rwkv_chunk_heads(part(0), [cums[q][:, col(0, h)] for q, h in inst], part(1), part(2), part(3), part(4),
                              part(5), [s_ref[q, h] for q, h in inst])
    normed = []
    for (q, h), o_h, s_h in zip(inst, o, s1):
        s_ref[q, h] = s_h
        mean = jnp.mean(o_h, axis=-1, keepdims=True)
        var = jnp.mean(jnp.square(o_h - mean), axis=-1, keepdims=True)
        normed.append((o_h - mean) * lax.rsqrt(var + RWKV_GN_EPS))
    for q in range(ns):
        o_q = jnp.concatenate(normed[q * RWKV_HEADS:(q + 1) * RWKV_HEADS], axis=1)
        o_ref[q] = o_q * gn_g_ref[...] + gn_b_ref[...] + feats[q][:, 6 * w:]

    @pl.when(j == pl.num_programs(1) - 1)
    def _():
        s_out_ref[...] = s_ref[...]


def _rwkv_scan(feat, s0, gn_g, gn_b, seq_len, ns=4):
    n = feat.shape[0]
    batch = n // seq_len
    chunk = min(RWKV_CHUNK, seq_len)
    nc = seq_len // chunk
    ns = min(ns, batch)
    assert batch % ns == 0 and seq_len % chunk == 0
    st = (ns, RWKV_HEADS, RWKV_HEAD, RWKV_HEAD)
    o, s1 = pl.pallas_call(
        functools.partial(_rwkv_scan_kernel, ns=ns, chunk=chunk),
        out_shape=[jax.ShapeDtypeStruct((batch, seq_len, RWKV_WIDTH), F32), jax.ShapeDtypeStruct(s0.shape, F32)],
        grid=(batch // ns, nc),
        in_specs=[
            pl.BlockSpec((ns, chunk, RWKV_FEAT), lambda b, j: (b, j, 0)),
            pl.BlockSpec(st, lambda b, j: (b, 0, 0, 0)),
            pl.BlockSpec((1, RWKV_WIDTH), lambda b, j: (0, 0)),
            pl.BlockSpec((1, RWKV_WIDTH), lambda b, j: (0, 0)),
        ],
        out_specs=[
            pl.BlockSpec((ns, chunk, RWKV_WIDTH), lambda b, j: (b, j, 0)),
            pl.BlockSpec(st, lambda b, j: (b, 0, 0, 0)),
        ],
        scratch_shapes=[pltpu.VMEM(st, F32)],
        compiler_params=_cparams(("arbitrary", "arbitrary")),
        name="rwkv_scan",
    )(feat.reshape(batch, seq_len, RWKV_FEAT), s0, gn_g, gn_b)
    return o.reshape(n, RWKV_WIDTH), s1


def _hgrn_scan_kernel(p_ref, lb_ref, ng_ref, s0_ref, o_ref, s_out_ref, s_ref, q_s, bk_s, v_s, b_s, *, ns, nsub, sub):
    j = pl.program_id(1)
    w = HGRN_WIDTH
    tile = ns * nsub * sub

    @pl.when(j == 0)
    def _():
        s_ref[...] = s0_ref[...]

    p = p_ref[...]
    lb = lb_ref[...]
    sig = jax.nn.sigmoid(p[:, w:2 * w])
    q_s[...] = _silu(p[:, 0:w]) * HGRN_HEAD ** -0.5
    k = (1.0 - lb) * (1.0 - sig)
    v_s[...] = p[:, 2 * w:]
    log_f = jnp.log(jnp.maximum(lb + (1.0 - lb) * sig, MIN_FORGET))
    tr = lax.broadcasted_iota(jnp.int32, (tile, tile), 0)
    tc = lax.broadcasted_iota(jnp.int32, (tile, tile), 1)
    tri = ((tr // sub == tc // sub) & (tc <= tr)).astype(BF16)
    b = _dot_exact_lhs(tri, log_f) * math.log2(math.e)
    b_s[...] = b
    bk_s[...] = b - jnp.log2(k)
    ones = _head_ones(w, HGRN_HEAD)
    sub_row = lax.broadcasted_iota(jnp.int32, (sub, w), 0)
    nsc = ns * nsub
    blk = lambda ref, c: ref[c * sub:(c + 1) * sub, :]
    row = lambda ref, r: ref[r:r + 1, :]
    heads = [slice(h * HGRN_HEAD, (h + 1) * HGRN_HEAD) for h in range(HGRN_HEADS)]
    pair, spans = [], []
    for c in range(nsc):
        for s in range(sub):
            t0 = c * sub + (s // 8) * 8
            t1 = (c + 1) * sub
            pair.append(q_s[t0:t1, :] * jnp.exp2(jnp.minimum(b_s[t0:t1, :] - row(bk_s, c * sub + s), 0.0)))
            spans.append(t1 - t0)
    att = _bdot(jnp.concatenate(pair, axis=0), ones)
    intra, off = [], 0
    for c in range(nsc):
        parts = [jnp.zeros((8, w), F32)] * (sub // 8)
        for s in range(sub):
            n = spans[c * sub + s]
            t_row = lax.broadcasted_iota(jnp.int32, (n, w), 0) + (sub - n)
            a = jnp.where(t_row >= s, att[off:off + n, :], 0.0) * row(v_s, c * sub + s)
            g0 = (sub - n) // 8
            for g in range(g0, sub // 8):
                parts[g] = parts[g] + a[(g - g0) * 8:(g - g0 + 1) * 8, :]
            off += n
        intra.append(parts[0] if len(parts) == 1 else jnp.concatenate(parts, axis=0))
    last = [row(b_s, (c + 1) * sub - 1) for c in range(nsc)]
    kv = [[_bdot(blk(v_s, c)[:, hs], jnp.exp2(last[c] - blk(bk_s, c))[:, hs], TN) for hs in heads]
          for c in range(nsc)]
    states = []
    for seq in range(ns):
        st = [s_ref[seq, h] for h in range(HGRN_HEADS)]
        for c in range(seq * nsub, (seq + 1) * nsub):
            states.append(st)
            gamma = jnp.exp2(last[c])
            st = [st[h] * gamma[:, heads[h]] + kv[c][h] for h in range(HGRN_HEADS)]
        for h in range(HGRN_HEADS):
            s_ref[seq, h] = st[h]
    for c in range(nsc):
        q_in = blk(q_s, c) * jnp.exp2(blk(b_s, c))
        o = intra[c] + jnp.concatenate([_bdot(q_in[:, heads[h]], states[c][h], NT) for h in range(HGRN_HEADS)], axis=1)
        intra[c] = o
    o = jnp.concatenate(intra, axis=0)
    ms = _bdot(o * o, ones) * (1.0 / HGRN_HEAD)
    o_ref[...] = o * lax.rsqrt(ms + NORM_EPS) * ng_ref[...]

    @pl.when(j == pl.num_programs(1) - 1)
    def _():
        s_out_ref[...] = s_ref[...]


def _hgrn_scan(p_hgrn, lb, norm_g, s0, seq_len, tile=256, ns_max=16):
    n = p_hgrn.shape[0]
    batch = n // seq_len
    sub = min(HGRN_SUB, seq_len)
    if seq_len >= tile:
        ns, nsub = 1, tile // sub
    else:
        ns, nsub = min(ns_max, batch), seq_len // sub
    rows = ns * nsub * sub
    nt = seq_len // (nsub * sub)
    assert batch % ns == 0 and seq_len % (nsub * sub) == 0
    st = (ns, HGRN_HEADS, HGRN_HEAD, HGRN_HEAD)
    return pl.pallas_call(
        functools.partial(_hgrn_scan_kernel, ns=ns, nsub=nsub, sub=sub),
        out_shape=[jax.ShapeDtypeStruct((n, HGRN_WIDTH), F32), jax.ShapeDtypeStruct(s0.shape, F32)],
        grid=(batch // ns, nt),
        in_specs=[
            pl.BlockSpec((rows, HGRN_IN), lambda b, j: (b * nt + j, 0)),
            pl.BlockSpec((1, HGRN_WIDTH), lambda b, j: (0, 0)),
            pl.BlockSpec((1, HGRN_WIDTH), lambda b, j: (0, 0)),
            pl.BlockSpec(st, lambda b, j: (b, 0, 0, 0)),
        ],
        out_specs=[
            pl.BlockSpec((rows, HGRN_WIDTH), lambda b, j: (b * nt + j, 0)),
            pl.BlockSpec(st, lambda b, j: (b, 0, 0, 0)),
        ],
        scratch_shapes=[pltpu.VMEM(st, F32)] + [pltpu.VMEM((rows, HGRN_WIDTH), F32)] * 4,
        compiler_params=_cparams(("arbitrary", "arbitrary")),
        name="hgrn_scan",
    )(p_hgrn, lb, norm_g, s0)


def _layer_weights(P, l):
    w_in = P['w_in'][l]
    c_q, c_kv = MLA_Q_RANK, MLA_Q_RANK + MLA_KV_RANK
    c_pe = c_kv + MLA_ROPE
    half = MLA_ROPE // 2
    tail = QK_PAD - PE_LO - MLA_ROPE - half
    w_pe = w_in[:, c_kv:c_pe]
    kpe_tile = jnp.concatenate(
        [jnp.zeros((D_MODEL, PE_LO), F32), w_pe, w_pe[:, :half], jnp.zeros((D_MODEL, tail), F32)], axis=1)
    w_in_pad = jnp.concatenate([w_in[:, :c_kv], kpe_tile, w_in[:, c_pe:]], axis=1).astype(BF16)

    hd = MLA_NOPE + MLA_ROPE
    wq = P['mla_w_q_b'][l].reshape(MLA_Q_RANK, MLA_HEADS, hd)
    wq = jnp.concatenate([wq, wq[:, :, MLA_NOPE:MLA_NOPE + half], jnp.zeros((MLA_Q_RANK, MLA_HEADS, tail), F32)], axis=2)
    wq = wq.reshape(MLA_Q_RANK, MLA_HEADS * QK_PAD).astype(BF16)
    wkv = P['mla_w_kv_b'][l].reshape(MLA_KV_RANK, MLA_HEADS, MLA_NOPE + MLA_V)
    w_uk, w_uv = wkv[..., :MLA_NOPE], wkv[..., MLA_NOPE:]
    wk = jnp.pad(w_uk, ((0, 0), (0, 0), (0, QK_PAD - MLA_NOPE))).reshape(MLA_KV_RANK, MLA_HEADS * QK_PAD).astype(BF16)
    wv = jnp.zeros((MLA_KV_RANK, MLA_HEADS, QK_PAD), F32)
    for h in range(MLA_HEADS):
        off = (h % 2) * MLA_V
        wv = wv.at[:, h, off:off + MLA_V].set(w_uv[:, h, :])
    wv = wv.reshape(MLA_KV_RANK, MLA_HEADS * QK_PAD).astype(BF16)
    ext = MLA_KV_RANK + LANE
    wuk_ext = jnp.zeros((MLA_HEADS, QK_PAD, ext), F32)
    wuk_ext = wuk_ext.at[:, :MLA_NOPE, :MLA_KV_RANK].set(jnp.transpose(w_uk, (1, 2, 0)))
    wuk_ext = wuk_ext.at[:, PE_LO:PE_LO + MLA_ROPE, MLA_KV_RANK:MLA_KV_RANK + MLA_ROPE].set(
        jnp.broadcast_to(jnp.eye(MLA_ROPE, dtype=F32), (MLA_HEADS, MLA_ROPE, MLA_ROPE)))
    wuk_ext = wuk_ext.astype(BF16)
    w_uv_h = jnp.transpose(w_uv, (1, 0, 2)).astype(BF16)

    w = RWKV_WIDTH
    w2 = jnp.zeros((DECAY_RANK + ICLR_RANK, 2 * w), F32)
    w2 = w2.at[:DECAY_RANK, :w].set(P['rwkv_w_decay_b'][l]).at[DECAY_RANK:, w:].set(P['rwkv_w_iclr_b'][l])
    wa0 = jnp.concatenate([P['rwkv_w0'][l], P['rwkv_a0'][l]])[None, :]
    row = lambda a: a.reshape(1, -1)
    rwkv_consts = (row(P['rwkv_mu'][l]), w2, wa0, row(P['rwkv_k_k'][l]), row(P['rwkv_k_a'][l]), row(P['rwkv_r_k'][l]))
    vres = None
    if l > 0:
        wva = jnp.pad(P['rwkv_w_vres_a'][l - 1], ((0, 0), (0, LANE - VRES_RANK)))
        wvb = jnp.pad(P['rwkv_w_vres_b'][l - 1], ((0, LANE - VRES_RANK), (0, 0)))
        vres = (row(P['rwkv_v0'][l - 1]), wva, wvb)
    return dict(
        w_in=w_in_pad, pre_g=row(P['pre_norm_g'][l]), post_g=row(P['post_norm_g'][l]),
        w_out=P['w_out'][l].astype(BF16),
        q_g=row(P['mla_q_norm_g'][l]), kv_g=row(P['mla_kv_norm_g'][l]),
        wq=wq, wk=wk, wv=wv, wuk_ext=wuk_ext, w_uv=w_uv_h,
        rwkv_consts=rwkv_consts, vres=vres,
        gn_g=row(P['rwkv_gn_g'][l]), gn_b=row(P['rwkv_gn_b'][l]),
        hgrn_g=jnp.tile(P['hgrn_norm_g'][l], HGRN_HEADS)[None, :],
    )


def _rope_tables(pos, rows):
    half = MLA_ROPE // 2
    inv_freq = ROPE_BASE ** (-jnp.arange(half, dtype=F32) / half)
    ang = pos.astype(F32)[:, None] * inv_freq[None, :]
    cos, sin = jnp.cos(ang), jnp.sin(ang)
    t = pos.shape[0]
    ones, zeros = jnp.ones((t, PE_LO), F32), jnp.zeros((t, PE_LO), F32)
    pad = jnp.zeros((t, LANE - PE_LO - MLA_ROPE), F32)
    cos_t = jnp.concatenate([ones, cos, cos, pad], axis=1)
    sin_t = jnp.concatenate([zeros, -sin, sin, pad], axis=1)
    rep = rows // t
    return jnp.tile(cos_t, (rep, 1)), jnp.tile(sin_t, (rep, 1))


def _trunk(x, mods, pos, shift0, rwkv0, hgrn0, cache, LW, lb_all):
    batch, seq_len, _ = x.shape
    n = batch * seq_len
    x = x.reshape(n, D_MODEL)
    prep_tile = min(512, n)
    cos, sin = _rope_tables(pos, max(seq_len, prep_tile))
    v_first = None
    ckvs, kpes, rs, shs, hs = [], [], [], [], []
    for l in range(DEPTH):
        W = LW[l]
        mod = mods[l][:, None, :]
        p_mla, p_rwkv, p_hgrn, gate = _in_proj(x, mod, W['pre_g'], W['w_in'], seq_len)
        if cache is None:
            q, k, v, ckv, kpe = _mla_prep(p_mla, cos, sin, W['q_g'], W['kv_g'], W['wq'], (W['wk'], W['wv']),
                                          seq_len, absorbed=False)
            o_mla = _attention_prompt(q, k, v, batch, seq_len)
        else:
            cache_ckv, cache_kpe, page_table = cache
            qx, ckv, kpe = _mla_prep(p_mla, cos, sin, W['q_g'], W['kv_g'], W['wq'], (W['wuk_ext'],),
                                     seq_len, absorbed=True)
            ext = qx.shape[1] // MLA_HEADS
            qx = qx.reshape(batch, seq_len, MLA_HEADS, ext).transpose(0, 2, 1, 3).reshape(batch, MLA_HEADS * seq_len, ext)
            o = _attention_paged(qx, ckv.reshape(batch, seq_len, MLA_KV_RANK), kpe.reshape(batch, seq_len, MLA_ROPE),
                                 W['w_uv'], cache_ckv, cache_kpe, page_table, l)
            o_mla = o.reshape(batch, MLA_HEADS, seq_len, MLA_V).transpose(0, 2, 1, 3).reshape(n, MLA_WIDTH)
        p3 = p_rwkv.reshape(batch, seq_len, RWKV_IN)
        if l == 0:
            feat, v_first = _rwkv_prep(p_rwkv, shift0[l], W['rwkv_consts'], None, None, seq_len)
        else:
            feat = _rwkv_prep(p_rwkv, shift0[l], W['rwkv_consts'], v_first, W['vres'], seq_len)
        o_rwkv, s_r = _rwkv_scan(feat, rwkv0[l], W['gn_g'], W['gn_b'], seq_len)
        o_hgrn, s_h = _hgrn_scan(p_hgrn, lb_all[l][None, :], W['hgrn_g'], jnp.swapaxes(hgrn0[l], -1, -2), seq_len)
        s_h = jnp.swapaxes(s_h, -1, -2)
        x = _out_proj(o_mla, o_rwkv, o_hgrn, gate, x, mod, W['post_g'], W['w_out'], seq_len)
        ckvs.append(ckv.reshape(batch, seq_len, MLA_KV_RANK))
        kpes.append(kpe.reshape(batch, seq_len, MLA_ROPE))
        rs.append(s_r)
        shs.append(p3[:, -1, :])
        hs.append(s_h)
    return (x.reshape(batch, seq_len, D_MODEL), jnp.stack(ckvs), jnp.stack(kpes), jnp.stack(rs), jnp.stack(shs),
            jnp.stack(hs))


def kernel(x_prompt, x_sample, c_prompt, c_sample, cache_ckv, cache_kpe, page_table, state_rwkv, state_rwkv_shift, state_hgrn, w_ada, b_ada, pre_norm_g, post_norm_g, w_in, mla_q_norm_g, mla_w_q_b, mla_kv_norm_g, mla_w_kv_b, rwkv_mu, rwkv_w0, rwkv_w_decay_b, rwkv_a0, rwkv_w_iclr_b, rwkv_k_k, rwkv_k_a, rwkv_r_k, rwkv_gn_g, rwkv_gn_b, rwkv_v0, rwkv_w_vres_a, rwkv_w_vres_b, hgrn_lb_raw, hgrn_norm_g, w_out):
    P = {
        'w_in': w_in, 'pre_norm_g': pre_norm_g, 'post_norm_g': post_norm_g,
        'mla_q_norm_g': mla_q_norm_g, 'mla_w_q_b': mla_w_q_b, 'mla_kv_norm_g': mla_kv_norm_g, 'mla_w_kv_b': mla_w_kv_b,
        'rwkv_mu': rwkv_mu, 'rwkv_w0': rwkv_w0, 'rwkv_w_decay_b': rwkv_w_decay_b, 'rwkv_a0': rwkv_a0,
        'rwkv_w_iclr_b': rwkv_w_iclr_b, 'rwkv_k_k': rwkv_k_k, 'rwkv_k_a': rwkv_k_a, 'rwkv_r_k': rwkv_r_k,
        'rwkv_gn_g': rwkv_gn_g, 'rwkv_gn_b': rwkv_gn_b, 'rwkv_v0': rwkv_v0, 'rwkv_w_vres_a': rwkv_w_vres_a,
        'rwkv_w_vres_b': rwkv_w_vres_b, 'hgrn_norm_g': hgrn_norm_g, 'w_out': w_out,
    }
    LW = [_layer_weights(P, l) for l in range(DEPTH)]
    lb_p = jax.nn.softmax(hgrn_lb_raw.astype(F32), axis=0)
    lb_all = jnp.cumsum(lb_p, axis=0) - lb_p[0]
    bp, tp = x_prompt.shape[:2]
    bs, ts = x_sample.shape[:2]
    mods = _ada(jnp.concatenate([c_prompt, c_sample], axis=0), w_ada, b_ada)
    past_len = page_table.shape[1] * PAGE_SIZE
    zeros = lambda *s: jnp.zeros(s, F32)
    out_p = _trunk(x_prompt, mods[:, :bp], jnp.arange(tp), zeros(DEPTH, bp, RWKV_IN),
                   zeros(DEPTH, bp, RWKV_HEADS, RWKV_HEAD, RWKV_HEAD), zeros(DEPTH, bp, HGRN_HEADS, HGRN_HEAD, HGRN_HEAD),
                   None, LW, lb_all)
    out_s = _trunk(x_sample, mods[:, bp:], past_len + jnp.arange(ts), state_rwkv_shift, state_rwkv, state_hgrn,
                   (cache_ckv, cache_kpe, page_table), LW, lb_all)
    return (out_p[0], out_s[0]) + out_p[1:] + out_s[1:]
```

```python
import functools
import math

import jax
import jax.numpy as jnp
from jax import lax
from jax.experimental import pallas as pl
from jax.experimental.pallas import tpu as pltpu

F32 = jnp.float32
BF16 = jnp.bfloat16

D_MODEL = 1024
DEPTH = 2
NORM_EPS = 1e-6
MASK_VALUE = -1e30
MIN_FORGET = 1e-20
PAGE_SIZE = 128
MLA_HEADS = 8
MLA_NOPE = 64
MLA_ROPE = 32
MLA_V = 64
MLA_Q_RANK = 384
MLA_KV_RANK = 256
MLA_WIDTH = MLA_HEADS * MLA_V
MLA_SCALE = (MLA_NOPE + MLA_ROPE) ** -0.5
ROPE_BASE = 10000.0
RWKV_WIDTH = 256
RWKV_HEAD = 64
RWKV_HEADS = 4
DECAY_RANK = 64
ICLR_RANK = 64
VRES_RANK = 32
RWKV_GN_EPS = 64e-5
RWKV_IN = 3 * RWKV_WIDTH + DECAY_RANK + ICLR_RANK
RWKV_CHUNK = 64
HGRN_WIDTH = 256
HGRN_HEADS = 4
HGRN_HEAD = 64
HGRN_IN = 3 * HGRN_WIDTH
HGRN_SUB = 16
LANE = 128
MLA_IN_PAD = MLA_Q_RANK + MLA_KV_RANK + LANE
QK_PAD = LANE
PE_LO = MLA_NOPE
IN_PAD = MLA_IN_PAD + RWKV_IN + HGRN_IN + D_MODEL
RWKV_FEAT = 7 * RWKV_WIDTH
VMEM_LIMIT = 56 * 1024 * 1024

NN = (((1,), (0,)), ((), ()))
NT = (((1,), (1,)), ((), ()))
TN = (((0,), (0,)), ((), ()))


def _bdot(a, b, dims=NN):
    return lax.dot_general(a.astype(BF16), b.astype(BF16), dims, preferred_element_type=F32)


def _split(a):
    hi = a.astype(BF16)
    lo = (a - hi.astype(F32)).astype(BF16)
    return hi, lo


def _dot3(a, b, dims=NN):
    ah, al = _split(a)
    bh, bl = _split(b)
    d = lambda x, y: lax.dot_general(x, y, dims, preferred_element_type=F32)
    return d(ah, bh) + (d(ah, bl) + d(al, bh))


def _dot_exact_lhs(m, x):
    x1 = x.astype(BF16)
    r1 = x - x1.astype(F32)
    x2 = r1.astype(BF16)
    x3 = (r1 - x2.astype(F32)).astype(BF16)
    d = lambda y: lax.dot_general(m, y, NN, preferred_element_type=F32)
    return d(x1) + (d(x2) + d(x3))


def _silu(x):
    return x * jax.nn.sigmoid(x)


def _rms(x, g):
    return x * lax.rsqrt(jnp.mean(x * x, axis=-1, keepdims=True) + NORM_EPS) * g


def _rows(m, ns, seg):
    w = m.shape[-1]
    return jnp.broadcast_to(m, (ns, seg, w)).reshape(ns * seg, w)


def _head_ones(width, head):
    r = lax.broadcasted_iota(jnp.int32, (width, width), 0) // head
    c = lax.broadcasted_iota(jnp.int32, (width, width), 1) // head
    return (r == c).astype(BF16)


def _mixer_dtype(seq_len):
    return BF16 if seq_len % 16 == 0 else F32


def _cparams(sem):
    return pltpu.CompilerParams(dimension_semantics=sem, vmem_limit_bytes=VMEM_LIMIT)


def _seg_tiles(n_rows, seq_len, tile):
    tile = min(tile, n_rows)
    seg = min(seq_len, tile)
    ns = tile // seg
    assert ns * seg == tile and n_rows % tile == 0 and seq_len % seg == 0
    return tile, ns, seg


def _ada_kernel(c_ref, w_ref, b_ref, o_ref):
    o_ref[0] = _dot3(_silu(c_ref[...]), w_ref[0]) + b_ref[0]


def _ada(c_all, w_ada, b_ada):
    bt = c_all.shape[0]
    tn = 768
    return pl.pallas_call(
        _ada_kernel,
        out_shape=jax.ShapeDtypeStruct((DEPTH, bt, 3 * D_MODEL), F32),
        grid=(DEPTH, 3 * D_MODEL // tn),
        in_specs=[
            pl.BlockSpec((bt, D_MODEL), lambda l, j: (0, 0)),
            pl.BlockSpec((1, D_MODEL, tn), lambda l, j: (l, 0, j)),
            pl.BlockSpec((1, 1, tn), lambda l, j: (l, 0, j)),
        ],
        out_specs=pl.BlockSpec((1, bt, tn), lambda l, j: (l, 0, j)),
        compiler_params=_cparams(("arbitrary", "arbitrary")),
        name="ada",
    )(c_all, w_ada, b_ada.reshape(DEPTH, 1, 3 * D_MODEL))


IN_COLS = (0, MLA_IN_PAD, MLA_IN_PAD + RWKV_IN, MLA_IN_PAD + RWKV_IN + HGRN_IN, IN_PAD)


def _in_proj_kernel(x_ref, mod_ref, g_ref, w_ref, *rest, ns, seg, per_seq, first):
    mod = mod_ref[...]
    shift = _rows(mod[:, :, 0:D_MODEL], ns, seg)
    scale = _rows(mod[:, :, D_MODEL:2 * D_MODEL], ns, seg)
    h = (_rms(x_ref[...], g_ref[...]) * (1.0 + scale) + shift).astype(BF16)
    proj = lambda j: lax.dot_general(h, w_ref[:, IN_COLS[j]:IN_COLS[j + 1]], NN, preferred_element_type=F32)
    if not per_seq:
        for j, o_ref in enumerate(rest):
            o_ref[...] = proj(j).astype(o_ref.dtype)
        return
    shift_ref, *rest = rest
    vres = None
    if not first:
        vres, rest = rest[:4], rest[4:]
    consts, (o_mla, o_hgrn, o_gate, feat_ref), rest = rest[:6], rest[6:10], rest[10:]
    if first:
        vfirst_out, *rest = rest
    tail_out, prev_s = rest
    o_mla[...] = proj(0)
    o_hgrn[...] = proj(2)
    o_gate[...] = proj(3).astype(o_gate.dtype)
    p = proj(1)
    row = lax.broadcasted_iota(jnp.int32, p.shape, 0)
    before = jnp.where(pl.program_id(0) % per_seq == 0, shift_ref[0], prev_s[...])
    prev = jnp.where(row == 0, jnp.broadcast_to(before, p.shape), pltpu.roll(p, 1, axis=0))
    feats, v_raw = _rwkv_features(p, prev, consts, vres)
    if first:
        vfirst_out[...] = v_raw
    for j, t in enumerate(feats):
        feat_ref[:, j * RWKV_WIDTH:(j + 1) * RWKV_WIDTH] = t
    last = p[p.shape[0] - 1:, :]
    prev_s[...] = last
    tail_out[0] = last


def _in_proj(x, mod, g, w, seq_len, rwkv=None, tile=512):
    n = x.shape[0]
    tm, ns, seg = _seg_tiles(n, seq_len, tile)
    per_seq = seq_len // seg
    row = lambda wd: pl.BlockSpec((tm, wd), lambda i: (i, 0))
    full = lambda a: pl.BlockSpec(a.shape, lambda i: (0,) * a.ndim)
    ins = [x, mod, g, w]
    specs = [row(D_MODEL), pl.BlockSpec((ns, 1, 3 * D_MODEL), lambda i: (i // per_seq, 0, 0)), full(g), full(w)]
    if rwkv is None or ns != 1:
        outs = [(MLA_IN_PAD, F32), (RWKV_IN, F32), (HGRN_IN, F32), (D_MODEL, BF16)]
        return pl.pallas_call(
            functools.partial(_in_proj_kernel, ns=ns, seg=seg, per_seq=0, first=True),
            out_shape=[jax.ShapeDtypeStruct((n, wd), dt) for wd, dt in outs],
            grid=(n // tm,), in_specs=specs, out_specs=[row(wd) for wd, _ in outs],
            compiler_params=_cparams(("arbitrary",)), name="in_proj",
        )(*ins)
    shift_prev, consts, v_first, vres = rwkv
    first = v_first is None
    batch = n // seq_len
    per_batch = pl.BlockSpec((1, 1, RWKV_IN), lambda i: (i // per_seq, 0, 0))
    ins.append(shift_prev[:, None, :])
    specs.append(per_batch)
    if not first:
        ins += [v_first] + list(vres)
        specs += [row(RWKV_WIDTH)] + [full(a) for a in vres]
    ins += list(consts)
    specs += [full(a) for a in consts]
    outs = [(MLA_IN_PAD, F32), (HGRN_IN, F32), (D_MODEL, BF16), (RWKV_FEAT, F32)] + ([(RWKV_WIDTH, F32)] if first else [])
    res = pl.pallas_call(
        functools.partial(_in_proj_kernel, ns=ns, seg=seg, per_seq=per_seq, first=first),
        out_shape=[jax.ShapeDtypeStruct((n, wd), dt) for wd, dt in outs]
        + [jax.ShapeDtypeStruct((batch, 1, RWKV_IN), F32)],
        grid=(n // tm,), in_specs=specs, out_specs=[row(wd) for wd, _ in outs] + [per_batch],
        scratch_shapes=[pltpu.VMEM((1, RWKV_IN), F32)],
        compiler_params=_cparams(("arbitrary",)), name="in_proj_rwkv",
    )(*ins)
    p_mla, p_hgrn, gate, feat = res[:4]
    return p_mla, p_hgrn, gate, feat, (res[4] if first else v_first), res[-1][:, 0, :]


def _out_proj_kernel(om_ref, or_ref, oh_ref, gate_ref, x_ref, mod_ref, g_ref, w_ref, o_ref, *, ns, seg):
    sg = _silu(gate_ref[...].astype(F32))
    a0, a1 = MLA_WIDTH, MLA_WIDTH + RWKV_WIDTH
    acc = _bdot(om_ref[...] * sg[:, 0:a0], w_ref[0:a0, :])
    acc += _bdot(or_ref[...] * sg[:, a0:a1], w_ref[a0:a1, :])
    acc += _bdot(oh_ref[...] * sg[:, a1:], w_ref[a1:, :])
    a_gate = _rows(mod_ref[...][:, :, 2 * D_MODEL:], ns, seg)
    o_ref[...] = x_ref[...] + a_gate * _rms(acc, g_ref[...])


def _out_proj(o_mla, o_rwkv, o_hgrn, gate, x, mod, g, w, seq_len, tile=512):
    n = x.shape[0]
    tm, ns, seg = _seg_tiles(n, seq_len, tile)
    per_seq = seq_len // seg
    row = lambda wd: pl.BlockSpec((tm, wd), lambda i: (i, 0))
    return pl.pallas_call(
        functools.partial(_out_proj_kernel, ns=ns, seg=seg),
        out_shape=jax.ShapeDtypeStruct((n, D_MODEL), F32),
        grid=(n // tm,),
        in_specs=[
            row(MLA_WIDTH), row(RWKV_WIDTH), row(HGRN_WIDTH), row(D_MODEL), row(D_MODEL),
            pl.BlockSpec((ns, 1, 3 * D_MODEL), lambda i: (i // per_seq, 0, 0)),
            pl.BlockSpec((1, D_MODEL), lambda i: (0, 0)),
            pl.BlockSpec((D_MODEL, D_MODEL), lambda i: (0, 0)),
        ],
        out_specs=row(D_MODEL),
        compiler_params=_cparams(("arbitrary",)),
        name="out_proj",
    )(o_mla, o_rwkv, o_hgrn, gate, x, mod, g, w)


def _rope_tile(x, cos, sin):
    return x * cos + pltpu.roll(x, LANE - MLA_ROPE // 2, axis=1) * sin


def _mla_prep_kernel(p_ref, cos_ref, sin_ref, qg_ref, kvg_ref, wq_ref, *rest, absorbed):
    p = p_ref[...]
    cos, sin = cos_ref[...], sin_ref[...]
    cq = _rms(p[:, 0:MLA_Q_RANK], qg_ref[...])
    ckv = _rms(p[:, MLA_Q_RANK:MLA_Q_RANK + MLA_KV_RANK], kvg_ref[...])
    kpe = _rope_tile(p[:, MLA_Q_RANK + MLA_KV_RANK:], cos, sin)
    q = _bdot(cq, wq_ref[...])
    if absorbed:
        wuk_ref, qx_out, ckv_out, kpe_out = rest
        ext = wuk_ref.shape[2]
        for h in range(MLA_HEADS):
            qh = _rope_tile(q[:, h * QK_PAD:(h + 1) * QK_PAD], cos, sin)
            qx_out[:, h * ext:(h + 1) * ext] = _bdot(qh, wuk_ref[h]).astype(BF16)
    else:
        wk_ref, wv_ref, q_out, k_out, v_out, ckv_out, kpe_out = rest
        kn = _bdot(ckv, wk_ref[...])
        for h in range(MLA_HEADS):
            sl = slice(h * QK_PAD, (h + 1) * QK_PAD)
            q_out[:, sl] = _rope_tile(q[:, sl], cos, sin).astype(BF16)
            k_out[:, sl] = (kn[:, sl] + kpe).astype(BF16)
        v = _bdot(ckv, wv_ref[...])
        col = lax.broadcasted_iota(jnp.int32, v.shape, 1)
        ones_col = jnp.where((col // QK_PAD) % 2 == 0, MLA_V, 0)
        v_out[...] = jnp.where(col % QK_PAD == ones_col, 1.0, v).astype(BF16)
    ckv_out[...] = ckv
    kpe_out[...] = kpe[:, PE_LO:PE_LO + MLA_ROPE]


def _mla_prep(p_mla, cos, sin, qg, kvg, wq, extra_w, seq_len, absorbed, tile=512):
    n = p_mla.shape[0]
    tm = min(tile, n)
    rep = max(seq_len // tm, 1)
    hw = MLA_HEADS * QK_PAD
    row = lambda wd: pl.BlockSpec((tm, wd), lambda i: (i, 0))
    full = lambda a: pl.BlockSpec(a.shape, lambda i: (0,) * a.ndim)
    tab = pl.BlockSpec((tm, LANE), lambda i: (i % rep, 0))
    if absorbed:
        ext = extra_w[0].shape[2]
        outs = [(MLA_HEADS * ext, BF16), (MLA_KV_RANK, F32), (MLA_ROPE, F32)]
    else:
        outs = [(hw, BF16), (hw, BF16), (hw, BF16), (MLA_KV_RANK, F32), (MLA_ROPE, F32)]
    return pl.pallas_call(
        functools.partial(_mla_prep_kernel, absorbed=absorbed),
        out_shape=[jax.ShapeDtypeStruct((n, wd), dt) for wd, dt in outs],
        grid=(n // tm,),
        in_specs=[row(MLA_IN_PAD), tab, tab, full(qg), full(kvg), full(wq)] + [full(a) for a in extra_w],
        out_specs=[row(wd) for wd, _ in outs],
        compiler_params=_cparams(("arbitrary",)),
        name="mla_prep_absorbed" if absorbed else "mla_prep",
    )(p_mla, cos, sin, qg, kvg, wq, *extra_w)


def _softmax_step(s, v, m, l, acc):
    c = MLA_SCALE * math.log2(math.e)
    m_new = jnp.maximum(m, jnp.max(s, axis=-1, keepdims=True))
    alpha = jnp.exp2((m - m_new) * c)
    p = jnp.exp2((s - m_new) * c)
    l = alpha * l + jnp.sum(p, axis=-1, keepdims=True)
    acc = alpha * acc + lax.dot_general(p.astype(BF16), v, NN, preferred_element_type=F32)
    return m_new, l, acc


def _softmax_steps(ss, vs, ms, accs):
    c = MLA_SCALE * math.log2(math.e)
    m_new = [jnp.maximum(m, jnp.max(s, axis=-1, keepdims=True)) for m, s in zip(ms, ss)]
    alpha = [jnp.exp2((m - mn) * c) for m, mn in zip(ms, m_new)]
    ps = [jnp.exp2((s - mn) * c).astype(BF16) for s, mn in zip(ss, m_new)]
    pv = [lax.dot_general(p, v, NN, preferred_element_type=F32) for p, v in zip(ps, vs)]
    accs = [a * acc + x for a, acc, x in zip(alpha, accs, pv)]
    return m_new, accs


def _attn_kernel(q_ref, k_ref, v_ref, o_ref, *, tq, heads):
    i = pl.program_id(2)
    sls = [slice(h * QK_PAD, (h + 1) * QK_PAD) for h in range(heads)]
    qs = [q_ref[:, sl] for sl in sls]

    def scores(j):
        rows = pl.ds(pl.multiple_of(j * tq, tq), tq)
        ss = [lax.dot_general(q, k_ref[rows, sl], NT, preferred_element_type=F32) for q, sl in zip(qs, sls)]
        return ss, [v_ref[rows, sl] for sl in sls]

    def body(j, carry):
        ss, vs = scores(j)
        return _softmax_steps(ss, vs, *carry)

    init = ([jnp.full((tq, 1), MASK_VALUE, F32)] * heads, [jnp.zeros((tq, LANE), F32)] * heads)
    carry = lax.fori_loop(0, i, body, init)
    ss, vs = scores(i)
    r = lax.broadcasted_iota(jnp.int32, (tq, tq), 0)
    c = lax.broadcasted_iota(jnp.int32, (tq, tq), 1)
    _, accs = _softmax_steps([jnp.where(c <= r, s, MASK_VALUE) for s in ss], vs, *carry)
    low = lax.broadcasted_iota(jnp.int32, (tq, LANE), 1) < MLA_V
    for hp in range(heads // 2):
        even, odd = accs[2 * hp], accs[2 * hp + 1]
        o_pair = jnp.where(low, even / even[:, MLA_V:MLA_V + 1], odd / odd[:, 0:1])
        o_ref[:, hp * LANE:(hp + 1) * LANE] = o_pair.astype(o_ref.dtype)


def _attention_prompt(q, k, v, batch, seq_len, tq=512, heads=4):
    n = q.shape[0]
    tq = min(tq, seq_len)
    nq = seq_len // tq
    wide = heads * QK_PAD
    return pl.pallas_call(
        functools.partial(_attn_kernel, tq=tq, heads=heads),
        out_shape=jax.ShapeDtypeStruct((n, MLA_WIDTH), _mixer_dtype(seq_len)),
        grid=(batch, MLA_HEADS // heads, nq),
        in_specs=[
            pl.BlockSpec((tq, wide), lambda b, h, i: (b * nq + i, h)),
            pl.BlockSpec((seq_len, wide), lambda b, h, i: (b, h)),
            pl.BlockSpec((seq_len, wide), lambda b, h, i: (b, h)),
        ],
        out_specs=pl.BlockSpec((tq, wide // 2), lambda b, h, i: (b * nq + i, h)),
        compiler_params=_cparams(("arbitrary", "arbitrary", "arbitrary")),
        name="attn_prompt",
    )(q, k, v)


def _attn_paged_kernel(pt_ref, q_ref, ckv_new_ref, kpe_new_ref, wuv_ref, ckv_hbm, kpe_hbm, o_ref,
                       ckv_buf, kpe_buf, sem, *, pages, group, t_new, layer):
    b = pl.program_id(0)
    slot = b % 2
    rows = MLA_HEADS * t_new

    def page_copies(elem, slot_idx, lookup):
        out = []
        for pg in range(pages):
            page = pt_ref[elem * pages + pg] if lookup else 0
            out.append(pltpu.make_async_copy(ckv_hbm.at[layer, page], ckv_buf.at[slot_idx, pg], sem.at[slot_idx, 0]))
            out.append(pltpu.make_async_copy(kpe_hbm.at[layer, page], kpe_buf.at[slot_idx, pg], sem.at[slot_idx, 1]))
        return out

    @pl.when(b == 0)
    def _():
        for cp in page_copies(0, 0, True):
            cp.start()

    @pl.when(b + 1 < pl.num_programs(0))
    def _():
        for cp in page_copies(b + 1, 1 - slot, True):
            cp.start()

    for cp in page_copies(b, slot, False):
        cp.wait()

    q = q_ref[0]
    q_lat = q[:, 0:MLA_KV_RANK]
    q_pe = q[:, MLA_KV_RANK:MLA_KV_RANK + MLA_ROPE]
    def scores(n):
        pg = n * group
        ckv = jnp.concatenate([ckv_buf[slot, pg + i].astype(BF16) for i in range(group)], axis=0)
        kpe_t = jnp.concatenate([kpe_buf[slot, pg + i].astype(BF16) for i in range(group)], axis=1)
        return (lax.dot_general(q_lat, ckv, NT, preferred_element_type=F32)
                + lax.dot_general(q_pe, kpe_t, NN, preferred_element_type=F32)), ckv

    ckv = ckv_new_ref[0].astype(BF16)
    s = lax.dot_general(q_lat, ckv, NT, preferred_element_type=F32)
    s += lax.dot_general(q_pe, kpe_new_ref[0].astype(BF16), NT, preferred_element_type=F32)
    t_q = lax.broadcasted_iota(jnp.int32, (rows, t_new), 0) % t_new
    t_k = lax.broadcasted_iota(jnp.int32, (rows, t_new), 1)
    ahead = scores(0)
    m, l, acc = _softmax_step(jnp.where(t_k <= t_q, s, MASK_VALUE), ckv, jnp.full((rows, 1), MASK_VALUE, F32),
                              jnp.zeros((rows, 1), F32), jnp.zeros((rows, MLA_KV_RANK), F32))
    n_chunks = pages // group
    for n in range(n_chunks):
        s, ckv = ahead
        if n + 1 < n_chunks:
            ahead = scores(n + 1)
        m, l, acc = _softmax_step(s, ckv, m, l, acc)
    lat = acc / l
    for h in range(MLA_HEADS):
        o_ref[0, h * t_new:(h + 1) * t_new, :] = _bdot(lat[h * t_new:(h + 1) * t_new, :], wuv_ref[h])


def _attention_paged(q_ext, ckv_new, kpe_new, w_uv, cache_ckv, cache_kpe, page_table, layer, group=16):
    batch, rows, ext = q_ext.shape
    t_new = rows // MLA_HEADS
    pages = page_table.shape[1]
    group = min(group, pages)
    assert pages % group == 0
    kpe_t = jnp.swapaxes(cache_kpe, 2, 3)
    grid_spec = pltpu.PrefetchScalarGridSpec(
        num_scalar_prefetch=1,
        grid=(batch,),
        in_specs=[
            pl.BlockSpec((1, rows, ext), lambda b, pt: (b, 0, 0)),
            pl.BlockSpec((1, t_new, MLA_KV_RANK), lambda b, pt: (b, 0, 0)),
            pl.BlockSpec((1, t_new, MLA_ROPE), lambda b, pt: (b, 0, 0)),
            pl.BlockSpec(w_uv.shape, lambda b, pt: (0, 0, 0)),
            pl.BlockSpec(memory_space=pl.ANY),
            pl.BlockSpec(memory_space=pl.ANY),
        ],
        out_specs=pl.BlockSpec((1, rows, MLA_V), lambda b, pt: (b, 0, 0)),
        scratch_shapes=[
            pltpu.VMEM((2, pages, PAGE_SIZE, MLA_KV_RANK), F32),
            pltpu.VMEM((2, pages, MLA_ROPE, PAGE_SIZE), F32),
            pltpu.SemaphoreType.DMA((2, 2)),
        ],
    )
    return pl.pallas_call(
        functools.partial(_attn_paged_kernel, pages=pages, group=group, t_new=t_new, layer=layer),
        out_shape=jax.ShapeDtypeStruct((batch, rows, MLA_V), F32),
        grid_spec=grid_spec,
        compiler_params=_cparams(("arbitrary",)),
        name="attn_paged",
    )(page_table.reshape(-1), q_ext, ckv_new, kpe_new, w_uv, cache_ckv, kpe_t)


def _rwkv_prep_kernel(p_ref, shift_ref, *rest, ns, seg, per_seq, first):
    if per_seq:
        tail_ref, *rest = rest
    mu_ref, w2_ref, wa0_ref, kk_ref, ka_ref, rk_ref, *rest = rest
    if first:
        feat_ref, vfirst_out = rest
    else:
        vfirst_ref, v0_ref, wva_ref, wvb_ref, feat_ref = rest
    p = p_ref[...]
    row = lax.broadcasted_iota(jnp.int32, p.shape, 0)
    before = shift_ref[...]
    if per_seq:
        before = jnp.where(pl.program_id(0) % per_seq == 0, before, tail_ref[7:8, :][None])
    prev = jnp.where(row % seg == 0, _rows(before, ns, seg), pltpu.roll(p, 1, axis=0))
    consts = (mu_ref, w2_ref, wa0_ref, kk_ref, ka_ref, rk_ref)
    vres = None if first else (vfirst_ref, v0_ref, wva_ref, wvb_ref)
    feats, v_raw = _rwkv_features(p, prev, consts, vres)
    if first:
        vfirst_out[...] = v_raw
    for j, t in enumerate(feats):
        feat_ref[:, j * RWKV_WIDTH:(j + 1) * RWKV_WIDTH] = t


def _rwkv_features(p, prev, consts, vres):
    mu_ref, w2_ref, wa0_ref, kk_ref, ka_ref, rk_ref = consts
    w = RWKV_WIDTH
    pm = p + mu_ref[...] * (prev - p)
    r, k, v, xwa = pm[:, 0:w], pm[:, w:2 * w], pm[:, 2 * w:3 * w], pm[:, 3 * w:]
    lane = lax.broadcasted_iota(jnp.int32, xwa.shape, 1)
    da = _bdot(jnp.where(lane < DECAY_RANK, jnp.tanh(xwa), xwa), w2_ref[...]) + wa0_ref[...]
    z = -da[:, 0:w]
    softplus = jnp.maximum(z, 0.0) + jnp.log1p(jnp.exp(-jnp.abs(z)))
    log_decay = -jnp.exp(-softplus - 0.5)
    a = jax.nn.sigmoid(da[:, w:])
    v_raw = v
    if vres is not None:
        vfirst_ref, v0_ref, wva_ref, wvb_ref = vres
        mix = jax.nn.sigmoid(v0_ref[...] + _bdot(_bdot(v, wva_ref[...]), wvb_ref[...]))
        v = v + (vfirst_ref[...] - v) * mix
    kk = k * kk_ref[...]
    k = k * (1.0 + (a - 1.0) * ka_ref[...])
    ones = _head_ones(w, RWKV_HEAD)
    kk = kk / jnp.maximum(jnp.sqrt(_bdot(kk * kk, ones)), 1e-12)
    bonus = _bdot(r * k * rk_ref[...], ones) * v
    return (r, log_decay, k, v, -kk, kk * a, bonus), v_raw


def _rwkv_prep(p_rwkv, shift_prev, consts, v_first, vres, seq_len, tile=256):
    n = p_rwkv.shape[0]
    tm, ns, seg = _seg_tiles(n, seq_len, tile)
    per_seq = seq_len // seg if ns == 1 else 0
    assert per_seq or seg == seq_len
    first = v_first is None
    row = lambda wd: pl.BlockSpec((tm, wd), lambda i: (i, 0))
    full = lambda a: pl.BlockSpec(a.shape, lambda i: (0,) * a.ndim)
    ins = [p_rwkv, shift_prev[:, None, :]]
    specs = [row(RWKV_IN), pl.BlockSpec((ns, 1, RWKV_IN), lambda i: (i // max(per_seq, 1), 0, 0))]
    if per_seq:
        sub_tiles = tm // 8
        ins.append(p_rwkv)
        specs.append(pl.BlockSpec((8, RWKV_IN), lambda i: (jnp.maximum(i * sub_tiles - 1, 0), 0)))
    ins += list(consts)
    specs += [full(a) for a in consts]
    feat = jax.ShapeDtypeStruct((n, RWKV_FEAT), F32)
    if first:
        out_shape = [feat, jax.ShapeDtypeStruct((n, RWKV_WIDTH), F32)]
        out_specs = [row(RWKV_FEAT), row(RWKV_WIDTH)]
    else:
        ins += [v_first] + list(vres)
        specs += [row(RWKV_WIDTH)] + [full(a) for a in vres]
        out_shape, out_specs = feat, row(RWKV_FEAT)
    return pl.pallas_call(
        functools.partial(_rwkv_prep_kernel, ns=ns, seg=seg, per_seq=per_seq, first=first),
        out_shape=out_shape,
        grid=(n // tm,),
        in_specs=specs,
        out_specs=out_specs,
        compiler_params=_cparams(("arbitrary",)),
        name="rwkv_prep",
    )(*ins)


def _rwkv_chunk_heads(r, lw_cum, lw, k, v, a, b, s0):
    n = len(r)
    c = r[0].shape[0]
    mm = _bdot
    each = lambda f, *xs: [f(*t) for t in zip(*xs)]
    ar = each(lambda a_, r_, cum, lw_: jnp.concatenate([a_ * jnp.exp(cum - lw_), r_ * jnp.exp(cum)], axis=0),
              a, r, lw_cum, lw)
    bk = each(lambda b_, k_, cum: jnp.concatenate([b_ * jnp.exp(-cum), k_ * jnp.exp(-cum)], axis=0),
              b, k, lw_cum)
    g = each(lambda x, y: mm(x, y, NT), ar, bk)
    gr = lax.broadcasted_iota(jnp.int32, (2 * c, 2 * c), 0)
    gc = lax.broadcasted_iota(jnp.int32, (2 * c, 2 * c), 1) % c
    keep = gc < gr - jnp.where(gr < c, 0, c - 1)
    g = [jnp.where(keep, x, 0.0) for x in g]
    a_ab = [x[0:c, 0:c] for x in g]
    eye = (lax.broadcasted_iota(jnp.int32, (c, c), 0) == lax.broadcasted_iota(jnp.int32, (c, c), 1)).astype(F32)
    t_inv = [eye + x for x in a_ab]
    x = each(mm, a_ab, a_ab)
    for _ in range(int(math.log2(c)) - 2):
        tx = each(lambda t_, x_: mm(jnp.concatenate([t_, x_], axis=0), x_), t_inv, x)
        t_inv = each(lambda t_, tx_: t_ + tx_[0:c, :], t_inv, tx)
        x = [tx_[c:, :] for tx_ in tx]
    t_inv = each(lambda t_, x_: t_ + mm(t_, x_), t_inv, x)
    ar_s0 = each(lambda x, y: mm(x, y, NT), ar, s0)
    g_v = each(lambda g_, v_: mm(g_[0:c, :], jnp.concatenate([jnp.zeros_like(v_), v_], axis=0)), g, v)
    u = each(lambda t_, x, y: mm(t_, x[0:c, :] + y), t_inv, ar_s0, g_v)
    uv = each(lambda u_, v_: jnp.concatenate([u_, v_], axis=0), u, v)
    o = each(lambda x, g_, uv_: x[c:, :] + mm(g_[c:, :], uv_), ar_s0, g, uv)
    bk_end = each(lambda b_, k_, cum: jnp.concatenate(
        [b_ * jnp.exp(cum[c - 1:c, :] - cum), k_ * jnp.exp(cum[c - 1:c, :] - cum)], axis=0), b, k, lw_cum)
    s1 = each(lambda s_, cum, uv_, e_: s_ * jnp.exp(cum[c - 1:c, :]) + mm(uv_, e_, TN), s0, lw_cum, uv, bk_end)
    assert len(o) == n
    return o, s1


def _rwkv_scan_kernel(feat_ref, s0_ref, gn_g_ref, gn_b_ref, o_ref, s_out_ref, s_ref, *, ns, chunk):
    j = pl.program_id(1)
    w = RWKV_WIDTH

    @pl.when(j == 0)
    def _():
        s_ref[...] = s0_ref[...]

    tri = (lax.broadcasted_iota(jnp.int32, (chunk, chunk), 1)
           <= lax.broadcasted_iota(jnp.int32, (chunk, chunk), 0)).astype(BF16)

    feats = [feat_ref[q] for q in range(ns)]
    cums = [_dot_exact_lhs(tri, f[:, w:2 * w]) for f in feats]
    inst = [(q, h) for q in range(ns) for h in range(RWKV_HEADS)]
    col = lambda i, h: slice(i * w + h * RWKV_HEAD, i * w + (h + 1) * RWKV_HEAD)
    part = lambda i: [feats[q][:, col(i, h)] for q, h in inst]
    o, s1 = _rwkv_chunk_heads(part(0), [cums[q][:, col(0, h)] for q, h in inst], part(1), part(2), part(3), part(4),
                              part(5), [s_ref[q, h] for q, h in inst])
    normed = []
    for (q, h), o_h, s_h in zip(inst, o, s1):
        s_ref[q, h] = s_h
        mean = jnp.mean(o_h, axis=-1, keepdims=True)
        var = jnp.mean(jnp.square(o_h - mean), axis=-1, keepdims=True)
        normed.append((o_h - mean) * lax.rsqrt(var + RWKV_GN_EPS))
    for q in range(ns):
        o_q = jnp.concatenate(normed[q * RWKV_HEADS:(q + 1) * RWKV_HEADS], axis=1)
        o_ref[q] = (o_q * gn_g_ref[...] + gn_b_ref[...] + feats[q][:, 6 * w:]).astype(o_ref.dtype)

    @pl.when(j == pl.num_programs(1) - 1)
    def _():
        s_out_ref[...] = s_ref[...]


def _rwkv_scan(feat, s0, gn_g, gn_b, seq_len, ns=4):
    n = feat.shape[0]
    batch = n // seq_len
    chunk = min(RWKV_CHUNK, seq_len)
    nc = seq_len // chunk
    ns = min(ns, batch)
    assert batch % ns == 0 and seq_len % chunk == 0
    st = (ns, RWKV_HEADS, RWKV_HEAD, RWKV_HEAD)
    o, s1 = pl.pallas_call(
        functools.partial(_rwkv_scan_kernel, ns=ns, chunk=chunk),
        out_shape=[jax.ShapeDtypeStruct((batch, seq_len, RWKV_WIDTH), _mixer_dtype(seq_len)),
                   jax.ShapeDtypeStruct(s0.shape, F32)],
        grid=(batch // ns, nc),
        in_specs=[
            pl.BlockSpec((ns, chunk, RWKV_FEAT), lambda b, j: (b, j, 0)),
            pl.BlockSpec(st, lambda b, j: (b, 0, 0, 0)),
            pl.BlockSpec((1, RWKV_WIDTH), lambda b, j: (0, 0)),
            pl.BlockSpec((1, RWKV_WIDTH), lambda b, j: (0, 0)),
        ],
        out_specs=[
            pl.BlockSpec((ns, chunk, RWKV_WIDTH), lambda b, j: (b, j, 0)),
            pl.BlockSpec(st, lambda b, j: (b, 0, 0, 0)),
        ],
        scratch_shapes=[pltpu.VMEM(st, F32)],
        compiler_params=_cparams(("arbitrary", "arbitrary")),
        name="rwkv_scan",
    )(feat.reshape(batch, seq_len, RWKV_FEAT), s0, gn_g, gn_b)
    return o.reshape(n, RWKV_WIDTH), s1


def _hgrn_scan_kernel(p_ref, lb_ref, ng_ref, s0_ref, o_ref, s_out_ref, s_ref, q_s, bk_s, v_s, b_s, *, ns, nsub, sub):
    j = pl.program_id(1)
    w = HGRN_WIDTH
    tile = ns * nsub * sub

    @pl.when(j == 0)
    def _():
        s_ref[...] = s0_ref[...]

    p = p_ref[...]
    lb = lb_ref[...]
    sig = jax.nn.sigmoid(p[:, w:2 * w])
    q_s[...] = _silu(p[:, 0:w]) * HGRN_HEAD ** -0.5
    k = (1.0 - lb) * (1.0 - sig)
    v_s[...] = p[:, 2 * w:]
    log_f = jnp.log(jnp.maximum(lb + (1.0 - lb) * sig, MIN_FORGET))
    tr = lax.broadcasted_iota(jnp.int32, (tile, tile), 0)
    tc = lax.broadcasted_iota(jnp.int32, (tile, tile), 1)
    tri = ((tr // sub == tc // sub) & (tc <= tr)).astype(BF16)
    b = _dot_exact_lhs(tri, log_f) * math.log2(math.e)
    b_s[...] = b
    bk_s[...] = b - jnp.log2(k)
    ones = _head_ones(w, HGRN_HEAD)
    sub_row = lax.broadcasted_iota(jnp.int32, (sub, w), 0)
    nsc = ns * nsub
    blk = lambda ref, c: ref[c * sub:(c + 1) * sub, :]
    row = lambda ref, r: ref[r:r + 1, :]
    heads = [slice(h * HGRN_HEAD, (h + 1) * HGRN_HEAD) for h in range(HGRN_HEADS)]
    pair, spans = [], []
    for c in range(nsc):
        for s in range(sub):
            t0 = c * sub + (s // 8) * 8
            t1 = (c + 1) * sub
            pair.append(q_s[t0:t1, :] * jnp.exp2(jnp.minimum(b_s[t0:t1, :] - row(bk_s, c * sub + s), 0.0)))
            spans.append(t1 - t0)
    att = _bdot(jnp.concatenate(pair, axis=0), ones)
    intra, off = [], 0
    for c in range(nsc):
        parts = [jnp.zeros((8, w), F32)] * (sub // 8)
        for s in range(sub):
            n = spans[c * sub + s]
            t_row = lax.broadcasted_iota(jnp.int32, (n, w), 0) + (sub - n)
            a = jnp.where(t_row >= s, att[off:off + n, :], 0.0) * row(v_s, c * sub + s)
            g0 = (sub - n) // 8
            for g in range(g0, sub // 8):
                parts[g] = parts[g] + a[(g - g0) * 8:(g - g0 + 1) * 8, :]
            off += n
        intra.append(parts[0] if len(parts) == 1 else jnp.concatenate(parts, axis=0))
    last = [row(b_s, (c + 1) * sub - 1) for c in range(nsc)]
    kv = [[_bdot(blk(v_s, c)[:, hs], jnp.exp2(last[c] - blk(bk_s, c))[:, hs], TN) for hs in heads]
          for c in range(nsc)]
    states = []
    for seq in range(ns):
        st = [s_ref[seq, h] for h in range(HGRN_HEADS)]
        for c in range(seq * nsub, (seq + 1) * nsub):
            states.append(st)
            gamma = jnp.exp2(last[c])
            st = [st[h] * gamma[:, heads[h]] + kv[c][h] for h in range(HGRN_HEADS)]
        for h in range(HGRN_HEADS):
            s_ref[seq, h] = st[h]
    for c in range(nsc):
        q_in = blk(q_s, c) * jnp.exp2(blk(b_s, c))
        o = intra[c] + jnp.concatenate([_bdot(q_in[:, heads[h]], states[c][h], NT) for h in range(HGRN_HEADS)], axis=1)
        intra[c] = o
    o = jnp.concatenate(intra, axis=0)
    ms = _bdot(o * o, ones) * (1.0 / HGRN_HEAD)
    o_ref[...] = (o * lax.rsqrt(ms + NORM_EPS) * ng_ref[...]).astype(o_ref.dtype)

    @pl.when(j == pl.num_programs(1) - 1)
    def _():
        s_out_ref[...] = s_ref[...]


def _hgrn_scan(p_hgrn, lb, norm_g, s0, seq_len, tile=256, ns_max=16):
    n = p_hgrn.shape[0]
    batch = n // seq_len
    sub = min(HGRN_SUB, seq_len)
    if seq_len >= tile:
        ns, nsub = 1, tile // sub
    else:
        ns, nsub = min(ns_max, batch), seq_len // sub
    rows = ns * nsub * sub
    nt = seq_len // (nsub * sub)
    assert batch % ns == 0 and seq_len % (nsub * sub) == 0
    st = (ns, HGRN_HEADS, HGRN_HEAD, HGRN_HEAD)
    return pl.pallas_call(
        functools.partial(_hgrn_scan_kernel, ns=ns, nsub=nsub, sub=sub),
        out_shape=[jax.ShapeDtypeStruct((n, HGRN_WIDTH), _mixer_dtype(seq_len)), jax.ShapeDtypeStruct(s0.shape, F32)],
        grid=(batch // ns, nt),
        in_specs=[
            pl.BlockSpec((rows, HGRN_IN), lambda b, j: (b * nt + j, 0)),
            pl.BlockSpec((1, HGRN_WIDTH), lambda b, j: (0, 0)),
            pl.BlockSpec((1, HGRN_WIDTH), lambda b, j: (0, 0)),
            pl.BlockSpec(st, lambda b, j: (b, 0, 0, 0)),
        ],
        out_specs=[
            pl.BlockSpec((rows, HGRN_WIDTH), lambda b, j: (b * nt + j, 0)),
            pl.BlockSpec(st, lambda b, j: (b, 0, 0, 0)),
        ],
        scratch_shapes=[pltpu.VMEM(st, F32)] + [pltpu.VMEM((rows, HGRN_WIDTH), F32)] * 4,
        compiler_params=_cparams(("arbitrary", "arbitrary")),
        name="hgrn_scan",
    )(p_hgrn, lb, norm_g, s0)


def _layer_weights(P, l):
    w_in = P['w_in'][l]
    c_q, c_kv = MLA_Q_RANK, MLA_Q_RANK + MLA_KV_RANK
    c_pe = c_kv + MLA_ROPE
    half = MLA_ROPE // 2
    tail = QK_PAD - PE_LO - MLA_ROPE - half
    w_pe = w_in[:, c_kv:c_pe]
    kpe_tile = jnp.concatenate(
        [jnp.zeros((D_MODEL, PE_LO), F32), w_pe, w_pe[:, :half], jnp.zeros((D_MODEL, tail), F32)], axis=1)
    w_in_pad = jnp.concatenate([w_in[:, :c_kv], kpe_tile, w_in[:, c_pe:]], axis=1).astype(BF16)

    hd = MLA_NOPE + MLA_ROPE
    wq = P['mla_w_q_b'][l].reshape(MLA_Q_RANK, MLA_HEADS, hd)
    wq = jnp.concatenate([wq, wq[:, :, MLA_NOPE:MLA_NOPE + half], jnp.zeros((MLA_Q_RANK, MLA_HEADS, tail), F32)], axis=2)
    wq = wq.reshape(MLA_Q_RANK, MLA_HEADS * QK_PAD).astype(BF16)
    wkv = P['mla_w_kv_b'][l].reshape(MLA_KV_RANK, MLA_HEADS, MLA_NOPE + MLA_V)
    w_uk, w_uv = wkv[..., :MLA_NOPE], wkv[..., MLA_NOPE:]
    wk = jnp.pad(w_uk, ((0, 0), (0, 0), (0, QK_PAD - MLA_NOPE))).reshape(MLA_KV_RANK, MLA_HEADS * QK_PAD).astype(BF16)
    wv = jnp.zeros((MLA_KV_RANK, MLA_HEADS, QK_PAD), F32)
    for h in range(MLA_HEADS):
        off = (h % 2) * MLA_V
        wv = wv.at[:, h, off:off + MLA_V].set(w_uv[:, h, :])
    wv = wv.reshape(MLA_KV_RANK, MLA_HEADS * QK_PAD).astype(BF16)
    ext = MLA_KV_RANK + LANE
    wuk_ext = jnp.zeros((MLA_HEADS, QK_PAD, ext), F32)
    wuk_ext = wuk_ext.at[:, :MLA_NOPE, :MLA_KV_RANK].set(jnp.transpose(w_uk, (1, 2, 0)))
    wuk_ext = wuk_ext.at[:, PE_LO:PE_LO + MLA_ROPE, MLA_KV_RANK:MLA_KV_RANK + MLA_ROPE].set(
        jnp.broadcast_to(jnp.eye(MLA_ROPE, dtype=F32), (MLA_HEADS, MLA_ROPE, MLA_ROPE)))
    wuk_ext = wuk_ext.astype(BF16)
    w_uv_h = jnp.transpose(w_uv, (1, 0, 2)).astype(BF16)

    w = RWKV_WIDTH
    w2 = jnp.zeros((DECAY_RANK + ICLR_RANK, 2 * w), F32)
    w2 = w2.at[:DECAY_RANK, :w].set(P['rwkv_w_decay_b'][l]).at[DECAY_RANK:, w:].set(P['rwkv_w_iclr_b'][l])
    wa0 = jnp.concatenate([P['rwkv_w0'][l], P['rwkv_a0'][l]])[None, :]
    row = lambda a: a.reshape(1, -1)
    rwkv_consts = (row(P['rwkv_mu'][l]), w2, wa0, row(P['rwkv_k_k'][l]), row(P['rwkv_k_a'][l]), row(P['rwkv_r_k'][l]))
    vres = None
    if l > 0:
        wva = jnp.pad(P['rwkv_w_vres_a'][l - 1], ((0, 0), (0, LANE - VRES_RANK)))
        wvb = jnp.pad(P['rwkv_w_vres_b'][l - 1], ((0, LANE - VRES_RANK), (0, 0)))
        vres = (row(P['rwkv_v0'][l - 1]), wva, wvb)
    return dict(
        w_in=w_in_pad, pre_g=row(P['pre_norm_g'][l]), post_g=row(P['post_norm_g'][l]),
        w_out=P['w_out'][l].astype(BF16),
        q_g=row(P['mla_q_norm_g'][l]), kv_g=row(P['mla_kv_norm_g'][l]),
        wq=wq, wk=wk, wv=wv, wuk_ext=wuk_ext, w_uv=w_uv_h,
        rwkv_consts=rwkv_consts, vres=vres,
        gn_g=row(P['rwkv_gn_g'][l]), gn_b=row(P['rwkv_gn_b'][l]),
        hgrn_g=jnp.tile(P['hgrn_norm_g'][l], HGRN_HEADS)[None, :],
    )


def _rope_tables(pos, rows):
    half = MLA_ROPE // 2
    inv_freq = ROPE_BASE ** (-jnp.arange(half, dtype=F32) / half)
    ang = pos.astype(F32)[:, None] * inv_freq[None, :]
    cos, sin = jnp.cos(ang), jnp.sin(ang)
    t = pos.shape[0]
    ones, zeros = jnp.ones((t, PE_LO), F32), jnp.zeros((t, PE_LO), F32)
    pad = jnp.zeros((t, LANE - PE_LO - MLA_ROPE), F32)
    cos_t = jnp.concatenate([ones, cos, cos, pad], axis=1)
    sin_t = jnp.concatenate([zeros, -sin, sin, pad], axis=1)
    rep = rows // t
    return jnp.tile(cos_t, (rep, 1)), jnp.tile(sin_t, (rep, 1))


def _trunk(x, mods, pos, shift0, rwkv0, hgrn0, cache, LW, lb_all):
    batch, seq_len, _ = x.shape
    n = batch * seq_len
    x = x.reshape(n, D_MODEL)
    prep_tile = min(512, n)
    cos, sin = _rope_tables(pos, max(seq_len, prep_tile))
    v_first = None
    ckvs, kpes, rs, shs, hs = [], [], [], [], []
    for l in range(DEPTH):
        W = LW[l]
        mod = mods[l][:, None, :]
        rwkv_args = (shift0[l], W['rwkv_consts'], v_first, W['vres'])
        res = _in_proj(x, mod, W['pre_g'], W['w_in'], seq_len, rwkv=rwkv_args)
        if len(res) == 4:
            p_mla, p_rwkv, p_hgrn, gate = res
            if l == 0:
                feat, v_first = _rwkv_prep(p_rwkv, shift0[l], W['rwkv_consts'], None, None, seq_len)
            else:
                feat = _rwkv_prep(p_rwkv, shift0[l], W['rwkv_consts'], v_first, W['vres'], seq_len)
            shift_out = p_rwkv.reshape(batch, seq_len, RWKV_IN)[:, -1, :]
        else:
            p_mla, p_hgrn, gate, feat, v_first, shift_out = res
        if cache is None:
            q, k, v, ckv, kpe = _mla_prep(p_mla, cos, sin, W['q_g'], W['kv_g'], W['wq'], (W['wk'], W['wv']),
                                          seq_len, absorbed=False)
            o_mla = _attention_prompt(q, k, v, batch, seq_len)
        else:
            cache_ckv, cache_kpe, page_table = cache
            qx, ckv, kpe = _mla_prep(p_mla, cos, sin, W['q_g'], W['kv_g'], W['wq'], (W['wuk_ext'],),
                                     seq_len, absorbed=True)
            ext = qx.shape[1] // MLA_HEADS
            qx = qx.reshape(batch, seq_len, MLA_HEADS, ext).transpose(0, 2, 1, 3).reshape(batch, MLA_HEADS * seq_len, ext)
            o = _attention_paged(qx, ckv.reshape(batch, seq_len, MLA_KV_RANK), kpe.reshape(batch, seq_len, MLA_ROPE),
                                 W['w_uv'], cache_ckv, cache_kpe, page_table, l)
            o_mla = o.reshape(batch, MLA_HEADS, seq_len, MLA_V).transpose(0, 2, 1, 3).reshape(n, MLA_WIDTH)
        o_rwkv, s_r = _rwkv_scan(feat, rwkv0[l], W['gn_g'], W['gn_b'], seq_len)
        o_hgrn, s_h = _hgrn_scan(p_hgrn, lb_all[l][None, :], W['hgrn_g'], jnp.swapaxes(hgrn0[l], -1, -2), seq_len)
        s_h = jnp.swapaxes(s_h, -1, -2)
        x = _out_proj(o_mla, o_rwkv, o_hgrn, gate, x, mod, W['post_g'], W['w_out'], seq_len)
        ckvs.append(ckv.reshape(batch, seq_len, MLA_KV_RANK))
        kpes.append(kpe.reshape(batch, seq_len, MLA_ROPE))
        rs.append(s_r)
        shs.append(shift_out)
        hs.append(s_h)
    return (x.reshape(batch, seq_len, D_MODEL), jnp.stack(ckvs), jnp.stack(kpes), jnp.stack(rs), jnp.stack(shs),
            jnp.stack(hs))


def kernel(x_prompt, x_sample, c_prompt, c_sample, cache_ckv, cache_kpe, page_table, state_rwkv, state_rwkv_shift, state_hgrn, w_ada, b_ada, pre_norm_g, post_norm_g, w_in, mla_q_norm_g, mla_w_q_b, mla_kv_norm_g, mla_w_kv_b, rwkv_mu, rwkv_w0, rwkv_w_decay_b, rwkv_a0, rwkv_w_iclr_b, rwkv_k_k, rwkv_k_a, rwkv_r_k, rwkv_gn_g, rwkv_gn_b, rwkv_v0, rwkv_w_vres_a, rwkv_w_vres_b, hgrn_lb_raw, hgrn_norm_g, w_out):
    P = {
        'w_in': w_in, 'pre_norm_g': pre_norm_g, 'post_norm_g': post_norm_g,
        'mla_q_norm_g': mla_q_norm_g, 'mla_w_q_b': mla_w_q_b, 'mla_kv_norm_g': mla_kv_norm_g, 'mla_w_kv_b': mla_w_kv_b,
        'rwkv_mu': rwkv_mu, 'rwkv_w0': rwkv_w0, 'rwkv_w_decay_b': rwkv_w_decay_b, 'rwkv_a0': rwkv_a0,
        'rwkv_w_iclr_b': rwkv_w_iclr_b, 'rwkv_k_k': rwkv_k_k, 'rwkv_k_a': rwkv_k_a, 'rwkv_r_k': rwkv_r_k,
        'rwkv_gn_g': rwkv_gn_g, 'rwkv_gn_b': rwkv_gn_b, 'rwkv_v0': rwkv_v0, 'rwkv_w_vres_a': rwkv_w_vres_a,
        'rwkv_w_vres_b': rwkv_w_vres_b, 'hgrn_norm_g': hgrn_norm_g, 'w_out': w_out,
    }
    LW = [_layer_weights(P, l) for l in range(DEPTH)]
    lb_p = jax.nn.softmax(hgrn_lb_raw.astype(F32), axis=0)
    lb_all = jnp.cumsum(lb_p, axis=0) - lb_p[0]
    bp, tp = x_prompt.shape[:2]
    bs, ts = x_sample.shape[:2]
    mods = _ada(jnp.concatenate([c_prompt, c_sample], axis=0), w_ada, b_ada)
    past_len = page_table.shape[1] * PAGE_SIZE
    zeros = lambda *s: jnp.zeros(s, F32)
    out_p = _trunk(x_prompt, mods[:, :bp], jnp.arange(tp), zeros(DEPTH, bp, RWKV_IN),
                   zeros(DEPTH, bp, RWKV_HEADS, RWKV_HEAD, RWKV_HEAD), zeros(DEPTH, bp, HGRN_HEADS, HGRN_HEAD, HGRN_HEAD),
                   None, LW, lb_all)
    out_s = _trunk(x_sample, mods[:, bp:], past_len + jnp.arange(ts), state_rwkv_shift, state_rwkv, state_hgrn,
                   (cache_ckv, cache_kpe, page_table), LW, lb_all)
    return (out_p[0], out_s[0]) + out_p[1:] + out_s[1:]
```

```python
import functools
import math

import jax
import jax.numpy as jnp
from jax import lax
from jax.experimental import pallas as pl
from jax.experimental.pallas import tpu as pltpu

F32 = jnp.float32
BF16 = jnp.bfloat16

D_MODEL = 1024
DEPTH = 2
NORM_EPS = 1e-6
MASK_VALUE = -1e30
MIN_FORGET = 1e-20
PAGE_SIZE = 128
MLA_HEADS = 8
MLA_NOPE = 64
MLA_ROPE = 32
MLA_V = 64
MLA_Q_RANK = 384
MLA_KV_RANK = 256
MLA_WIDTH = MLA_HEADS * MLA_V
MLA_SCALE = (MLA_NOPE + MLA_ROPE) ** -0.5
ROPE_BASE = 10000.0
RWKV_WIDTH = 256
RWKV_HEAD = 64
RWKV_HEADS = 4
DECAY_RANK = 64
ICLR_RANK = 64
VRES_RANK = 32
RWKV_GN_EPS = 64e-5
RWKV_IN = 3 * RWKV_WIDTH + DECAY_RANK + ICLR_RANK
RWKV_CHUNK = 64
HGRN_WIDTH = 256
HGRN_HEADS = 4
HGRN_HEAD = 64
HGRN_IN = 3 * HGRN_WIDTH
HGRN_SUB = 16
LANE = 128
MLA_IN_PAD = MLA_Q_RANK + MLA_KV_RANK + LANE
QK_PAD = LANE
PE_LO = MLA_NOPE
IN_PAD = MLA_IN_PAD + RWKV_IN + HGRN_IN + D_MODEL
RWKV_FEAT = 7 * RWKV_WIDTH
VMEM_LIMIT = 56 * 1024 * 1024

NN = (((1,), (0,)), ((), ()))
NT = (((1,), (1,)), ((), ()))
TN = (((0,), (0,)), ((), ()))


def _bdot(a, b, dims=NN):
    return lax.dot_general(a.astype(BF16), b.astype(BF16), dims, preferred_element_type=F32)


def _split(a):
    hi = a.astype(BF16)
    lo = (a - hi.astype(F32)).astype(BF16)
    return hi, lo


def _dot3(a, b, dims=NN):
    ah, al = _split(a)
    bh, bl = _split(b)
    d = lambda x, y: lax.dot_general(x, y, dims, preferred_element_type=F32)
    return d(ah, bh) + (d(ah, bl) + d(al, bh))


def _dot_exact_lhs(m, x):
    x1 = x.astype(BF16)
    r1 = x - x1.astype(F32)
    x2 = r1.astype(BF16)
    x3 = (r1 - x2.astype(F32)).astype(BF16)
    d = lambda y: lax.dot_general(m, y, NN, preferred_element_type=F32)
    return d(x1) + (d(x2) + d(x3))


def _silu(x):
    return x * jax.nn.sigmoid(x)


def _rms(x, g):
    return x * lax.rsqrt(jnp.mean(x * x, axis=-1, keepdims=True) + NORM_EPS) * g


def _rows(m, ns, seg):
    w = m.shape[-1]
    return jnp.broadcast_to(m, (ns, seg, w)).reshape(ns * seg, w)


def _head_ones(width, head):
    r = lax.broadcasted_iota(jnp.int32, (width, width), 0) // head
    c = lax.broadcasted_iota(jnp.int32, (width, width), 1) // head
    return (r == c).astype(BF16)


def _mixer_dtype(seq_len):
    return BF16 if seq_len % 16 == 0 else F32


def _cparams(sem):
    return pltpu.CompilerParams(dimension_semantics=sem, vmem_limit_bytes=VMEM_LIMIT)


def _seg_tiles(n_rows, seq_len, tile):
    tile = min(tile, n_rows)
    seg = min(seq_len, tile)
    ns = tile // seg
    assert ns * seg == tile and n_rows % tile == 0 and seq_len % seg == 0
    return tile, ns, seg


def _ada_kernel(c_ref, w_ref, b_ref, o_ref):
    o_ref[0] = _dot3(_silu(c_ref[...]), w_ref[0]) + b_ref[0]


def _ada(c_all, w_ada, b_ada):
    bt = c_all.shape[0]
    tn = 768
    return pl.pallas_call(
        _ada_kernel,
        out_shape=jax.ShapeDtypeStruct((DEPTH, bt, 3 * D_MODEL), F32),
        grid=(DEPTH, 3 * D_MODEL // tn),
        in_specs=[
            pl.BlockSpec((bt, D_MODEL), lambda l, j: (0, 0)),
            pl.BlockSpec((1, D_MODEL, tn), lambda l, j: (l, 0, j)),
            pl.BlockSpec((1, 1, tn), lambda l, j: (l, 0, j)),
        ],
        out_specs=pl.BlockSpec((1, bt, tn), lambda l, j: (l, 0, j)),
        compiler_params=_cparams(("arbitrary", "arbitrary")),
        name="ada",
    )(c_all, w_ada, b_ada.reshape(DEPTH, 1, 3 * D_MODEL))


IN_COLS = (0, MLA_IN_PAD, MLA_IN_PAD + RWKV_IN, MLA_IN_PAD + RWKV_IN + HGRN_IN, IN_PAD)


def _in_proj_kernel(x_ref, mod_ref, g_ref, w_ref, *rest, ns, seg, per_seq, first):
    mod = mod_ref[...]
    shift = _rows(mod[:, :, 0:D_MODEL], ns, seg)
    scale = _rows(mod[:, :, D_MODEL:2 * D_MODEL], ns, seg)
    h = (_rms(x_ref[...], g_ref[...]) * (1.0 + scale) + shift).astype(BF16)
    proj = lambda j: lax.dot_general(h, w_ref[:, IN_COLS[j]:IN_COLS[j + 1]], NN, preferred_element_type=F32)
    if not per_seq:
        for j, o_ref in enumerate(rest):
            o_ref[...] = proj(j).astype(o_ref.dtype)
        return
    shift_ref, *rest = rest
    vres = None
    if not first:
        vres, rest = rest[:4], rest[4:]
    consts, (o_mla, o_hgrn, o_gate, feat_ref), rest = rest[:6], rest[6:10], rest[10:]
    if first:
        vfirst_out, *rest = rest
    tail_out, prev_s = rest
    o_mla[...] = proj(0)
    o_hgrn[...] = proj(2)
    o_gate[...] = proj(3).astype(o_gate.dtype)
    p = proj(1)
    row = lax.broadcasted_iota(jnp.int32, p.shape, 0)
    before = jnp.where(pl.program_id(0) % per_seq == 0, shift_ref[0], prev_s[...])
    prev = jnp.where(row == 0, jnp.broadcast_to(before, p.shape), pltpu.roll(p, 1, axis=0))
    feats, v_raw = _rwkv_features(p, prev, consts, vres)
    if first:
        vfirst_out[...] = v_raw
    for j, t in enumerate(feats):
        feat_ref[:, j * RWKV_WIDTH:(j + 1) * RWKV_WIDTH] = t
    last = p[p.shape[0] - 1:, :]
    prev_s[...] = last
    tail_out[0] = last


def _in_proj(x, mod, g, w, seq_len, rwkv=None, tile=512):
    n = x.shape[0]
    tm, ns, seg = _seg_tiles(n, seq_len, tile)
    per_seq = seq_len // seg
    row = lambda wd: pl.BlockSpec((tm, wd), lambda i: (i, 0))
    full = lambda a: pl.BlockSpec(a.shape, lambda i: (0,) * a.ndim)
    ins = [x, mod, g, w]
    specs = [row(D_MODEL), pl.BlockSpec((ns, 1, 3 * D_MODEL), lambda i: (i // per_seq, 0, 0)), full(g), full(w)]
    if rwkv is None or ns != 1:
        outs = [(MLA_IN_PAD, F32), (RWKV_IN, F32), (HGRN_IN, F32), (D_MODEL, BF16)]
        return pl.pallas_call(
            functools.partial(_in_proj_kernel, ns=ns, seg=seg, per_seq=0, first=True),
            out_shape=[jax.ShapeDtypeStruct((n, wd), dt) for wd, dt in outs],
            grid=(n // tm,), in_specs=specs, out_specs=[row(wd) for wd, _ in outs],
            compiler_params=_cparams(("arbitrary",)), name="in_proj",
        )(*ins)
    shift_prev, consts, v_first, vres = rwkv
    first = v_first is None
    batch = n // seq_len
    per_batch = pl.BlockSpec((1, 1, RWKV_IN), lambda i: (i // per_seq, 0, 0))
    ins.append(shift_prev[:, None, :])
    specs.append(per_batch)
    if not first:
        ins += [v_first] + list(vres)
        specs += [row(RWKV_WIDTH)] + [full(a) for a in vres]
    ins += list(consts)
    specs += [full(a) for a in consts]
    outs = [(MLA_IN_PAD, F32), (HGRN_IN, F32), (D_MODEL, BF16), (RWKV_FEAT, F32)] + ([(RWKV_WIDTH, F32)] if first else [])
    res = pl.pallas_call(
        functools.partial(_in_proj_kernel, ns=ns, seg=seg, per_seq=per_seq, first=first),
        out_shape=[jax.ShapeDtypeStruct((n, wd), dt) for wd, dt in outs]
        + [jax.ShapeDtypeStruct((batch, 1, RWKV_IN), F32)],
        grid=(n // tm,), in_specs=specs, out_specs=[row(wd) for wd, _ in outs] + [per_batch],
        scratch_shapes=[pltpu.VMEM((1, RWKV_IN), F32)],
        compiler_params=_cparams(("arbitrary",)), name="in_proj_rwkv",
    )(*ins)
    p_mla, p_hgrn, gate, feat = res[:4]
    return p_mla, p_hgrn, gate, feat, (res[4] if first else v_first), res[-1][:, 0, :]


def _out_proj_kernel(om_ref, or_ref, oh_ref, gate_ref, x_ref, mod_ref, g_ref, w_ref, o_ref, *, ns, seg):
    sg = _silu(gate_ref[...].astype(F32))
    a0, a1 = MLA_WIDTH, MLA_WIDTH + RWKV_WIDTH
    acc = _bdot(om_ref[...] * sg[:, 0:a0], w_ref[0:a0, :])
    acc += _bdot(or_ref[...] * sg[:, a0:a1], w_ref[a0:a1, :])
    acc += _bdot(oh_ref[...] * sg[:, a1:], w_ref[a1:, :])
    a_gate = _rows(mod_ref[...][:, :, 2 * D_MODEL:], ns, seg)
    o_ref[...] = x_ref[...] + a_gate * _rms(acc, g_ref[...])


def _out_proj(o_mla, o_rwkv, o_hgrn, gate, x, mod, g, w, seq_len, tile=512):
    n = x.shape[0]
    tm, ns, seg = _seg_tiles(n, seq_len, tile)
    per_seq = seq_len // seg
    row = lambda wd: pl.BlockSpec((tm, wd), lambda i: (i, 0))
    return pl.pallas_call(
        functools.partial(_out_proj_kernel, ns=ns, seg=seg),
        out_shape=jax.ShapeDtypeStruct((n, D_MODEL), F32),
        grid=(n // tm,),
        in_specs=[
            row(MLA_WIDTH), row(RWKV_WIDTH), row(HGRN_WIDTH), row(D_MODEL), row(D_MODEL),
            pl.BlockSpec((ns, 1, 3 * D_MODEL), lambda i: (i // per_seq, 0, 0)),
            pl.BlockSpec((1, D_MODEL), lambda i: (0, 0)),
            pl.BlockSpec((D_MODEL, D_MODEL), lambda i: (0, 0)),
        ],
        out_specs=row(D_MODEL),
        compiler_params=_cparams(("arbitrary",)),
        name="out_proj",
    )(o_mla, o_rwkv, o_hgrn, gate, x, mod, g, w)


def _rope_tile(x, cos, sin):
    return x * cos + pltpu.roll(x, LANE - MLA_ROPE // 2, axis=1) * sin


def _mla_prep_kernel(p_ref, *refs, absorbed):
    _mla_stage(p_ref[...], *refs, absorbed=absorbed)


def _mla_stage(p, cos_ref, sin_ref, qg_ref, kvg_ref, wq_ref, *rest, absorbed):
    cos, sin = cos_ref[...], sin_ref[...]
    cq = _rms(p[:, 0:MLA_Q_RANK], qg_ref[...])
    ckv = _rms(p[:, MLA_Q_RANK:MLA_Q_RANK + MLA_KV_RANK], kvg_ref[...])
    kpe = _rope_tile(p[:, MLA_Q_RANK + MLA_KV_RANK:], cos, sin)
    q = _bdot(cq, wq_ref[...])
    if absorbed:
        wuk_ref, qx_out, ckv_out, kpe_out = rest
        ext = wuk_ref.shape[2]
        for h in range(MLA_HEADS):
            qh = _rope_tile(q[:, h * QK_PAD:(h + 1) * QK_PAD], cos, sin)
            qx_out[:, h * ext:(h + 1) * ext] = _bdot(qh, wuk_ref[h]).astype(BF16)
    else:
        wk_ref, wv_ref, q_out, k_out, v_out, ckv_out, kpe_out = rest
        kn = _bdot(ckv, wk_ref[...])
        for h in range(MLA_HEADS):
            sl = slice(h * QK_PAD, (h + 1) * QK_PAD)
            q_out[:, sl] = _rope_tile(q[:, sl], cos, sin).astype(BF16)
            k_out[:, sl] = (kn[:, sl] + kpe).astype(BF16)
        v = _bdot(ckv, wv_ref[...])
        col = lax.broadcasted_iota(jnp.int32, v.shape, 1)
        ones_col = jnp.where((col // QK_PAD) % 2 == 0, MLA_V, 0)
        v_out[...] = jnp.where(col % QK_PAD == ones_col, 1.0, v).astype(BF16)
    ckv_out[...] = ckv
    kpe_out[...] = kpe[:, PE_LO:PE_LO + MLA_ROPE]


def _mla_prep(p_mla, cos, sin, qg, kvg, wq, extra_w, seq_len, absorbed, tile=512):
    n = p_mla.shape[0]
    tm = min(tile, n)
    rep = max(seq_len // tm, 1)
    hw = MLA_HEADS * QK_PAD
    row = lambda wd: pl.BlockSpec((tm, wd), lambda i: (i, 0))
    full = lambda a: pl.BlockSpec(a.shape, lambda i: (0,) * a.ndim)
    tab = pl.BlockSpec((tm, LANE), lambda i: (i % rep, 0))
    if absorbed:
        ext = extra_w[0].shape[2]
        outs = [(MLA_HEADS * ext, BF16), (MLA_KV_RANK, F32), (MLA_ROPE, F32)]
    else:
        outs = [(hw, BF16), (hw, BF16), (hw, BF16), (MLA_KV_RANK, F32), (MLA_ROPE, F32)]
    return pl.pallas_call(
        functools.partial(_mla_prep_kernel, absorbed=absorbed),
        out_shape=[jax.ShapeDtypeStruct((n, wd), dt) for wd, dt in outs],
        grid=(n // tm,),
        in_specs=[row(MLA_IN_PAD), tab, tab, full(qg), full(kvg), full(wq)] + [full(a) for a in extra_w],
        out_specs=[row(wd) for wd, _ in outs],
        compiler_params=_cparams(("arbitrary",)),
        name="mla_prep_absorbed" if absorbed else "mla_prep",
    )(p_mla, cos, sin, qg, kvg, wq, *extra_w)


def _softmax_step(s, v, m, l, acc):
    c = MLA_SCALE * math.log2(math.e)
    m_new = jnp.maximum(m, jnp.max(s, axis=-1, keepdims=True))
    alpha = jnp.exp2((m - m_new) * c)
    p = jnp.exp2((s - m_new) * c)
    l = alpha * l + jnp.sum(p, axis=-1, keepdims=True)
    acc = alpha * acc + lax.dot_general(p.astype(BF16), v, NN, preferred_element_type=F32)
    return m_new, l, acc


def _softmax_steps(ss, vs, ms, accs):
    c = MLA_SCALE * math.log2(math.e)
    m_new = [jnp.maximum(m, jnp.max(s, axis=-1, keepdims=True)) for m, s in zip(ms, ss)]
    alpha = [jnp.exp2((m - mn) * c) for m, mn in zip(ms, m_new)]
    ps = [jnp.exp2((s - mn) * c).astype(BF16) for s, mn in zip(ss, m_new)]
    pv = [lax.dot_general(p, v, NN, preferred_element_type=F32) for p, v in zip(ps, vs)]
    accs = [a * acc + x for a, acc, x in zip(alpha, accs, pv)]
    return m_new, accs


def _attn_kernel(q_ref, k_ref, v_ref, o_ref, *, tq, heads):
    i = pl.program_id(2)
    sls = [slice(h * QK_PAD, (h + 1) * QK_PAD) for h in range(heads)]
    qs = [q_ref[:, sl] for sl in sls]

    def scores(j):
        rows = pl.ds(pl.multiple_of(j * tq, tq), tq)
        ss = [lax.dot_general(q, k_ref[rows, sl], NT, preferred_element_type=F32) for q, sl in zip(qs, sls)]
        return ss, [v_ref[rows, sl] for sl in sls]

    def body(j, carry):
        ss, vs = scores(j)
        return _softmax_steps(ss, vs, *carry)

    init = ([jnp.full((tq, 1), MASK_VALUE, F32)] * heads, [jnp.zeros((tq, LANE), F32)] * heads)
    carry = lax.fori_loop(0, i, body, init)
    ss, vs = scores(i)
    r = lax.broadcasted_iota(jnp.int32, (tq, tq), 0)
    c = lax.broadcasted_iota(jnp.int32, (tq, tq), 1)
    _, accs = _softmax_steps([jnp.where(c <= r, s, MASK_VALUE) for s in ss], vs, *carry)
    low = lax.broadcasted_iota(jnp.int32, (tq, LANE), 1) < MLA_V
    for hp in range(heads // 2):
        even, odd = accs[2 * hp], accs[2 * hp + 1]
        o_pair = jnp.where(low, even / even[:, MLA_V:MLA_V + 1], odd / odd[:, 0:1])
        o_ref[:, hp * LANE:(hp + 1) * LANE] = o_pair.astype(o_ref.dtype)


def _attention_prompt(q, k, v, batch, seq_len, tq=512, heads=4):
    n = q.shape[0]
    tq = min(tq, seq_len)
    nq = seq_len // tq
    wide = heads * QK_PAD
    return pl.pallas_call(
        functools.partial(_attn_kernel, tq=tq, heads=heads),
        out_shape=jax.ShapeDtypeStruct((n, MLA_WIDTH), _mixer_dtype(seq_len)),
        grid=(batch, MLA_HEADS // heads, nq),
        in_specs=[
            pl.BlockSpec((tq, wide), lambda b, h, i: (b * nq + i, h)),
            pl.BlockSpec((seq_len, wide), lambda b, h, i: (b, h)),
            pl.BlockSpec((seq_len, wide), lambda b, h, i: (b, h)),
        ],
        out_specs=pl.BlockSpec((tq, wide // 2), lambda b, h, i: (b * nq + i, h)),
        compiler_params=_cparams(("arbitrary", "arbitrary", "arbitrary")),
        name="attn_prompt",
    )(q, k, v)


def _attn_paged_kernel(pt_ref, q_ref, ckv_new_ref, kpe_new_ref, wuv_ref, ckv_hbm, kpe_hbm, o_ref,
                       ckv_buf, kpe_buf, sem, *, pages, group, lookahead, t_new, layer):
    b = pl.program_id(0)
    slot = b % 2
    rows = MLA_HEADS * t_new

    def page_copies(elem, slot_idx, lookup):
        out = []
        for pg in range(pages):
            page = pt_ref[elem * pages + pg] if lookup else 0
            out.append(pltpu.make_async_copy(ckv_hbm.at[layer, page], ckv_buf.at[slot_idx, pg], sem.at[slot_idx, 0]))
            out.append(pltpu.make_async_copy(kpe_hbm.at[layer, page], kpe_buf.at[slot_idx, pg], sem.at[slot_idx, 1]))
        return out

    @pl.when(b == 0)
    def _():
        for cp in page_copies(0, 0, True):
            cp.start()

    @pl.when(b + 1 < pl.num_programs(0))
    def _():
        for cp in page_copies(b + 1, 1 - slot, True):
            cp.start()

    for cp in page_copies(b, slot, False):
        cp.wait()

    q = q_ref[0]
    q_lat = q[:, 0:MLA_KV_RANK]
    q_pe = q[:, MLA_KV_RANK:MLA_KV_RANK + MLA_ROPE]
    def scores(n):
        pg = n * group
        ckv = jnp.concatenate([ckv_buf[slot, pg + i].astype(BF16) for i in range(group)], axis=0)
        kpe_t = jnp.concatenate([kpe_buf[slot, pg + i].astype(BF16) for i in range(group)], axis=1)
        return (lax.dot_general(q_lat, ckv, NT, preferred_element_type=F32)
                + lax.dot_general(q_pe, kpe_t, NN, preferred_element_type=F32)), ckv

    ckv = ckv_new_ref[0].astype(BF16)
    s = lax.dot_general(q_lat, ckv, NT, preferred_element_type=F32)
    s += lax.dot_general(q_pe, kpe_new_ref[0].astype(BF16), NT, preferred_element_type=F32)
    t_q = lax.broadcasted_iota(jnp.int32, (rows, t_new), 0) % t_new
    t_k = lax.broadcasted_iota(jnp.int32, (rows, t_new), 1)
    n_chunks = pages // group
    ahead = [scores(n) for n in range(min(lookahead, n_chunks))]
    m, l, acc = _softmax_step(jnp.where(t_k <= t_q, s, MASK_VALUE), ckv, jnp.full((rows, 1), MASK_VALUE, F32),
                              jnp.zeros((rows, 1), F32), jnp.zeros((rows, MLA_KV_RANK), F32))
    for n in range(n_chunks):
        s, ckv = ahead.pop(0)
        if n + lookahead < n_chunks:
            ahead.append(scores(n + lookahead))
        m, l, acc = _softmax_step(s, ckv, m, l, acc)
    lat = acc / l
    for h in range(MLA_HEADS):
        o_ref[0, h * t_new:(h + 1) * t_new, :] = _bdot(lat[h * t_new:(h + 1) * t_new, :], wuv_ref[h])


def _attention_paged(q_ext, ckv_new, kpe_new, w_uv, cache_ckv, cache_kpe, page_table, layer, group=16, lookahead=4):
    batch, rows, ext = q_ext.shape
    t_new = rows // MLA_HEADS
    pages = page_table.shape[1]
    group = min(group, pages)
    assert pages % group == 0
    kpe_t = jnp.swapaxes(cache_kpe, 2, 3)
    grid_spec = pltpu.PrefetchScalarGridSpec(
        num_scalar_prefetch=1,
        grid=(batch,),
        in_specs=[
            pl.BlockSpec((1, rows, ext), lambda b, pt: (b, 0, 0)),
            pl.BlockSpec((1, t_new, MLA_KV_RANK), lambda b, pt: (b, 0, 0)),
            pl.BlockSpec((1, t_new, MLA_ROPE), lambda b, pt: (b, 0, 0)),
            pl.BlockSpec(w_uv.shape, lambda b, pt: (0, 0, 0)),
            pl.BlockSpec(memory_space=pl.ANY),
            pl.BlockSpec(memory_space=pl.ANY),
        ],
        out_specs=pl.BlockSpec((1, rows, MLA_V), lambda b, pt: (b, 0, 0)),
        scratch_shapes=[
            pltpu.VMEM((2, pages, PAGE_SIZE, MLA_KV_RANK), F32),
            pltpu.VMEM((2, pages, MLA_ROPE, PAGE_SIZE), F32),
            pltpu.SemaphoreType.DMA((2, 2)),
        ],
    )
    return pl.pallas_call(
        functools.partial(_attn_paged_kernel, pages=pages, group=group, lookahead=lookahead, t_new=t_new,
                          layer=layer),
        out_shape=jax.ShapeDtypeStruct((batch, rows, MLA_V), F32),
        grid_spec=grid_spec,
        compiler_params=_cparams(("arbitrary",)),
        name="attn_paged",
    )(page_table.reshape(-1), q_ext, ckv_new, kpe_new, w_uv, cache_ckv, kpe_t)


def _rwkv_prep_kernel(p_ref, shift_ref, *rest, ns, seg, per_seq, first):
    if per_seq:
        tail_ref, *rest = rest
    mu_ref, w2_ref, wa0_ref, kk_ref, ka_ref, rk_ref, *rest = rest
    if first:
        feat_ref, vfirst_out = rest
    else:
        vfirst_ref, v0_ref, wva_ref, wvb_ref, feat_ref = rest
    p = p_ref[...]
    row = lax.broadcasted_iota(jnp.int32, p.shape, 0)
    before = shift_ref[...]
    if per_seq:
        before = jnp.where(pl.program_id(0) % per_seq == 0, before, tail_ref[7:8, :][None])
    prev = jnp.where(row % seg == 0, _rows(before, ns, seg), pltpu.roll(p, 1, axis=0))
    consts = (mu_ref, w2_ref, wa0_ref, kk_ref, ka_ref, rk_ref)
    vres = None if first else (vfirst_ref, v0_ref, wva_ref, wvb_ref)
    feats, v_raw = _rwkv_features(p, prev, consts, vres)
    if first:
        vfirst_out[...] = v_raw
    for j, t in enumerate(feats):
        feat_ref[:, j * RWKV_WIDTH:(j + 1) * RWKV_WIDTH] = t


def _rwkv_features(p, prev, consts, vres):
    mu_ref, w2_ref, wa0_ref, kk_ref, ka_ref, rk_ref = consts
    w = RWKV_WIDTH
    pm = p + mu_ref[...] * (prev - p)
    r, k, v, xwa = pm[:, 0:w], pm[:, w:2 * w], pm[:, 2 * w:3 * w], pm[:, 3 * w:]
    lane = lax.broadcasted_iota(jnp.int32, xwa.shape, 1)
    da = _bdot(jnp.where(lane < DECAY_RANK, jnp.tanh(xwa), xwa), w2_ref[...]) + wa0_ref[...]
    z = -da[:, 0:w]
    softplus = jnp.maximum(z, 0.0) + jnp.log1p(jnp.exp(-jnp.abs(z)))
    log_decay = -jnp.exp(-softplus - 0.5)
    a = jax.nn.sigmoid(da[:, w:])
    v_raw = v
    if vres is not None:
        vfirst_ref, v0_ref, wva_ref, wvb_ref = vres
        mix = jax.nn.sigmoid(v0_ref[...] + _bdot(_bdot(v, wva_ref[...]), wvb_ref[...]))
        v = v + (vfirst_ref[...] - v) * mix
    kk = k * kk_ref[...]
    k = k * (1.0 + (a - 1.0) * ka_ref[...])
    ones = _head_ones(w, RWKV_HEAD)
    kk = kk / jnp.maximum(jnp.sqrt(_bdot(kk * kk, ones)), 1e-12)
    bonus = _bdot(r * k * rk_ref[...], ones) * v
    return (r, log_decay, k, v, -kk, kk * a, bonus), v_raw


def _rwkv_prep(p_rwkv, shift_prev, consts, v_first, vres, seq_len, tile=256):
    n = p_rwkv.shape[0]
    tm, ns, seg = _seg_tiles(n, seq_len, tile)
    per_seq = seq_len // seg if ns == 1 else 0
    assert per_seq or seg == seq_len
    first = v_first is None
    row = lambda wd: pl.BlockSpec((tm, wd), lambda i: (i, 0))
    full = lambda a: pl.BlockSpec(a.shape, lambda i: (0,) * a.ndim)
    ins = [p_rwkv, shift_prev[:, None, :]]
    specs = [row(RWKV_IN), pl.BlockSpec((ns, 1, RWKV_IN), lambda i: (i // max(per_seq, 1), 0, 0))]
    if per_seq:
        sub_tiles = tm // 8
        ins.append(p_rwkv)
        specs.append(pl.BlockSpec((8, RWKV_IN), lambda i: (jnp.maximum(i * sub_tiles - 1, 0), 0)))
    ins += list(consts)
    specs += [full(a) for a in consts]
    feat = jax.ShapeDtypeStruct((n, RWKV_FEAT), F32)
    if first:
        out_shape = [feat, jax.ShapeDtypeStruct((n, RWKV_WIDTH), F32)]
        out_specs = [row(RWKV_FEAT), row(RWKV_WIDTH)]
    else:
        ins += [v_first] + list(vres)
        specs += [row(RWKV_WIDTH)] + [full(a) for a in vres]
        out_shape, out_specs = feat, row(RWKV_FEAT)
    return pl.pallas_call(
        functools.partial(_rwkv_prep_kernel, ns=ns, seg=seg, per_seq=per_seq, first=first),
        out_shape=out_shape,
        grid=(n // tm,),
        in_specs=specs,
        out_specs=out_specs,
        compiler_params=_cparams(("arbitrary",)),
        name="rwkv_prep",
    )(*ins)


def _rwkv_chunk_heads(r, lw_cum, lw, k, v, a, b, s0):
    n = len(r)
    c = r[0].shape[0]
    mm = _bdot
    each = lambda f, *xs: [f(*t) for t in zip(*xs)]
    ar = each(lambda a_, r_, cum, lw_: jnp.concatenate([a_ * jnp.exp(cum - lw_), r_ * jnp.exp(cum)], axis=0),
              a, r, lw_cum, lw)
    bk = each(lambda b_, k_, cum: jnp.concatenate([b_ * jnp.exp(-cum), k_ * jnp.exp(-cum)], axis=0),
              b, k, lw_cum)
    g = each(lambda x, y: mm(x, y, NT), ar, bk)
    gr = lax.broadcasted_iota(jnp.int32, (2 * c, 2 * c), 0)
    gc = lax.broadcasted_iota(jnp.int32, (2 * c, 2 * c), 1) % c
    keep = gc < gr - jnp.where(gr < c, 0, c - 1)
    g = [jnp.where(keep, x, 0.0) for x in g]
    a_ab = [x[0:c, 0:c] for x in g]
    eye = (lax.broadcasted_iota(jnp.int32, (c, c), 0) == lax.broadcasted_iota(jnp.int32, (c, c), 1)).astype(F32)
    t_inv = [eye + x for x in a_ab]
    x = each(mm, a_ab, a_ab)
    for _ in range(int(math.log2(c)) - 2):
        tx = each(lambda t_, x_: mm(jnp.concatenate([t_, x_], axis=0), x_), t_inv, x)
        t_inv = each(lambda t_, tx_: t_ + tx_[0:c, :], t_inv, tx)
        x = [tx_[c:, :] for tx_ in tx]
    t_inv = each(lambda t_, x_: t_ + mm(t_, x_), t_inv, x)
    ar_s0 = each(lambda x, y: mm(x, y, NT), ar, s0)
    g_v = each(lambda g_, v_: mm(g_[0:c, :], jnp.concatenate([jnp.zeros_like(v_), v_], axis=0)), g, v)
    u = each(lambda t_, x, y: mm(t_, x[0:c, :] + y), t_inv, ar_s0, g_v)
    uv = each(lambda u_, v_: jnp.concatenate([u_, v_], axis=0), u, v)
    o = each(lambda x, g_, uv_: x[c:, :] + mm(g_[c:, :], uv_), ar_s0, g, uv)
    bk_end = each(lambda b_, k_, cum: jnp.concatenate(
        [b_ * jnp.exp(cum[c - 1:c, :] - cum), k_ * jnp.exp(cum[c - 1:c, :] - cum)], axis=0), b, k, lw_cum)
    s1 = each(lambda s_, cum, uv_, e_: s_ * jnp.exp(cum[c - 1:c, :]) + mm(uv_, e_, TN), s0, lw_cum, uv, bk_end)
    assert len(o) == n
    return o, s1


def _rwkv_scan_kernel(feat_ref, s0_ref, gn_g_ref, gn_b_ref, o_ref, s_out_ref, s_ref, *, ns, chunk):
    j = pl.program_id(1)
    w = RWKV_WIDTH

    @pl.when(j == 0)
    def _():
        s_ref[...] = s0_ref[...]

    tri = (lax.broadcasted_iota(jnp.int32, (chunk, chunk), 1)
           <= lax.broadcasted_iota(jnp.int32, (chunk, chunk), 0)).astype(BF16)

    feats = [feat_ref[q] for q in range(ns)]
    cums = [_dot_exact_lhs(tri, f[:, w:2 * w]) for f in feats]
    inst = [(q, h) for q in range(ns) for h in range(RWKV_HEADS)]
    col = lambda i, h: slice(i * w + h * RWKV_HEAD, i * w + (h + 1) * RWKV_HEAD)
    part = lambda i: [feats[q][:, col(i, h)] for q, h in inst]
    o, s1 = _rwkv_chunk_heads(part(0), [cums[q][:, col(0, h)] for q, h in inst], part(1), part(2), part(3), part(4),
                              part(5), [s_ref[q, h] for q, h in inst])
    normed = []
    for (q, h), o_h, s_h in zip(inst, o, s1):
        s_ref[q, h] = s_h
        mean = jnp.mean(o_h, axis=-1, keepdims=True)
        var = jnp.mean(jnp.square(o_h - mean), axis=-1, keepdims=True)
        normed.append((o_h - mean) * lax.rsqrt(var + RWKV_GN_EPS))
    for q in range(ns):
        o_q = jnp.concatenate(normed[q * RWKV_HEADS:(q + 1) * RWKV_HEADS], axis=1)
        o_ref[q] = (o_q * gn_g_ref[...] + gn_b_ref[...] + feats[q][:, 6 * w:]).astype(o_ref.dtype)

    @pl.when(j == pl.num_programs(1) - 1)
    def _():
        s_out_ref[...] = s_ref[...]


def _rwkv_scan(feat, s0, gn_g, gn_b, seq_len, ns=8):
    n = feat.shape[0]
    batch = n // seq_len
    chunk = min(RWKV_CHUNK, seq_len)
    nc = seq_len // chunk
    ns = min(ns, batch)
    assert batch % ns == 0 and seq_len % chunk == 0
    st = (ns, RWKV_HEADS, RWKV_HEAD, RWKV_HEAD)
    o, s1 = pl.pallas_call(
        functools.partial(_rwkv_scan_kernel, ns=ns, chunk=chunk),
        out_shape=[jax.ShapeDtypeStruct((batch, seq_len, RWKV_WIDTH), _mixer_dtype(seq_len)),
                   jax.ShapeDtypeStruct(s0.shape, F32)],
        grid=(batch // ns, nc),
        in_specs=[
            pl.BlockSpec((ns, chunk, RWKV_FEAT), lambda b, j: (b, j, 0)),
            pl.BlockSpec(st, lambda b, j: (b, 0, 0, 0)),
            pl.BlockSpec((1, RWKV_WIDTH), lambda b, j: (0, 0)),
            pl.BlockSpec((1, RWKV_WIDTH), lambda b, j: (0, 0)),
        ],
        out_specs=[
            pl.BlockSpec((ns, chunk, RWKV_WIDTH), lambda b, j: (b, j, 0)),
            pl.BlockSpec(st, lambda b, j: (b, 0, 0, 0)),
        ],
        scratch_shapes=[pltpu.VMEM(st, F32)],
        compiler_params=_cparams(("arbitrary", "arbitrary")),
        name="rwkv_scan",
    )(feat.reshape(batch, seq_len, RWKV_FEAT), s0, gn_g, gn_b)
    return o.reshape(n, RWKV_WIDTH), s1


def _hgrn_scan_kernel(p_ref, lb_ref, ng_ref, s0_ref, o_ref, s_out_ref, s_ref, q_s, bk_s, v_s, b_s, *, ns, nsub, sub):
    j = pl.program_id(1)
    w = HGRN_WIDTH
    tile = ns * nsub * sub

    @pl.when(j == 0)
    def _():
        s_ref[...] = s0_ref[...]

    p = p_ref[...]
    lb = lb_ref[...]
    sig = jax.nn.sigmoid(p[:, w:2 * w])
    q_s[...] = _silu(p[:, 0:w]) * HGRN_HEAD ** -0.5
    k = (1.0 - lb) * (1.0 - sig)
    v_s[...] = p[:, 2 * w:]
    log_f = jnp.log(jnp.maximum(lb + (1.0 - lb) * sig, MIN_FORGET))
    tr = lax.broadcasted_iota(jnp.int32, (tile, tile), 0)
    tc = lax.broadcasted_iota(jnp.int32, (tile, tile), 1)
    tri = ((tr // sub == tc // sub) & (tc <= tr)).astype(BF16)
    b = _dot_exact_lhs(tri, log_f) * math.log2(math.e)
    b_s[...] = b
    bk_s[...] = b - jnp.log2(k)
    ones = _head_ones(w, HGRN_HEAD)
    sub_row = lax.broadcasted_iota(jnp.int32, (sub, w), 0)
    nsc = ns * nsub
    blk = lambda ref, c: ref[c * sub:(c + 1) * sub, :]
    row = lambda ref, r: ref[r:r + 1, :]
    heads = [slice(h * HGRN_HEAD, (h + 1) * HGRN_HEAD) for h in range(HGRN_HEADS)]
    pair, spans = [], []
    for c in range(nsc):
        for s in range(sub):
            t0 = c * sub + (s // 8) * 8
            t1 = (c + 1) * sub
            pair.append(q_s[t0:t1, :] * jnp.exp2(jnp.minimum(b_s[t0:t1, :] - row(bk_s, c * sub + s), 0.0)))
            spans.append(t1 - t0)
    att = _bdot(jnp.concatenate(pair, axis=0), ones)
    intra, off = [], 0
    for c in range(nsc):
        parts = [jnp.zeros((8, w), F32)] * (sub // 8)
        for s in range(sub):
            n = spans[c * sub + s]
            t_row = lax.broadcasted_iota(jnp.int32, (n, w), 0) + (sub - n)
            a = jnp.where(t_row >= s, att[off:off + n, :], 0.0) * row(v_s, c * sub + s)
            g0 = (sub - n) // 8
            for g in range(g0, sub // 8):
                parts[g] = parts[g] + a[(g - g0) * 8:(g - g0 + 1) * 8, :]
            off += n
        intra.append(parts[0] if len(parts) == 1 else jnp.concatenate(parts, axis=0))
    last = [row(b_s, (c + 1) * sub - 1) for c in range(nsc)]
    kv = [[_bdot(blk(v_s, c)[:, hs], jnp.exp2(last[c] - blk(bk_s, c))[:, hs], TN) for hs in heads]
          for c in range(nsc)]
    states = []
    for seq in range(ns):
        st = [s_ref[seq, h] for h in range(HGRN_HEADS)]
        for c in range(seq * nsub, (seq + 1) * nsub):
            states.append(st)
            gamma = jnp.exp2(last[c])
            st = [st[h] * gamma[:, heads[h]] + kv[c][h] for h in range(HGRN_HEADS)]
        for h in range(HGRN_HEADS):
            s_ref[seq, h] = st[h]
    for c in range(nsc):
        q_in = blk(q_s, c) * jnp.exp2(blk(b_s, c))
        o = intra[c] + jnp.concatenate([_bdot(q_in[:, heads[h]], states[c][h], NT) for h in range(HGRN_HEADS)], axis=1)
        intra[c] = o
    o = jnp.concatenate(intra, axis=0)
    ms = _bdot(o * o, ones) * (1.0 / HGRN_HEAD)
    o_ref[...] = (o * lax.rsqrt(ms + NORM_EPS) * ng_ref[...]).astype(o_ref.dtype)

    @pl.when(j == pl.num_programs(1) - 1)
    def _():
        s_out_ref[...] = s_ref[...]


def _hgrn_scan(p_hgrn, lb, norm_g, s0, seq_len, tile=256, ns_max=16):
    n = p_hgrn.shape[0]
    batch = n // seq_len
    sub = min(HGRN_SUB, seq_len)
    if seq_len >= tile:
        ns, nsub = 1, tile // sub
    else:
        ns, nsub = min(ns_max, batch), seq_len // sub
    rows = ns * nsub * sub
    nt = seq_len // (nsub * sub)
    assert batch % ns == 0 and seq_len % (nsub * sub) == 0
    st = (ns, HGRN_HEADS, HGRN_HEAD, HGRN_HEAD)
    return pl.pallas_call(
        functools.partial(_hgrn_scan_kernel, ns=ns, nsub=nsub, sub=sub),
        out_shape=[jax.ShapeDtypeStruct((n, HGRN_WIDTH), _mixer_dtype(seq_len)), jax.ShapeDtypeStruct(s0.shape, F32)],
        grid=(batch // ns, nt),
        in_specs=[
            pl.BlockSpec((rows, HGRN_IN), lambda b, j: (b * nt + j, 0)),
            pl.BlockSpec((1, HGRN_WIDTH), lambda b, j: (0, 0)),
            pl.BlockSpec((1, HGRN_WIDTH), lambda b, j: (0, 0)),
            pl.BlockSpec(st, lambda b, j: (b, 0, 0, 0)),
        ],
        out_specs=[
            pl.BlockSpec((rows, HGRN_WIDTH), lambda b, j: (b * nt + j, 0)),
            pl.BlockSpec(st, lambda b, j: (b, 0, 0, 0)),
        ],
        scratch_shapes=[pltpu.VMEM(st, F32)] + [pltpu.VMEM((rows, HGRN_WIDTH), F32)] * 4,
        compiler_params=_cparams(("arbitrary", "arbitrary")),
        name="hgrn_scan",
    )(p_hgrn, lb, norm_g, s0)


def _layer_weights(P, l):
    w_in = P['w_in'][l]
    c_q, c_kv = MLA_Q_RANK, MLA_Q_RANK + MLA_KV_RANK
    c_pe = c_kv + MLA_ROPE
    half = MLA_ROPE // 2
    tail = QK_PAD - PE_LO - MLA_ROPE - half
    w_pe = w_in[:, c_kv:c_pe]
    kpe_tile = jnp.concatenate(
        [jnp.zeros((D_MODEL, PE_LO), F32), w_pe, w_pe[:, :half], jnp.zeros((D_MODEL, tail), F32)], axis=1)
    w_in_pad = jnp.concatenate([w_in[:, :c_kv], kpe_tile, w_in[:, c_pe:]], axis=1).astype(BF16)

    hd = MLA_NOPE + MLA_ROPE
    wq = P['mla_w_q_b'][l].reshape(MLA_Q_RANK, MLA_HEADS, hd)
    wq = jnp.concatenate([wq, wq[:, :, MLA_NOPE:MLA_NOPE + half], jnp.zeros((MLA_Q_RANK, MLA_HEADS, tail), F32)], axis=2)
    wq = wq.reshape(MLA_Q_RANK, MLA_HEADS * QK_PAD).astype(BF16)
    wkv = P['mla_w_kv_b'][l].reshape(MLA_KV_RANK, MLA_HEADS, MLA_NOPE + MLA_V)
    w_uk, w_uv = wkv[..., :MLA_NOPE], wkv[..., MLA_NOPE:]
    wk = jnp.pad(w_uk, ((0, 0), (0, 0), (0, QK_PAD - MLA_NOPE))).reshape(MLA_KV_RANK, MLA_HEADS * QK_PAD).astype(BF16)
    wv = jnp.zeros((MLA_KV_RANK, MLA_HEADS, QK_PAD), F32)
    for h in range(MLA_HEADS):
        off = (h % 2) * MLA_V
        wv = wv.at[:, h, off:off + MLA_V].set(w_uv[:, h, :])
    wv = wv.reshape(MLA_KV_RANK, MLA_HEADS * QK_PAD).astype(BF16)
    ext = MLA_KV_RANK + LANE
    wuk_ext = jnp.zeros((MLA_HEADS, QK_PAD, ext), F32)
    wuk_ext = wuk_ext.at[:, :MLA_NOPE, :MLA_KV_RANK].set(jnp.transpose(w_uk, (1, 2, 0)))
    wuk_ext = wuk_ext.at[:, PE_LO:PE_LO + MLA_ROPE, MLA_KV_RANK:MLA_KV_RANK + MLA_ROPE].set(
        jnp.broadcast_to(jnp.eye(MLA_ROPE, dtype=F32), (MLA_HEADS, MLA_ROPE, MLA_ROPE)))
    wuk_ext = wuk_ext.astype(BF16)
    w_uv_h = jnp.transpose(w_uv, (1, 0, 2)).astype(BF16)

    w = RWKV_WIDTH
    w2 = jnp.zeros((DECAY_RANK + ICLR_RANK, 2 * w), F32)
    w2 = w2.at[:DECAY_RANK, :w].set(P['rwkv_w_decay_b'][l]).at[DECAY_RANK:, w:].set(P['rwkv_w_iclr_b'][l])
    wa0 = jnp.concatenate([P['rwkv_w0'][l], P['rwkv_a0'][l]])[None, :]
    row = lambda a: a.reshape(1, -1)
    rwkv_consts = (row(P['rwkv_mu'][l]), w2, wa0, row(P['rwkv_k_k'][l]), row(P['rwkv_k_a'][l]), row(P['rwkv_r_k'][l]))
    vres = None
    if l > 0:
        wva = jnp.pad(P['rwkv_w_vres_a'][l - 1], ((0, 0), (0, LANE - VRES_RANK)))
        wvb = jnp.pad(P['rwkv_w_vres_b'][l - 1], ((0, LANE - VRES_RANK), (0, 0)))
        vres = (row(P['rwkv_v0'][l - 1]), wva, wvb)
    return dict(
        w_in=w_in_pad, pre_g=row(P['pre_norm_g'][l]), post_g=row(P['post_norm_g'][l]),
        w_out=P['w_out'][l].astype(BF16),
        q_g=row(P['mla_q_norm_g'][l]), kv_g=row(P['mla_kv_norm_g'][l]),
        wq=wq, wk=wk, wv=wv, wuk_ext=wuk_ext, w_uv=w_uv_h,
        rwkv_consts=rwkv_consts, vres=vres,
        gn_g=row(P['rwkv_gn_g'][l]), gn_b=row(P['rwkv_gn_b'][l]),
        hgrn_g=jnp.tile(P['hgrn_norm_g'][l], HGRN_HEADS)[None, :],
    )


def _rope_tables(pos, rows):
    half = MLA_ROPE // 2
    inv_freq = ROPE_BASE ** (-jnp.arange(half, dtype=F32) / half)
    ang = pos.astype(F32)[:, None] * inv_freq[None, :]
    cos, sin = jnp.cos(ang), jnp.sin(ang)
    t = pos.shape[0]
    ones, zeros = jnp.ones((t, PE_LO), F32), jnp.zeros((t, PE_LO), F32)
    pad = jnp.zeros((t, LANE - PE_LO - MLA_ROPE), F32)
    cos_t = jnp.concatenate([ones, cos, cos, pad], axis=1)
    sin_t = jnp.concatenate([zeros, -sin, sin, pad], axis=1)
    rep = rows // t
    return jnp.tile(cos_t, (rep, 1)), jnp.tile(sin_t, (rep, 1))


def _trunk(x, mods, pos, shift0, rwkv0, hgrn0, cache, LW, lb_all):
    batch, seq_len, _ = x.shape
    n = batch * seq_len
    x = x.reshape(n, D_MODEL)
    prep_tile = min(512, n)
    cos, sin = _rope_tables(pos, max(seq_len, prep_tile))
    v_first = None
    ckvs, kpes, rs, shs, hs = [], [], [], [], []
    for l in range(DEPTH):
        W = LW[l]
        mod = mods[l][:, None, :]
        rwkv_args = (shift0[l], W['rwkv_consts'], v_first, W['vres'])
        res = _in_proj(x, mod, W['pre_g'], W['w_in'], seq_len, rwkv=rwkv_args)
        if len(res) == 4:
            p_mla, p_rwkv, p_hgrn, gate = res
            if l == 0:
                feat, v_first = _rwkv_prep(p_rwkv, shift0[l], W['rwkv_consts'], None, None, seq_len)
            else:
                feat = _rwkv_prep(p_rwkv, shift0[l], W['rwkv_consts'], v_first, W['vres'], seq_len)
            shift_out = p_rwkv.reshape(batch, seq_len, RWKV_IN)[:, -1, :]
        else:
            p_mla, p_hgrn, gate, feat, v_first, shift_out = res
        if cache is None:
            q, k, v, ckv, kpe = _mla_prep(p_mla, cos, sin, W['q_g'], W['kv_g'], W['wq'], (W['wk'], W['wv']),
                                          seq_len, absorbed=False)
            o_mla = _attention_prompt(q, k, v, batch, seq_len)
        else:
            cache_ckv, cache_kpe, page_table = cache
            qx, ckv, kpe = _mla_prep(p_mla, cos, sin, W['q_g'], W['kv_g'], W['wq'], (W['wuk_ext'],),
                                     seq_len, absorbed=True)
            ext = qx.shape[1] // MLA_HEADS
            qx = qx.reshape(batch, seq_len, MLA_HEADS, ext).transpose(0, 2, 1, 3).reshape(batch, MLA_HEADS * seq_len, ext)
            o = _attention_paged(qx, ckv.reshape(batch, seq_len, MLA_KV_RANK), kpe.reshape(batch, seq_len, MLA_ROPE),
                                 W['w_uv'], cache_ckv, cache_kpe, page_table, l)
            o_mla = o.reshape(batch, MLA_HEADS, seq_len, MLA_V).transpose(0, 2, 1, 3).reshape(n, MLA_WIDTH)
        o_rwkv, s_r = _rwkv_scan(feat, rwkv0[l], W['gn_g'], W['gn_b'], seq_len)
        o_hgrn, s_h = _hgrn_scan(p_hgrn, lb_all[l][None, :], W['hgrn_g'], jnp.swapaxes(hgrn0[l], -1, -2), seq_len)
        s_h = jnp.swapaxes(s_h, -1, -2)
        x = _out_proj(o_mla, o_rwkv, o_hgrn, gate, x, mod, W['post_g'], W['w_out'], seq_len)
        ckvs.append(ckv.reshape(batch, seq_len, MLA_KV_RANK))
        kpes.append(kpe.reshape(batch, seq_len, MLA_ROPE))
        rs.append(s_r)
        shs.append(shift_out)
        hs.append(s_h)
    return (x.reshape(batch, seq_len, D_MODEL), jnp.stack(ckvs), jnp.stack(kpes), jnp.stack(rs), jnp.stack(shs),
            jnp.stack(hs))


def kernel(x_prompt, x_sample, c_prompt, c_sample, cache_ckv, cache_kpe, page_table, state_rwkv, state_rwkv_shift, state_hgrn, w_ada, b_ada, pre_norm_g, post_norm_g, w_in, mla_q_norm_g, mla_w_q_b, mla_kv_norm_g, mla_w_kv_b, rwkv_mu, rwkv_w0, rwkv_w_decay_b, rwkv_a0, rwkv_w_iclr_b, rwkv_k_k, rwkv_k_a, rwkv_r_k, rwkv_gn_g, rwkv_gn_b, rwkv_v0, rwkv_w_vres_a, rwkv_w_vres_b, hgrn_lb_raw, hgrn_norm_g, w_out):
    P = {
        'w_in': w_in, 'pre_norm_g': pre_norm_g, 'post_norm_g': post_norm_g,
        'mla_q_norm_g': mla_q_norm_g, 'mla_w_q_b': mla_w_q_b, 'mla_kv_norm_g': mla_kv_norm_g, 'mla_w_kv_b': mla_w_kv_b,
        'rwkv_mu': rwkv_mu, 'rwkv_w0': rwkv_w0, 'rwkv_w_decay_b': rwkv_w_decay_b, 'rwkv_a0': rwkv_a0,
        'rwkv_w_iclr_b': rwkv_w_iclr_b, 'rwkv_k_k': rwkv_k_k, 'rwkv_k_a': rwkv_k_a, 'rwkv_r_k': rwkv_r_k,
        'rwkv_gn_g': rwkv_gn_g, 'rwkv_gn_b': rwkv_gn_b, 'rwkv_v0': rwkv_v0, 'rwkv_w_vres_a': rwkv_w_vres_a,
        'rwkv_w_vres_b': rwkv_w_vres_b, 'hgrn_norm_g': hgrn_norm_g, 'w_out': w_out,
    }
    LW = [_layer_weights(P, l) for l in range(DEPTH)]
    lb_p = jax.nn.softmax(hgrn_lb_raw.astype(F32), axis=0)
    lb_all = jnp.cumsum(lb_p, axis=0) - lb_p[0]
    bp, tp = x_prompt.shape[:2]
    bs, ts = x_sample.shape[:2]
    mods = _ada(jnp.concatenate([c_prompt, c_sample], axis=0), w_ada, b_ada)
    past_len = page_table.shape[1] * PAGE_SIZE
    zeros = lambda *s: jnp.zeros(s, F32)
    out_p = _trunk(x_prompt, mods[:, :bp], jnp.arange(tp), zeros(DEPTH, bp, RWKV_IN),
                   zeros(DEPTH, bp, RWKV_HEADS, RWKV_HEAD, RWKV_HEAD), zeros(DEPTH, bp, HGRN_HEADS, HGRN_HEAD, HGRN_HEAD),
                   None, LW, lb_all)
    out_s = _trunk(x_sample, mods[:, bp:], past_len + jnp.arange(ts), state_rwkv_shift, state_rwkv, state_hgrn,
                   (cache_ckv, cache_kpe, page_table), LW, lb_all)
    return (out_p[0], out_s[0]) + out_p[1:] + out_s[1:]
```

```python
import functools
import math

import jax
import jax.numpy as jnp
from jax import lax
from jax.experimental import pallas as pl
from jax.experimental.pallas import tpu as pltpu

F32 = jnp.float32
BF16 = jnp.bfloat16

D_MODEL = 1024
DEPTH = 2
NORM_EPS = 1e-6
MASK_VALUE = -1e30
MIN_FORGET = 1e-20
PAGE_SIZE = 128
MLA_HEADS = 8
MLA_NOPE = 64
MLA_ROPE = 32
MLA_V = 64
MLA_Q_RANK = 384
MLA_KV_RANK = 256
MLA_WIDTH = MLA_HEADS * MLA_V
MLA_SCALE = (MLA_NOPE + MLA_ROPE) ** -0.5
ROPE_BASE = 10000.0
RWKV_WIDTH = 256
RWKV_HEAD = 64
RWKV_HEADS = 4
DECAY_RANK = 64
ICLR_RANK = 64
VRES_RANK = 32
RWKV_GN_EPS = 64e-5
RWKV_IN = 3 * RWKV_WIDTH + DECAY_RANK + ICLR_RANK
RWKV_CHUNK = 64
HGRN_WIDTH = 256
HGRN_HEADS = 4
HGRN_HEAD = 64
HGRN_IN = 3 * HGRN_WIDTH
HGRN_SUB = 16
LANE = 128
MLA_IN_PAD = MLA_Q_RANK + MLA_KV_RANK + LANE
QK_PAD = LANE
PE_LO = MLA_NOPE
IN_PAD = MLA_IN_PAD + RWKV_IN + HGRN_IN + D_MODEL
RWKV_FEAT = 7 * RWKV_WIDTH
VMEM_LIMIT = 56 * 1024 * 1024

NN = (((1,), (0,)), ((), ()))
NT = (((1,), (1,)), ((), ()))
TN = (((0,), (0,)), ((), ()))


def _bdot(a, b, dims=NN):
    return lax.dot_general(a.astype(BF16), b.astype(BF16), dims, preferred_element_type=F32)


def _split(a):
    hi = a.astype(BF16)
    lo = (a - hi.astype(F32)).astype(BF16)
    return hi, lo


def _dot3(a, b, dims=NN):
    ah, al = _split(a)
    bh, bl = _split(b)
    d = lambda x, y: lax.dot_general(x, y, dims, preferred_element_type=F32)
    return d(ah, bh) + (d(ah, bl) + d(al, bh))


def _dot_exact_lhs(m, x):
    x1 = x.astype(BF16)
    r1 = x - x1.astype(F32)
    x2 = r1.astype(BF16)
    x3 = (r1 - x2.astype(F32)).astype(BF16)
    d = lambda y: lax.dot_general(m, y, NN, preferred_element_type=F32)
    return d(x1) + (d(x2) + d(x3))


def _silu(x):
    return x * jax.nn.sigmoid(x)


def _rms(x, g):
    return x * lax.rsqrt(jnp.mean(x * x, axis=-1, keepdims=True) + NORM_EPS) * g


def _rows(m, ns, seg):
    w = m.shape[-1]
    return jnp.broadcast_to(m, (ns, seg, w)).reshape(ns * seg, w)


def _head_ones(width, head):
    r = lax.broadcasted_iota(jnp.int32, (width, width), 0) // head
    c = lax.broadcasted_iota(jnp.int32, (width, width), 1) // head
    return (r == c).astype(BF16)


def _mixer_dtype(seq_len):
    return BF16 if seq_len % 16 == 0 else F32


def _cparams(sem):
    return pltpu.CompilerParams(dimension_semantics=sem, vmem_limit_bytes=VMEM_LIMIT)


def _seg_tiles(n_rows, seq_len, tile):
    tile = min(tile, n_rows)
    seg = min(seq_len, tile)
    ns = tile // seg
    assert ns * seg == tile and n_rows % tile == 0 and seq_len % seg == 0
    return tile, ns, seg


def _ada_kernel(c_ref, w_ref, b_ref, o_ref):
    o_ref[0] = _dot3(_silu(c_ref[...]), w_ref[0]) + b_ref[0]


def _ada(c_all, w_ada, b_ada):
    bt = c_all.shape[0]
    tn = 768
    return pl.pallas_call(
        _ada_kernel,
        out_shape=jax.ShapeDtypeStruct((DEPTH, bt, 3 * D_MODEL), F32),
        grid=(DEPTH, 3 * D_MODEL // tn),
        in_specs=[
            pl.BlockSpec((bt, D_MODEL), lambda l, j: (0, 0)),
            pl.BlockSpec((1, D_MODEL, tn), lambda l, j: (l, 0, j)),
            pl.BlockSpec((1, 1, tn), lambda l, j: (l, 0, j)),
        ],
        out_specs=pl.BlockSpec((1, bt, tn), lambda l, j: (l, 0, j)),
        compiler_params=_cparams(("arbitrary", "arbitrary")),
        name="ada",
    )(c_all, w_ada, b_ada.reshape(DEPTH, 1, 3 * D_MODEL))


IN_COLS = (0, MLA_IN_PAD, MLA_IN_PAD + RWKV_IN, MLA_IN_PAD + RWKV_IN + HGRN_IN, IN_PAD)


def _in_proj_kernel(x_ref, mod_ref, g_ref, w_ref, *rest, ns, seg, per_seq, first):
    mod = mod_ref[...]
    shift = _rows(mod[:, :, 0:D_MODEL], ns, seg)
    scale = _rows(mod[:, :, D_MODEL:2 * D_MODEL], ns, seg)
    h = (_rms(x_ref[...], g_ref[...]) * (1.0 + scale) + shift).astype(BF16)
    proj = lambda j: lax.dot_general(h, w_ref[:, IN_COLS[j]:IN_COLS[j + 1]], NN, preferred_element_type=F32)
    if not per_seq:
        for j, o_ref in enumerate(rest):
            o_ref[...] = proj(j).astype(o_ref.dtype)
        return
    shift_ref, *rest = rest
    vres = None
    if not first:
        vres, rest = rest[:4], rest[4:]
    consts, (o_mla, o_hgrn, o_gate, feat_ref), rest = rest[:6], rest[6:10], rest[10:]
    if first:
        vfirst_out, *rest = rest
    tail_out, prev_s = rest
    o_mla[...] = proj(0)
    o_hgrn[...] = proj(2)
    o_gate[...] = proj(3).astype(o_gate.dtype)
    p = proj(1)
    row = lax.broadcasted_iota(jnp.int32, p.shape, 0)
    before = jnp.where(pl.program_id(0) % per_seq == 0, shift_ref[0], prev_s[...])
    prev = jnp.where(row == 0, jnp.broadcast_to(before, p.shape), pltpu.roll(p, 1, axis=0))
    feats, v_raw = _rwkv_features(p, prev, consts, vres)
    if first:
        vfirst_out[...] = v_raw
    for j, t in enumerate(feats):
        feat_ref[:, j * RWKV_WIDTH:(j + 1) * RWKV_WIDTH] = t
    last = p[p.shape[0] - 1:, :]
    prev_s[...] = last
    tail_out[0] = last


def _in_proj(x, mod, g, w, seq_len, rwkv=None, tile=512):
    n = x.shape[0]
    tm, ns, seg = _seg_tiles(n, seq_len, tile)
    per_seq = seq_len // seg
    row = lambda wd: pl.BlockSpec((tm, wd), lambda i: (i, 0))
    full = lambda a: pl.BlockSpec(a.shape, lambda i: (0,) * a.ndim)
    ins = [x, mod, g, w]
    specs = [row(D_MODEL), pl.BlockSpec((ns, 1, 3 * D_MODEL), lambda i: (i // per_seq, 0, 0)), full(g), full(w)]
    if rwkv is None or ns != 1:
        outs = [(MLA_IN_PAD, F32), (RWKV_IN, F32), (HGRN_IN, F32), (D_MODEL, BF16)]
        return pl.pallas_call(
            functools.partial(_in_proj_kernel, ns=ns, seg=seg, per_seq=0, first=True),
            out_shape=[jax.ShapeDtypeStruct((n, wd), dt) for wd, dt in outs],
            grid=(n // tm,), in_specs=specs, out_specs=[row(wd) for wd, _ in outs],
            compiler_params=_cparams(("arbitrary",)), name="in_proj",
        )(*ins)
    shift_prev, consts, v_first, vres = rwkv
    first = v_first is None
    batch = n // seq_len
    per_batch = pl.BlockSpec((1, 1, RWKV_IN), lambda i: (i // per_seq, 0, 0))
    ins.append(shift_prev[:, None, :])
    specs.append(per_batch)
    if not first:
        ins += [v_first] + list(vres)
        specs += [row(RWKV_WIDTH)] + [full(a) for a in vres]
    ins += list(consts)
    specs += [full(a) for a in consts]
    outs = [(MLA_IN_PAD, F32), (HGRN_IN, F32), (D_MODEL, BF16), (RWKV_FEAT, F32)] + ([(RWKV_WIDTH, F32)] if first else [])
    res = pl.pallas_call(
        functools.partial(_in_proj_kernel, ns=ns, seg=seg, per_seq=per_seq, first=first),
        out_shape=[jax.ShapeDtypeStruct((n, wd), dt) for wd, dt in outs]
        + [jax.ShapeDtypeStruct((batch, 1, RWKV_IN), F32)],
        grid=(n // tm,), in_specs=specs, out_specs=[row(wd) for wd, _ in outs] + [per_batch],
        scratch_shapes=[pltpu.VMEM((1, RWKV_IN), F32)],
        compiler_params=_cparams(("arbitrary",)), name="in_proj_rwkv",
    )(*ins)
    p_mla, p_hgrn, gate, feat = res[:4]
    return p_mla, p_hgrn, gate, feat, (res[4] if first else v_first), res[-1][:, 0, :]


def _out_proj_kernel(om_ref, or_ref, oh_ref, gate_ref, x_ref, mod_ref, g_ref, w_ref, o_ref, *, ns, seg):
    sg = _silu(gate_ref[...].astype(F32))
    a0, a1 = MLA_WIDTH, MLA_WIDTH + RWKV_WIDTH
    acc = _bdot(om_ref[...] * sg[:, 0:a0], w_ref[0:a0, :])
    acc += _bdot(or_ref[...] * sg[:, a0:a1], w_ref[a0:a1, :])
    acc += _bdot(oh_ref[...] * sg[:, a1:], w_ref[a1:, :])
    a_gate = _rows(mod_ref[...][:, :, 2 * D_MODEL:], ns, seg)
    o_ref[...] = x_ref[...] + a_gate * _rms(acc, g_ref[...])


def _out_proj(o_mla, o_rwkv, o_hgrn, gate, x, mod, g, w, seq_len, tile=512):
    n = x.shape[0]
    tm, ns, seg = _seg_tiles(n, seq_len, tile)
    per_seq = seq_len // seg
    row = lambda wd: pl.BlockSpec((tm, wd), lambda i: (i, 0))
    return pl.pallas_call(
        functools.partial(_out_proj_kernel, ns=ns, seg=seg),
        out_shape=jax.ShapeDtypeStruct((n, D_MODEL), F32),
        grid=(n // tm,),
        in_specs=[
            row(MLA_WIDTH), row(RWKV_WIDTH), row(HGRN_WIDTH), row(D_MODEL), row(D_MODEL),
            pl.BlockSpec((ns, 1, 3 * D_MODEL), lambda i: (i // per_seq, 0, 0)),
            pl.BlockSpec((1, D_MODEL), lambda i: (0, 0)),
            pl.BlockSpec((D_MODEL, D_MODEL), lambda i: (0, 0)),
        ],
        out_specs=row(D_MODEL),
        compiler_params=_cparams(("arbitrary",)),
        name="out_proj",
    )(o_mla, o_rwkv, o_hgrn, gate, x, mod, g, w)


def _rope_tile(x, cos, sin):
    return x * cos + pltpu.roll(x, LANE - MLA_ROPE // 2, axis=1) * sin


def _mla_prep_kernel(p_ref, *refs, absorbed):
    _mla_stage(p_ref[...], *refs, absorbed=absorbed)


def _mla_stage(p, cos_ref, sin_ref, qg_ref, kvg_ref, wq_ref, *rest, absorbed):
    cos, sin = cos_ref[...], sin_ref[...]
    cq = _rms(p[:, 0:MLA_Q_RANK], qg_ref[...])
    ckv = _rms(p[:, MLA_Q_RANK:MLA_Q_RANK + MLA_KV_RANK], kvg_ref[...])
    kpe = _rope_tile(p[:, MLA_Q_RANK + MLA_KV_RANK:], cos, sin)
    q = _bdot(cq, wq_ref[...])
    if absorbed:
        wuk_ref, qx_out, ckv_out, kpe_out = rest
        ext = wuk_ref.shape[2]
        for h in range(MLA_HEADS):
            qh = _rope_tile(q[:, h * QK_PAD:(h + 1) * QK_PAD], cos, sin)
            qx_out[:, h * ext:(h + 1) * ext] = _bdot(qh, wuk_ref[h]).astype(BF16)
    else:
        wk_ref, wv_ref, q_out, k_out, v_out, ckv_out, kpe_out = rest
        kn = _bdot(ckv, wk_ref[...])
        for h in range(MLA_HEADS):
            sl = slice(h * QK_PAD, (h + 1) * QK_PAD)
            q_out[:, sl] = _rope_tile(q[:, sl], cos, sin).astype(BF16)
            k_out[:, sl] = (kn[:, sl] + kpe).astype(BF16)
        v = _bdot(ckv, wv_ref[...])
        col = lax.broadcasted_iota(jnp.int32, v.shape, 1)
        ones_col = jnp.where((col // QK_PAD) % 2 == 0, MLA_V, 0)
        v_out[...] = jnp.where(col % QK_PAD == ones_col, 1.0, v).astype(BF16)
    ckv_out[...] = ckv
    kpe_out[...] = kpe[:, PE_LO:PE_LO + MLA_ROPE]


def _mla_prep(p_mla, cos, sin, qg, kvg, wq, extra_w, seq_len, absorbed, tile=512):
    n = p_mla.shape[0]
    tm = min(tile, n)
    rep = max(seq_len // tm, 1)
    hw = MLA_HEADS * QK_PAD
    row = lambda wd: pl.BlockSpec((tm, wd), lambda i: (i, 0))
    full = lambda a: pl.BlockSpec(a.shape, lambda i: (0,) * a.ndim)
    tab = pl.BlockSpec((tm, LANE), lambda i: (i % rep, 0))
    if absorbed:
        ext = extra_w[0].shape[2]
        outs = [(MLA_HEADS * ext, BF16), (MLA_KV_RANK, F32), (MLA_ROPE, F32)]
    else:
        outs = [(hw, BF16), (hw, BF16), (hw, BF16), (MLA_KV_RANK, F32), (MLA_ROPE, F32)]
    return pl.pallas_call(
        functools.partial(_mla_prep_kernel, absorbed=absorbed),
        out_shape=[jax.ShapeDtypeStruct((n, wd), dt) for wd, dt in outs],
        grid=(n // tm,),
        in_specs=[row(MLA_IN_PAD), tab, tab, full(qg), full(kvg), full(wq)] + [full(a) for a in extra_w],
        out_specs=[row(wd) for wd, _ in outs],
        compiler_params=_cparams(("arbitrary",)),
        name="mla_prep_absorbed" if absorbed else "mla_prep",
    )(p_mla, cos, sin, qg, kvg, wq, *extra_w)


def _softmax_step(s, v, m, l, acc):
    c = MLA_SCALE * math.log2(math.e)
    m_new = jnp.maximum(m, jnp.max(s, axis=-1, keepdims=True))
    alpha = jnp.exp2((m - m_new) * c)
    p = jnp.exp2((s - m_new) * c)
    l = alpha * l + jnp.sum(p, axis=-1, keepdims=True)
    acc = alpha * acc + lax.dot_general(p.astype(BF16), v, NN, preferred_element_type=F32)
    return m_new, l, acc


def _softmax_steps(ss, vs, ms, accs):
    c = MLA_SCALE * math.log2(math.e)
    m_new = [jnp.maximum(m, jnp.max(s, axis=-1, keepdims=True)) for m, s in zip(ms, ss)]
    alpha = [jnp.exp2((m - mn) * c) for m, mn in zip(ms, m_new)]
    ps = [jnp.exp2((s - mn) * c).astype(BF16) for s, mn in zip(ss, m_new)]
    pv = [lax.dot_general(p, v, NN, preferred_element_type=F32) for p, v in zip(ps, vs)]
    accs = [a * acc + x for a, acc, x in zip(alpha, accs, pv)]
    return m_new, accs


def _attn_kernel(q_ref, k_ref, v_ref, o_ref, *, tq, heads):
    i = pl.program_id(2)
    sls = [slice(h * QK_PAD, (h + 1) * QK_PAD) for h in range(heads)]
    qs = [q_ref[:, sl] for sl in sls]

    def scores(j):
        rows = pl.ds(pl.multiple_of(j * tq, tq), tq)
        ss = [lax.dot_general(q, k_ref[rows, sl], NT, preferred_element_type=F32) for q, sl in zip(qs, sls)]
        return ss, [v_ref[rows, sl] for sl in sls]

    def body(j, carry):
        ss, vs = scores(j)
        return _softmax_steps(ss, vs, *carry)

    init = ([jnp.full((tq, 1), MASK_VALUE, F32)] * heads, [jnp.zeros((tq, LANE), F32)] * heads)
    carry = lax.fori_loop(0, i, body, init)
    ss, vs = scores(i)
    r = lax.broadcasted_iota(jnp.int32, (tq, tq), 0)
    c = lax.broadcasted_iota(jnp.int32, (tq, tq), 1)
    _, accs = _softmax_steps([jnp.where(c <= r, s, MASK_VALUE) for s in ss], vs, *carry)
    low = lax.broadcasted_iota(jnp.int32, (tq, LANE), 1) < MLA_V
    for hp in range(heads // 2):
        even, odd = accs[2 * hp], accs[2 * hp + 1]
        o_pair = jnp.where(low, even / even[:, MLA_V:MLA_V + 1], odd / odd[:, 0:1])
        o_ref[:, hp * LANE:(hp + 1) * LANE] = o_pair.astype(o_ref.dtype)


def _attention_prompt(q, k, v, batch, seq_len, tq=512, heads=4):
    n = q.shape[0]
    tq = min(tq, seq_len)
    nq = seq_len // tq
    wide = heads * QK_PAD
    return pl.pallas_call(
        functools.partial(_attn_kernel, tq=tq, heads=heads),
        out_shape=jax.ShapeDtypeStruct((n, MLA_WIDTH), _mixer_dtype(seq_len)),
        grid=(batch, MLA_HEADS // heads, nq),
        in_specs=[
            pl.BlockSpec((tq, wide), lambda b, h, i: (b * nq + i, h)),
            pl.BlockSpec((seq_len, wide), lambda b, h, i: (b, h)),
            pl.BlockSpec((seq_len, wide), lambda b, h, i: (b, h)),
        ],
        out_specs=pl.BlockSpec((tq, wide // 2), lambda b, h, i: (b * nq + i, h)),
        compiler_params=_cparams(("arbitrary", "arbitrary", "arbitrary")),
        name="attn_prompt",
    )(q, k, v)


def _attn_paged_kernel(pt_ref, q_ref, ckv_new_ref, kpe_new_ref, wuv_ref, ckv_hbm, kpe_hbm, o_ref,
                       ckv_buf, kpe_buf, sem, *, pages, group, lookahead, t_new, layer):
    b = pl.program_id(0)
    slot = b % 2
    rows = MLA_HEADS * t_new

    def page_copies(elem, slot_idx, lookup):
        out = []
        for pg in range(pages):
            page = pt_ref[elem * pages + pg] if lookup else 0
            out.append(pltpu.make_async_copy(ckv_hbm.at[layer, page], ckv_buf.at[slot_idx, pg], sem.at[slot_idx, 0]))
            out.append(pltpu.make_async_copy(kpe_hbm.at[layer, page], kpe_buf.at[slot_idx, pg], sem.at[slot_idx, 1]))
        return out

    @pl.when(b == 0)
    def _():
        for cp in page_copies(0, 0, True):
            cp.start()

    @pl.when(b + 1 < pl.num_programs(0))
    def _():
        for cp in page_copies(b + 1, 1 - slot, True):
            cp.start()

    for cp in page_copies(b, slot, False):
        cp.wait()

    q = q_ref[0]
    q_lat = q[:, 0:MLA_KV_RANK]
    q_pe = q[:, MLA_KV_RANK:MLA_KV_RANK + MLA_ROPE]
    def scores(n):
        pg = n * group
        ckv = jnp.concatenate([ckv_buf[slot, pg + i].astype(BF16) for i in range(group)], axis=0)
        kpe_t = jnp.concatenate([kpe_buf[slot, pg + i].astype(BF16) for i in range(group)], axis=1)
        return (lax.dot_general(q_lat, ckv, NT, preferred_element_type=F32)
                + lax.dot_general(q_pe, kpe_t, NN, preferred_element_type=F32)), ckv

    ckv = ckv_new_ref[0].astype(BF16)
    s = lax.dot_general(q_lat, ckv, NT, preferred_element_type=F32)
    s += lax.dot_general(q_pe, kpe_new_ref[0].astype(BF16), NT, preferred_element_type=F32)
    t_q = lax.broadcasted_iota(jnp.int32, (rows, t_new), 0) % t_new
    t_k = lax.broadcasted_iota(jnp.int32, (rows, t_new), 1)
    n_chunks = pages // group
    ahead = [scores(n) for n in range(min(lookahead, n_chunks))]
    m, l, acc = _softmax_step(jnp.where(t_k <= t_q, s, MASK_VALUE), ckv, jnp.full((rows, 1), MASK_VALUE, F32),
                              jnp.zeros((rows, 1), F32), jnp.zeros((rows, MLA_KV_RANK), F32))
    for n in range(n_chunks):
        s, ckv = ahead.pop(0)
        if n + lookahead < n_chunks:
            ahead.append(scores(n + lookahead))
        m, l, acc = _softmax_step(s, ckv, m, l, acc)
    lat = acc / l
    for h in range(MLA_HEADS):
        o_ref[0, h * t_new:(h + 1) * t_new, :] = _bdot(lat[h * t_new:(h + 1) * t_new, :], wuv_ref[h])


def _attention_paged(q_ext, ckv_new, kpe_new, w_uv, cache_ckv, cache_kpe, page_table, layer, group=16, lookahead=4):
    batch, rows, ext = q_ext.shape
    t_new = rows // MLA_HEADS
    pages = page_table.shape[1]
    group = min(group, pages)
    assert pages % group == 0
    kpe_t = jnp.swapaxes(cache_kpe, 2, 3)
    grid_spec = pltpu.PrefetchScalarGridSpec(
        num_scalar_prefetch=1,
        grid=(batch,),
        in_specs=[
            pl.BlockSpec((1, rows, ext), lambda b, pt: (b, 0, 0)),
            pl.BlockSpec((1, t_new, MLA_KV_RANK), lambda b, pt: (b, 0, 0)),
            pl.BlockSpec((1, t_new, MLA_ROPE), lambda b, pt: (b, 0, 0)),
            pl.BlockSpec(w_uv.shape, lambda b, pt: (0, 0, 0)),
            pl.BlockSpec(memory_space=pl.ANY),
            pl.BlockSpec(memory_space=pl.ANY),
        ],
        out_specs=pl.BlockSpec((1, rows, MLA_V), lambda b, pt: (b, 0, 0)),
        scratch_shapes=[
            pltpu.VMEM((2, pages, PAGE_SIZE, MLA_KV_RANK), F32),
            pltpu.VMEM((2, pages, MLA_ROPE, PAGE_SIZE), F32),
            pltpu.SemaphoreType.DMA((2, 2)),
        ],
    )
    return pl.pallas_call(
        functools.partial(_attn_paged_kernel, pages=pages, group=group, lookahead=lookahead, t_new=t_new,
                          layer=layer),
        out_shape=jax.ShapeDtypeStruct((batch, rows, MLA_V), F32),
        grid_spec=grid_spec,
        compiler_params=_cparams(("arbitrary",)),
        name="attn_paged",
    )(page_table.reshape(-1), q_ext, ckv_new, kpe_new, w_uv, cache_ckv, kpe_t)


def _rwkv_prep_kernel(p_ref, shift_ref, *rest, ns, seg, per_seq, first):
    if per_seq:
        tail_ref, *rest = rest
    mu_ref, w2_ref, wa0_ref, kk_ref, ka_ref, rk_ref, *rest = rest
    if first:
        feat_ref, vfirst_out = rest
    else:
        vfirst_ref, v0_ref, wva_ref, wvb_ref, feat_ref = rest
    p = p_ref[...]
    row = lax.broadcasted_iota(jnp.int32, p.shape, 0)
    before = shift_ref[...]
    if per_seq:
        before = jnp.where(pl.program_id(0) % per_seq == 0, before, tail_ref[7:8, :][None])
    prev = jnp.where(row % seg == 0, _rows(before, ns, seg), pltpu.roll(p, 1, axis=0))
    consts = (mu_ref, w2_ref, wa0_ref, kk_ref, ka_ref, rk_ref)
    vres = None if first else (vfirst_ref, v0_ref, wva_ref, wvb_ref)
    feats, v_raw = _rwkv_features(p, prev, consts, vres)
    if first:
        vfirst_out[...] = v_raw
    for j, t in enumerate(feats):
        feat_ref[:, j * RWKV_WIDTH:(j + 1) * RWKV_WIDTH] = t


def _rwkv_features(p, prev, consts, vres):
    mu_ref, w2_ref, wa0_ref, kk_ref, ka_ref, rk_ref = consts
    w = RWKV_WIDTH
    pm = p + mu_ref[...] * (prev - p)
    r, k, v, xwa = pm[:, 0:w], pm[:, w:2 * w], pm[:, 2 * w:3 * w], pm[:, 3 * w:]
    lane = lax.broadcasted_iota(jnp.int32, xwa.shape, 1)
    da = _bdot(jnp.where(lane < DECAY_RANK, jnp.tanh(xwa), xwa), w2_ref[...]) + wa0_ref[...]
    z = -da[:, 0:w]
    softplus = jnp.maximum(z, 0.0) + jnp.log1p(jnp.exp(-jnp.abs(z)))
    log_decay = -jnp.exp(-softplus - 0.5)
    a = jax.nn.sigmoid(da[:, w:])
    v_raw = v
    if vres is not None:
        vfirst_ref, v0_ref, wva_ref, wvb_ref = vres
        mix = jax.nn.sigmoid(v0_ref[...] + _bdot(_bdot(v, wva_ref[...]), wvb_ref[...]))
        v = v + (vfirst_ref[...] - v) * mix
    kk = k * kk_ref[...]
    k = k * (1.0 + (a - 1.0) * ka_ref[...])
    ones = _head_ones(w, RWKV_HEAD)
    kk = kk / jnp.maximum(jnp.sqrt(_bdot(kk * kk, ones)), 1e-12)
    bonus = _bdot(r * k * rk_ref[...], ones) * v
    return (r, log_decay, k, v, -kk, kk * a, bonus), v_raw


def _rwkv_prep(p_rwkv, shift_prev, consts, v_first, vres, seq_len, tile=256):
    n = p_rwkv.shape[0]
    tm, ns, seg = _seg_tiles(n, seq_len, tile)
    per_seq = seq_len // seg if ns == 1 else 0
    assert per_seq or seg == seq_len
    first = v_first is None
    row = lambda wd: pl.BlockSpec((tm, wd), lambda i: (i, 0))
    full = lambda a: pl.BlockSpec(a.shape, lambda i: (0,) * a.ndim)
    ins = [p_rwkv, shift_prev[:, None, :]]
    specs = [row(RWKV_IN), pl.BlockSpec((ns, 1, RWKV_IN), lambda i: (i // max(per_seq, 1), 0, 0))]
    if per_seq:
        sub_tiles = tm // 8
        ins.append(p_rwkv)
        specs.append(pl.BlockSpec((8, RWKV_IN), lambda i: (jnp.maximum(i * sub_tiles - 1, 0), 0)))
    ins += list(consts)
    specs += [full(a) for a in consts]
    feat = jax.ShapeDtypeStruct((n, RWKV_FEAT), F32)
    if first:
        out_shape = [feat, jax.ShapeDtypeStruct((n, RWKV_WIDTH), F32)]
        out_specs = [row(RWKV_FEAT), row(RWKV_WIDTH)]
    else:
        ins += [v_first] + list(vres)
        specs += [row(RWKV_WIDTH)] + [full(a) for a in vres]
        out_shape, out_specs = feat, row(RWKV_FEAT)
    return pl.pallas_call(
        functools.partial(_rwkv_prep_kernel, ns=ns, seg=seg, per_seq=per_seq, first=first),
        out_shape=out_shape,
        grid=(n // tm,),
        in_specs=specs,
        out_specs=out_specs,
        compiler_params=_cparams(("arbitrary",)),
        name="rwkv_prep",
    )(*ins)


def _head_blockdiag(x):
    m, width = x.shape
    blk = width // RWKV_HEADS
    rows = lax.broadcasted_iota(jnp.int32, (RWKV_HEADS * m, width), 0) // m
    cols = lax.broadcasted_iota(jnp.int32, (RWKV_HEADS * m, width), 1) // blk
    return jnp.where(rows == cols, jnp.concatenate([x] * RWKV_HEADS, axis=0), 0.0)


def _rwkv_chunk_lanes(r, lw_cum, lw, k, v, a, b, s0):
    c = r[0].shape[0]
    nh, hd = RWKV_HEADS, RWKV_HEAD
    mm = _bdot
    bd = _head_blockdiag
    each = lambda f, *xs: [f(*t) for t in zip(*xs)]
    ar = each(lambda a_, r_, cum, lw_: jnp.concatenate([a_ * jnp.exp(cum - lw_), r_ * jnp.exp(cum)], axis=0),
              a, r, lw_cum, lw)
    e_neg = [jnp.exp(-cum) for cum in lw_cum]
    g_b = each(lambda x, b_, e: mm(x, bd(b_ * e), NT), ar, b, e_neg)
    g_k = each(lambda x, k_, e: mm(x, bd(k_ * e), NT), ar, k, e_neg)
    gr = lax.broadcasted_iota(jnp.int32, (2 * c, nh * c), 0)
    gc = lax.broadcasted_iota(jnp.int32, (2 * c, nh * c), 1) % c
    keep = gc < gr - jnp.where(gr < c, 0, c - 1)
    g_b = [jnp.where(keep, x, 0.0) for x in g_b]
    g_k = [jnp.where(keep, x, 0.0) for x in g_k]
    a_ab = [x[0:c, :] for x in g_b]
    eye = (lax.broadcasted_iota(jnp.int32, (c, nh * c), 1) % c
           == lax.broadcasted_iota(jnp.int32, (c, nh * c), 0)).astype(F32)
    t_inv = [eye + x for x in a_ab]
    x = each(lambda a_: mm(a_, bd(a_)), a_ab)
    for _ in range(int(math.log2(c)) - 2):
        tx = each(lambda t_, x_: mm(jnp.concatenate([t_, x_], axis=0), bd(x_)), t_inv, x)
        t_inv = each(lambda t_, tx_: t_ + tx_[0:c, :], t_inv, tx)
        x = [tx_[c:, :] for tx_ in tx]
    t_inv = each(lambda t_, x_: t_ + mm(t_, bd(x_)), t_inv, x)
    ar_s0 = each(lambda x, s_: mm(x, bd(s_), NT), ar, s0)
    g_v = each(lambda g_, v_: mm(g_[0:c, :], bd(v_)), g_k, v)
    u = each(lambda t_, x, y: mm(t_, bd(x[0:c, :] + y)), t_inv, ar_s0, g_v)
    o = each(lambda x, gb_, gk_, u_, v_: x[c:, :] + mm(gb_[c:, :], bd(u_)) + mm(gk_[c:, :], bd(v_)),
             ar_s0, g_b, g_k, u, v)
    full = each(lambda u_, v_, b_, k_, cum: mm(
        jnp.concatenate([u_, v_], axis=0),
        jnp.concatenate([b_ * jnp.exp(cum[c - 1:c, :] - cum), k_ * jnp.exp(cum[c - 1:c, :] - cum)], axis=0), TN),
        u, v, b, k, lw_cum)
    lane_head = lax.broadcasted_iota(jnp.int32, (hd, nh * hd), 1) // hd

    def fold(f):
        out = jnp.where(lane_head == 0, f[0:hd, :], 0.0)
        for h in range(1, nh):
            out = out + jnp.where(lane_head == h, f[h * hd:(h + 1) * hd, :], 0.0)
        return out

    s1 = each(lambda s_, cum, f: s_ * jnp.exp(cum[c - 1:c, :]) + fold(f), s0, lw_cum, full)
    return o, s1


def _rwkv_scan_kernel(feat_ref, s0_ref, gn_g_ref, gn_b_ref, o_ref, s_out_ref, s_ref, *, ns, chunk):
    j = pl.program_id(1)
    w = RWKV_WIDTH

    @pl.when(j == 0)
    def _():
        s_ref[...] = s0_ref[...]

    tri = (lax.broadcasted_iota(jnp.int32, (chunk, chunk), 1)
           <= lax.broadcasted_iota(jnp.int32, (chunk, chunk), 0)).astype(BF16)

    feats = [feat_ref[q] for q in range(ns)]
    cums = [_dot_exact_lhs(tri, f[:, w:2 * w]) for f in feats]
    part = lambda i: [f[:, i * w:(i + 1) * w] for f in feats]
    o, s1 = _rwkv_chunk_lanes(part(0), cums, part(1), part(2), part(3), part(4), part(5),
                              [s_ref[q] for q in range(ns)])
    ones = _head_ones(w, RWKV_HEAD)
    seg_mean = lambda x: (_bdot(x, ones) + _bdot(x - x.astype(BF16).astype(F32), ones)) * (1.0 / RWKV_HEAD)
    for q in range(ns):
        s_ref[q] = s1[q]
        dev = o[q] - seg_mean(o[q])
        o_q = dev * lax.rsqrt(seg_mean(dev * dev) + RWKV_GN_EPS)
        o_ref[q] = (o_q * gn_g_ref[...] + gn_b_ref[...] + feats[q][:, 6 * w:]).astype(o_ref.dtype)

    @pl.when(j == pl.num_programs(1) - 1)
    def _():
        s_out_ref[...] = s_ref[...]


def _rwkv_scan(feat, s0, gn_g, gn_b, seq_len, ns=16):
    n = feat.shape[0]
    batch = n // seq_len
    chunk = min(RWKV_CHUNK, seq_len)
    nc = seq_len // chunk
    ns = min(ns, batch)
    assert batch % ns == 0 and seq_len % chunk == 0
    st = (ns, RWKV_HEAD, RWKV_WIDTH)
    s0_lanes = jnp.transpose(s0, (0, 2, 1, 3)).reshape(batch, RWKV_HEAD, RWKV_WIDTH)
    o, s1 = pl.pallas_call(
        functools.partial(_rwkv_scan_kernel, ns=ns, chunk=chunk),
        out_shape=[jax.ShapeDtypeStruct((batch, seq_len, RWKV_WIDTH), _mixer_dtype(seq_len)),
                   jax.ShapeDtypeStruct(s0_lanes.shape, F32)],
        grid=(batch // ns, nc),
        in_specs=[
            pl.BlockSpec((ns, chunk, RWKV_FEAT), lambda b, j: (b, j, 0)),
            pl.BlockSpec(st, lambda b, j: (b, 0, 0)),
            pl.BlockSpec((1, RWKV_WIDTH), lambda b, j: (0, 0)),
            pl.BlockSpec((1, RWKV_WIDTH), lambda b, j: (0, 0)),
        ],
        out_specs=[
            pl.BlockSpec((ns, chunk, RWKV_WIDTH), lambda b, j: (b, j, 0)),
            pl.BlockSpec(st, lambda b, j: (b, 0, 0)),
        ],
        scratch_shapes=[pltpu.VMEM(st, F32)],
        compiler_params=_cparams(("arbitrary", "arbitrary")),
        name="rwkv_scan",
    )(feat.reshape(batch, seq_len, RWKV_FEAT), s0_lanes, gn_g, gn_b)
    s1 = jnp.transpose(s1.reshape(batch, RWKV_HEAD, RWKV_HEADS, RWKV_HEAD), (0, 2, 1, 3))
    return o.reshape(n, RWKV_WIDTH), s1


def _hgrn_scan_kernel(p_ref, lb_ref, ng_ref, s0_ref, o_ref, s_out_ref, s_ref, q_s, bk_s, v_s, b_s, *, ns, nsub, sub):
    j = pl.program_id(1)
    w = HGRN_WIDTH
    tile = ns * nsub * sub

    @pl.when(j == 0)
    def _():
        s_ref[...] = s0_ref[...]

    p = p_ref[...]
    lb = lb_ref[...]
    sig = jax.nn.sigmoid(p[:, w:2 * w])
    q_s[...] = _silu(p[:, 0:w]) * HGRN_HEAD ** -0.5
    k = (1.0 - lb) * (1.0 - sig)
    v_s[...] = p[:, 2 * w:]
    log_f = jnp.log(jnp.maximum(lb + (1.0 - lb) * sig, MIN_FORGET))
    tr = lax.broadcasted_iota(jnp.int32, (tile, tile), 0)
    tc = lax.broadcasted_iota(jnp.int32, (tile, tile), 1)
    tri = ((tr // sub == tc // sub) & (tc <= tr)).astype(BF16)
    b = _dot_exact_lhs(tri, log_f) * math.log2(math.e)
    b_s[...] = b
    bk_s[...] = b - jnp.log2(k)
    ones = _head_ones(w, HGRN_HEAD)
    sub_row = lax.broadcasted_iota(jnp.int32, (sub, w), 0)
    nsc = ns * nsub
    blk = lambda ref, c: ref[c * sub:(c + 1) * sub, :]
    row = lambda ref, r: ref[r:r + 1, :]
    heads = [slice(h * HGRN_HEAD, (h + 1) * HGRN_HEAD) for h in range(HGRN_HEADS)]
    pair, spans = [], []
    for c in range(nsc):
        for s in range(sub):
            t0 = c * sub + (s // 8) * 8
            t1 = (c + 1) * sub
            pair.append(q_s[t0:t1, :] * jnp.exp2(jnp.minimum(b_s[t0:t1, :] - row(bk_s, c * sub + s), 0.0)))
            spans.append(t1 - t0)
    att = _bdot(jnp.concatenate(pair, axis=0), ones)
    intra, off = [], 0
    for c in range(nsc):
        parts = [jnp.zeros((8, w), F32)] * (sub // 8)
        for s in range(sub):
            n = spans[c * sub + s]
            t_row = lax.broadcasted_iota(jnp.int32, (n, w), 0) + (sub - n)
            a = jnp.where(t_row >= s, att[off:off + n, :], 0.0) * row(v_s, c * sub + s)
            g0 = (sub - n) // 8
            for g in range(g0, sub // 8):
                parts[g] = parts[g] + a[(g - g0) * 8:(g - g0 + 1) * 8, :]
            off += n
        intra.append(parts[0] if len(parts) == 1 else jnp.concatenate(parts, axis=0))
    last = [row(b_s, (c + 1) * sub - 1) for c in range(nsc)]
    kv = [[_bdot(blk(v_s, c)[:, hs], jnp.exp2(last[c] - blk(bk_s, c))[:, hs], TN) for hs in heads]
          for c in range(nsc)]
    states = []
    for seq in range(ns):
        st = [s_ref[seq, h] for h in range(HGRN_HEADS)]
        for c in range(seq * nsub, (seq + 1) * nsub):
            states.append(st)
            gamma = jnp.exp2(last[c])
            st = [st[h] * gamma[:, heads[h]] + kv[c][h] for h in range(HGRN_HEADS)]
        for h in range(HGRN_HEADS):
            s_ref[seq, h] = st[h]
    for c in range(nsc):
        q_in = blk(q_s, c) * jnp.exp2(blk(b_s, c))
        o = intra[c] + jnp.concatenate([_bdot(q_in[:, heads[h]], states[c][h], NT) for h in range(HGRN_HEADS)], axis=1)
        intra[c] = o
    o = jnp.concatenate(intra, axis=0)
    ms = _bdot(o * o, ones) * (1.0 / HGRN_HEAD)
    o_ref[...] = (o * lax.rsqrt(ms + NORM_EPS) * ng_ref[...]).astype(o_ref.dtype)

    @pl.when(j == pl.num_programs(1) - 1)
    def _():
        s_out_ref[...] = s_ref[...]


def _hgrn_scan(p_hgrn, lb, norm_g, s0, seq_len, tile=256, ns_max=16):
    n = p_hgrn.shape[0]
    batch = n // seq_len
    sub = min(HGRN_SUB, seq_len)
    if seq_len >= tile:
        ns, nsub = 1, tile // sub
    else:
        ns, nsub = min(ns_max, batch), seq_len // sub
    rows = ns * nsub * sub
    nt = seq_len // (nsub * sub)
    assert batch % ns == 0 and seq_len % (nsub * sub) == 0
    st = (ns, HGRN_HEADS, HGRN_HEAD, HGRN_HEAD)
    return pl.pallas_call(
        functools.partial(_hgrn_scan_kernel, ns=ns, nsub=nsub, sub=sub),
        out_shape=[jax.ShapeDtypeStruct((n, HGRN_WIDTH), _mixer_dtype(seq_len)), jax.ShapeDtypeStruct(s0.shape, F32)],
        grid=(batch // ns, nt),
        in_specs=[
            pl.BlockSpec((rows, HGRN_IN), lambda b, j: (b * nt + j, 0)),
            pl.BlockSpec((1, HGRN_WIDTH), lambda b, j: (0, 0)),
            pl.BlockSpec((1, HGRN_WIDTH), lambda b, j: (0, 0)),
            pl.BlockSpec(st, lambda b, j: (b, 0, 0, 0)),
        ],
        out_specs=[
            pl.BlockSpec((rows, HGRN_WIDTH), lambda b, j: (b * nt + j, 0)),
            pl.BlockSpec(st, lambda b, j: (b, 0, 0, 0)),
        ],
        scratch_shapes=[pltpu.VMEM(st, F32)] + [pltpu.VMEM((rows, HGRN_WIDTH), F32)] * 4,
        compiler_params=_cparams(("arbitrary", "arbitrary")),
        name="hgrn_scan",
    )(p_hgrn, lb, norm_g, s0)


def _layer_weights(P, l):
    w_in = P['w_in'][l]
    c_q, c_kv = MLA_Q_RANK, MLA_Q_RANK + MLA_KV_RANK
    c_pe = c_kv + MLA_ROPE
    half = MLA_ROPE // 2
    tail = QK_PAD - PE_LO - MLA_ROPE - half
    w_pe = w_in[:, c_kv:c_pe]
    kpe_tile = jnp.concatenate(
        [jnp.zeros((D_MODEL, PE_LO), F32), w_pe, w_pe[:, :half], jnp.zeros((D_MODEL, tail), F32)], axis=1)
    w_in_pad = jnp.concatenate([w_in[:, :c_kv], kpe_tile, w_in[:, c_pe:]], axis=1).astype(BF16)

    hd = MLA_NOPE + MLA_ROPE
    wq = P['mla_w_q_b'][l].reshape(MLA_Q_RANK, MLA_HEADS, hd)
    wq = jnp.concatenate([wq, wq[:, :, MLA_NOPE:MLA_NOPE + half], jnp.zeros((MLA_Q_RANK, MLA_HEADS, tail), F32)], axis=2)
    wq = wq.reshape(MLA_Q_RANK, MLA_HEADS * QK_PAD).astype(BF16)
    wkv = P['mla_w_kv_b'][l].reshape(MLA_KV_RANK, MLA_HEADS, MLA_NOPE + MLA_V)
    w_uk, w_uv = wkv[..., :MLA_NOPE], wkv[..., MLA_NOPE:]
    wk = jnp.pad(w_uk, ((0, 0), (0, 0), (0, QK_PAD - MLA_NOPE))).reshape(MLA_KV_RANK, MLA_HEADS * QK_PAD).astype(BF16)
    wv = jnp.zeros((MLA_KV_RANK, MLA_HEADS, QK_PAD), F32)
    for h in range(MLA_HEADS):
        off = (h % 2) * MLA_V
        wv = wv.at[:, h, off:off + MLA_V].set(w_uv[:, h, :])
    wv = wv.reshape(MLA_KV_RANK, MLA_HEADS * QK_PAD).astype(BF16)
    ext = MLA_KV_RANK + LANE
    wuk_ext = jnp.zeros((MLA_HEADS, QK_PAD, ext), F32)
    wuk_ext = wuk_ext.at[:, :MLA_NOPE, :MLA_KV_RANK].set(jnp.transpose(w_uk, (1, 2, 0)))
    wuk_ext = wuk_ext.at[:, PE_LO:PE_LO + MLA_ROPE, MLA_KV_RANK:MLA_KV_RANK + MLA_ROPE].set(
        jnp.broadcast_to(jnp.eye(MLA_ROPE, dtype=F32), (MLA_HEADS, MLA_ROPE, MLA_ROPE)))
    wuk_ext = wuk_ext.astype(BF16)
    w_uv_h = jnp.transpose(w_uv, (1, 0, 2)).astype(BF16)

    w = RWKV_WIDTH
    w2 = jnp.zeros((DECAY_RANK + ICLR_RANK, 2 * w), F32)
    w2 = w2.at[:DECAY_RANK, :w].set(P['rwkv_w_decay_b'][l]).at[DECAY_RANK:, w:].set(P['rwkv_w_iclr_b'][l])
    wa0 = jnp.concatenate([P['rwkv_w0'][l], P['rwkv_a0'][l]])[None, :]
    row = lambda a: a.reshape(1, -1)
    rwkv_consts = (row(P['rwkv_mu'][l]), w2, wa0, row(P['rwkv_k_k'][l]), row(P['rwkv_k_a'][l]), row(P['rwkv_r_k'][l]))
    vres = None
    if l > 0:
        wva = jnp.pad(P['rwkv_w_vres_a'][l - 1], ((0, 0), (0, LANE - VRES_RANK)))
        wvb = jnp.pad(P['rwkv_w_vres_b'][l - 1], ((0, LANE - VRES_RANK), (0, 0)))
        vres = (row(P['rwkv_v0'][l - 1]), wva, wvb)
    return dict(
        w_in=w_in_pad, pre_g=row(P['pre_norm_g'][l]), post_g=row(P['post_norm_g'][l]),
        w_out=P['w_out'][l].astype(BF16),
        q_g=row(P['mla_q_norm_g'][l]), kv_g=row(P['mla_kv_norm_g'][l]),
        wq=wq, wk=wk, wv=wv, wuk_ext=wuk_ext, w_uv=w_uv_h,
        rwkv_consts=rwkv_consts, vres=vres,
        gn_g=row(P['rwkv_gn_g'][l]), gn_b=row(P['rwkv_gn_b'][l]),
        hgrn_g=jnp.tile(P['hgrn_norm_g'][l], HGRN_HEADS)[None, :],
    )


def _rope_tables(pos, rows):
    half = MLA_ROPE // 2
    inv_freq = ROPE_BASE ** (-jnp.arange(half, dtype=F32) / half)
    ang = pos.astype(F32)[:, None] * inv_freq[None, :]
    cos, sin = jnp.cos(ang), jnp.sin(ang)
    t = pos.shape[0]
    ones, zeros = jnp.ones((t, PE_LO), F32), jnp.zeros((t, PE_LO), F32)
    pad = jnp.zeros((t, LANE - PE_LO - MLA_ROPE), F32)
    cos_t = jnp.concatenate([ones, cos, cos, pad], axis=1)
    sin_t = jnp.concatenate([zeros, -sin, sin, pad], axis=1)
    rep = rows // t
    return jnp.tile(cos_t, (rep, 1)), jnp.tile(sin_t, (rep, 1))


def _trunk(x, mods, pos, shift0, rwkv0, hgrn0, cache, LW, lb_all):
    batch, seq_len, _ = x.shape
    n = batch * seq_len
    x = x.reshape(n, D_MODEL)
    prep_tile = min(512, n)
    cos, sin = _rope_tables(pos, max(seq_len, prep_tile))
    v_first = None
    ckvs, kpes, rs, shs, hs = [], [], [], [], []
    for l in range(DEPTH):
        W = LW[l]
        mod = mods[l][:, None, :]
        rwkv_args = (shift0[l], W['rwkv_consts'], v_first, W['vres'])
        res = _in_proj(x, mod, W['pre_g'], W['w_in'], seq_len, rwkv=rwkv_args)
        if len(res) == 4:
            p_mla, p_rwkv, p_hgrn, gate = res
            if l == 0:
                feat, v_first = _rwkv_prep(p_rwkv, shift0[l], W['rwkv_consts'], None, None, seq_len)
            else:
                feat = _rwkv_prep(p_rwkv, shift0[l], W['rwkv_consts'], v_first, W['vres'], seq_len)
            shift_out = p_rwkv.reshape(batch, seq_len, RWKV_IN)[:, -1, :]
        else:
            p_mla, p_hgrn, gate, feat, v_first, shift_out = res
        if cache is None:
            q, k, v, ckv, kpe = _mla_prep(p_mla, cos, sin, W['q_g'], W['kv_g'], W['wq'], (W['wk'], W['wv']),
                                          seq_len, absorbed=False)
            o_mla = _attention_prompt(q, k, v, batch, seq_len)
        else:
            cache_ckv, cache_kpe, page_table = cache
            qx, ckv, kpe = _mla_prep(p_mla, cos, sin, W['q_g'], W['kv_g'], W['wq'], (W['wuk_ext'],),
                                     seq_len, absorbed=True)
            ext = qx.shape[1] // MLA_HEADS
            qx = qx.reshape(batch, seq_len, MLA_HEADS, ext).transpose(0, 2, 1, 3).reshape(batch, MLA_HEADS * seq_len, ext)
            o = _attention_paged(qx, ckv.reshape(batch, seq_len, MLA_KV_RANK), kpe.reshape(batch, seq_len, MLA_ROPE),
                                 W['w_uv'], cache_ckv, cache_kpe, page_table, l)
            o_mla = o.reshape(batch, MLA_HEADS, seq_len, MLA_V).transpose(0, 2, 1, 3).reshape(n, MLA_WIDTH)
        o_rwkv, s_r = _rwkv_scan(feat, rwkv0[l], W['gn_g'], W['gn_b'], seq_len)
        o_hgrn, s_h = _hgrn_scan(p_hgrn, lb_all[l][None, :], W['hgrn_g'], jnp.swapaxes(hgrn0[l], -1, -2), seq_len)
        s_h = jnp.swapaxes(s_h, -1, -2)
        x = _out_proj(o_mla, o_rwkv, o_hgrn, gate, x, mod, W['post_g'], W['w_out'], seq_len)
        ckvs.append(ckv.reshape(batch, seq_len, MLA_KV_RANK))
        kpes.append(kpe.reshape(batch, seq_len, MLA_ROPE))
        rs.append(s_r)
        shs.append(shift_out)
        hs.append(s_h)
    return (x.reshape(batch, seq_len, D_MODEL), jnp.stack(ckvs), jnp.stack(kpes), jnp.stack(rs), jnp.stack(shs),
            jnp.stack(hs))


def kernel(x_prompt, x_sample, c_prompt, c_sample, cache_ckv, cache_kpe, page_table, state_rwkv, state_rwkv_shift, state_hgrn, w_ada, b_ada, pre_norm_g, post_norm_g, w_in, mla_q_norm_g, mla_w_q_b, mla_kv_norm_g, mla_w_kv_b, rwkv_mu, rwkv_w0, rwkv_w_decay_b, rwkv_a0, rwkv_w_iclr_b, rwkv_k_k, rwkv_k_a, rwkv_r_k, rwkv_gn_g, rwkv_gn_b, rwkv_v0, rwkv_w_vres_a, rwkv_w_vres_b, hgrn_lb_raw, hgrn_norm_g, w_out):
    P = {
        'w_in': w_in, 'pre_norm_g': pre_norm_g, 'post_norm_g': post_norm_g,
        'mla_q_norm_g': mla_q_norm_g, 'mla_w_q_b': mla_w_q_b, 'mla_kv_norm_g': mla_kv_norm_g, 'mla_w_kv_b': mla_w_kv_b,
        'rwkv_mu': rwkv_mu, 'rwkv_w0': rwkv_w0, 'rwkv_w_decay_b': rwkv_w_decay_b, 'rwkv_a0': rwkv_a0,
        'rwkv_w_iclr_b': rwkv_w_iclr_b, 'rwkv_k_k': rwkv_k_k, 'rwkv_k_a': rwkv_k_a, 'rwkv_r_k': rwkv_r_k,
        'rwkv_gn_g': rwkv_gn_g, 'rwkv_gn_b': rwkv_gn_b, 'rwkv_v0': rwkv_v0, 'rwkv_w_vres_a': rwkv_w_vres_a,
        'rwkv_w_vres_b': rwkv_w_vres_b, 'hgrn_norm_g': hgrn_norm_g, 'w_out': w_out,
    }
    LW = [_layer_weights(P, l) for l in range(DEPTH)]
    lb_p = jax.nn.softmax(hgrn_lb_raw.astype(F32), axis=0)
    lb_all = jnp.cumsum(lb_p, axis=0) - lb_p[0]
    bp, tp = x_prompt.shape[:2]
    bs, ts = x_sample.shape[:2]
    mods = _ada(jnp.concatenate([c_prompt, c_sample], axis=0), w_ada, b_ada)
    past_len = page_table.shape[1] * PAGE_SIZE
    zeros = lambda *s: jnp.zeros(s, F32)
    out_p = _trunk(x_prompt, mods[:, :bp], jnp.arange(tp), zeros(DEPTH, bp, RWKV_IN),
                   zeros(DEPTH, bp, RWKV_HEADS, RWKV_HEAD, RWKV_HEAD), zeros(DEPTH, bp, HGRN_HEADS, HGRN_HEAD, HGRN_HEAD),
                   None, LW, lb_all)
    out_s = _trunk(x_sample, mods[:, bp:], past_len + jnp.arange(ts), state_rwkv_shift, state_rwkv, state_hgrn,
                   (cache_ckv, cache_kpe, page_table), LW, lb_all)
    return (out_p[0], out_s[0]) + out_p[1:] + out_s[1:]
```
